```python
import math
import jax
import jax.numpy as jnp
from jax import lax
import numpy as np

D_MODEL = 1024
BATCH = 4
SEQ = 4096
DEPTH = 1

GRID_W = 64
CTX_LEN = 256

MLA_HEADS = 8
MLA_NOPE = 64
MLA_ROPE = 32
MLA_QK = MLA_NOPE + MLA_ROPE
MLA_V = 64
Q_LORA = 256
KV_LORA = 128

SWA_HEADS = 8
SWA_KV_HEADS = 2
SWA_GROUP = SWA_HEADS // SWA_KV_HEADS
SWA_HD = 64
WINDOW = 128
BLOCK = 128

MIX_W = MLA_HEADS * MLA_V + SWA_HEADS * SWA_HD
IN_SIZES = (Q_LORA, KV_LORA, MLA_ROPE, SWA_HEADS * SWA_HD, SWA_KV_HEADS * SWA_HD, SWA_KV_HEADS * SWA_HD)
IN_W = Q_LORA + KV_LORA + MLA_ROPE + SWA_HEADS * SWA_HD + 2 * SWA_KV_HEADS * SWA_HD
D_FF = 4 * D_MODEL
ROPE_THETA = 10000.0
EPS = 1e-6
NEG_INF = -1e30
MLA_SCALE = 1.0 / math.sqrt(MLA_QK)
SWA_SCALE = 1.0 / math.sqrt(SWA_HD)

kernel_name = "hybrid_mla_swa_dit_block"


def rmsnorm(x, g):
    xf = x.astype(jnp.float32)
    y = xf * lax.rsqrt(jnp.mean(xf * xf, axis=-1, keepdims=True) + EPS)
    return (y * g.astype(jnp.float32)).astype(x.dtype)


def modulation(cond, w_mod, b_mod):
    m = jax.nn.silu(cond) @ w_mod + b_mod
    return jnp.split(m, 6, axis=-1)


def modulate(x, g, shift, scale):
    return rmsnorm(x, g) * (1.0 + scale) + shift


def axial_rope(rows, rot_dim):
    n_freq = rot_dim // 4
    inv = ROPE_THETA ** (-jnp.arange(n_freq, dtype=jnp.float32) / n_freq)
    row = jnp.repeat(jnp.arange(rows, dtype=jnp.float32), GRID_W)
    col = jnp.tile(jnp.arange(GRID_W, dtype=jnp.float32), rows)
    ang = jnp.concatenate([row[:, None] * inv, col[:, None] * inv], axis=-1)
    return jnp.cos(ang), jnp.sin(ang)


def apply_rope(x, cos, sin):
    half = x.shape[-1] // 2
    x1, x2 = x[..., :half], x[..., half:]
    c = cos[None, :, None, :].astype(x.dtype)
    s = sin[None, :, None, :].astype(x.dtype)
    return jnp.concatenate([x1 * c - x2 * s, x2 * c + x1 * s], axis=-1)


def project(h, w_in, g_q_a, w_uq, g_kv_a, w_ukv, g_mla_q, g_mla_k, g_swa_q, g_swa_k, rope):
    B, n, _ = h.shape
    z = h @ w_in
    offs = []
    acc = 0
    for sz in IN_SIZES[:-1]:
        acc += sz
        offs.append(acc)
    cq, ckv, kr, qs, ks, vs = jnp.split(z, offs, axis=-1)
    q_m = (rmsnorm(cq, g_q_a) @ w_uq).reshape(B, n, MLA_HEADS, MLA_QK)
    q_m = rmsnorm(q_m, g_mla_q)
    kv = (rmsnorm(ckv, g_kv_a) @ w_ukv).reshape(B, n, MLA_HEADS, MLA_NOPE + MLA_V)
    k_nope, v_m = kv[..., :MLA_NOPE], kv[..., MLA_NOPE:]
    k_rope = jnp.broadcast_to(kr[:, :, None, :], (B, n, MLA_HEADS, MLA_ROPE))
    k_m = rmsnorm(jnp.concatenate([k_nope, k_rope], axis=-1), g_mla_k)
    q_s = rmsnorm(qs.reshape(B, n, SWA_HEADS, SWA_HD), g_swa_q)
    k_s = rmsnorm(ks.reshape(B, n, SWA_KV_HEADS, SWA_HD), g_swa_k)
    v_s = vs.reshape(B, n, SWA_KV_HEADS, SWA_HD)
    if rope is not None:
        (cos_m, sin_m), (cos_s, sin_s) = rope
        q_m = jnp.concatenate([q_m[..., :MLA_NOPE], apply_rope(q_m[..., MLA_NOPE:], cos_m, sin_m)], axis=-1)
        k_m = jnp.concatenate([k_m[..., :MLA_NOPE], apply_rope(k_m[..., MLA_NOPE:], cos_m, sin_m)], axis=-1)
        q_s = apply_rope(q_s, cos_s, sin_s)
        k_s = apply_rope(k_s, cos_s, sin_s)
    return q_m, k_m, v_m, q_s, k_s, v_s


def mla_latent(q, k_lat, v_lat, k_ctx, v_ctx):
    B, S = q.shape[0], q.shape[1]
    nb = S // BLOCK
    k_all = jnp.concatenate([k_ctx, k_lat], axis=1)
    v_all = jnp.concatenate([v_ctx, v_lat], axis=1)
    qb = q.reshape(B, nb, BLOCK, MLA_HEADS, MLA_QK).transpose(1, 0, 2, 3, 4)

    def one_block(qblk):
        s = jnp.einsum('bqhd,bkhd->bhqk', qblk, k_all, preferred_element_type=jnp.float32) * MLA_SCALE
        p = jax.nn.softmax(s, axis=-1).astype(v_all.dtype)
        return jnp.einsum('bhqk,bkhd->bqhd', p, v_all)

    out = lax.map(one_block, qb)
    return out.transpose(1, 0, 2, 3, 4).reshape(B, S, MLA_HEADS * MLA_V)


def mla_context(q, k, v):
    B, n = q.shape[0], q.shape[1]
    s = jnp.einsum('bqhd,bkhd->bhqk', q, k, preferred_element_type=jnp.float32) * MLA_SCALE
    p = jax.nn.softmax(s, axis=-1).astype(v.dtype)
    return jnp.einsum('bhqk,bkhd->bqhd', p, v).reshape(B, n, MLA_HEADS * MLA_V)


def swa_latent(q, k, v, k_ctx, v_ctx, sink):
    B, S = q.shape[0], q.shape[1]
    nb = S // BLOCK
    n_ctx = k_ctx.shape[1]
    qb = q.reshape(B, nb, BLOCK, SWA_KV_HEADS, SWA_GROUP, SWA_HD)
    pad = ((0, 0), (BLOCK, BLOCK), (0, 0), (0, 0))
    kp = jnp.pad(k, pad).reshape(B, nb + 2, BLOCK, SWA_KV_HEADS, SWA_HD)
    vp = jnp.pad(v, pad).reshape(B, nb + 2, BLOCK, SWA_KV_HEADS, SWA_HD)
    kband = jnp.concatenate([kp[:, :-2], kp[:, 1:-1], kp[:, 2:]], axis=2)
    vband = jnp.concatenate([vp[:, :-2], vp[:, 1:-1], vp[:, 2:]], axis=2)
    blk = jnp.arange(nb)[:, None] * BLOCK
    qpos = blk + jnp.arange(BLOCK)[None, :]
    kpos = blk - BLOCK + jnp.arange(3 * BLOCK)[None, :]
    valid = ((jnp.abs(qpos[:, :, None] - kpos[:, None, :]) <= WINDOW)
             & (kpos[:, None, :] >= 0) & (kpos[:, None, :] < S))
    s_band = jnp.einsum('bnqhgd,bnkhd->bnhgqk', qb, kband, preferred_element_type=jnp.float32) * SWA_SCALE
    s_band = jnp.where(valid[None, :, None, None], s_band, NEG_INF)
    s_ctx = jnp.einsum('bnqhgd,bkhd->bnhgqk', qb, k_ctx, preferred_element_type=jnp.float32) * SWA_SCALE
    s_sink = jnp.broadcast_to(
        sink.astype(jnp.float32).reshape(SWA_KV_HEADS, SWA_GROUP)[None, None, :, :, None, None],
        s_band.shape[:-1] + (1,))
    p = jax.nn.softmax(jnp.concatenate([s_band, s_ctx, s_sink], axis=-1), axis=-1)
    p_band = p[..., :3 * BLOCK].astype(v.dtype)
    p_ctx = p[..., 3 * BLOCK:3 * BLOCK + n_ctx].astype(v.dtype)
    out = (jnp.einsum('bnhgqk,bnkhd->bnqhgd', p_band, vband)
           + jnp.einsum('bnhgqk,bkhd->bnqhgd', p_ctx, v_ctx))
    return out.reshape(B, S, SWA_HEADS * SWA_HD)


def swa_context(q, k, v, sink):
    B, n = q.shape[0], q.shape[1]
    qg = q.reshape(B, n, SWA_KV_HEADS, SWA_GROUP, SWA_HD)
    s = jnp.einsum('bqhgd,bkhd->bhgqk', qg, k, preferred_element_type=jnp.float32) * SWA_SCALE
    s_sink = jnp.broadcast_to(
        sink.astype(jnp.float32).reshape(SWA_KV_HEADS, SWA_GROUP)[None, :, :, None, None],
        s.shape[:-1] + (1,))
    p = jax.nn.softmax(jnp.concatenate([s, s_sink], axis=-1), axis=-1)[..., :n].astype(v.dtype)
    return jnp.einsum('bhgqk,bkhd->bqhgd', p, v).reshape(B, n, SWA_HEADS * SWA_HD)


def squared_relu_mlp(h, w1, w2):
    a = jax.nn.relu(h @ w1)
    return (a * a) @ w2


def setup_inputs(seed: int = 0) -> dict:
    key = jax.random.key(seed)
    ks = jax.random.split(key, 24)
    f32 = jnp.float32

    def w(k, shape, fan_in):
        return jax.random.normal(k, shape, f32) * (fan_in ** -0.5)

    def gain(k, shape):
        return 1.0 + 0.1 * jax.random.normal(k, shape, f32)

    L = DEPTH
    return {
        "x": jax.random.normal(ks[0], (BATCH, SEQ, D_MODEL), f32),
        "c": jax.random.normal(ks[1], (BATCH, D_MODEL), f32),
        "ctx": jax.random.normal(ks[2], (BATCH, CTX_LEN, D_MODEL), f32),
        "c_ctx": jax.random.normal(ks[3], (D_MODEL,), f32),
        "w_mod": w(ks[4], (L, D_MODEL, 6 * D_MODEL), D_MODEL),
        "b_mod": 0.01 * jax.random.normal(ks[5], (L, 6 * D_MODEL), f32),
        "g_attn": gain(ks[6], (L, D_MODEL)),
        "w_in": w(ks[7], (L, D_MODEL, IN_W), D_MODEL),
        "g_q_a": gain(ks[8], (L, Q_LORA)),
        "w_uq": w(ks[9], (L, Q_LORA, MLA_HEADS * MLA_QK), Q_LORA),
        "g_kv_a": gain(ks[10], (L, KV_LORA)),
        "w_ukv": w(ks[11], (L, KV_LORA, MLA_HEADS * (MLA_NOPE + MLA_V)), KV_LORA),
        "g_mla_q": gain(ks[12], (L, MLA_QK)),
        "g_mla_k": gain(ks[13], (L, MLA_QK)),
        "g_swa_q": gain(ks[14], (L, SWA_HD)),
        "g_swa_k": gain(ks[15], (L, SWA_HD)),
        "swa_sink": 0.5 * jax.random.normal(ks[16], (L, SWA_HEADS), f32),
        "w_out": w(ks[17], (L, MIX_W, D_MODEL), MIX_W),
        "g_mlp": gain(ks[18], (L, D_MODEL)),
        "w_mlp1": w(ks[19], (L, D_MODEL, D_FF), D_MODEL),
        "w_mlp2": w(ks[20], (L, D_FF, D_MODEL), D_FF),
    }


def reference(x, c, ctx, c_ctx, w_mod, b_mod, g_attn, w_in, g_q_a, w_uq, g_kv_a, w_ukv,
              g_mla_q, g_mla_k, g_swa_q, g_swa_k, swa_sink, w_out, g_mlp, w_mlp1, w_mlp2):
    rows = x.shape[1] // GRID_W
    rope = (axial_rope(rows, MLA_ROPE), axial_rope(rows, SWA_HD))

    for i in range(DEPTH):
        sh1, sc1, g1, sh2, sc2, g2 = [m[:, None, :] for m in modulation(c, w_mod[i], b_mod[i])]
        csh1, csc1, cg1, csh2, csc2, cg2 = modulation(c_ctx, w_mod[i], b_mod[i])

        h_lat = modulate(x, g_attn[i], sh1, sc1)
        h_ctx = modulate(ctx, g_attn[i], csh1, csc1)
        proj_args = (w_in[i], g_q_a[i], w_uq[i], g_kv_a[i], w_ukv[i],
                     g_mla_q[i], g_mla_k[i], g_swa_q[i], g_swa_k[i])
        qm_l, km_l, vm_l, qs_l, ks_l, vs_l = project(h_lat, *proj_args, rope)
        qm_c, km_c, vm_c, qs_c, ks_c, vs_c = project(h_ctx, *proj_args, None)

        mix_lat = jnp.concatenate([
            mla_latent(qm_l, km_l, vm_l, km_c, vm_c),
            swa_latent(qs_l, ks_l, vs_l, ks_c, vs_c, swa_sink[i]),
        ], axis=-1)
        x = x + g1 * (mix_lat @ w_out[i])
        x = x + g2 * squared_relu_mlp(modulate(x, g_mlp[i], sh2, sc2), w_mlp1[i], w_mlp2[i])

        if i + 1 < DEPTH:
            mix_ctx = jnp.concatenate([
                mla_context(qm_c, km_c, vm_c),
                swa_context(qs_c, ks_c, vs_c, swa_sink[i]),
            ], axis=-1)
            ctx = ctx + cg1 * (mix_ctx @ w_out[i])
            ctx = ctx + cg2 * squared_relu_mlp(modulate(ctx, g_mlp[i], csh2, csc2), w_mlp1[i], w_mlp2[i])

    return x
```

```python
import functools
import math

import jax
import jax.numpy as jnp
from jax import lax
from jax.experimental import pallas as pl
from jax.experimental.pallas import tpu as pltpu

D_MODEL = 1024
GRID_W = 64
MLA_HEADS = 8
MLA_NOPE = 64
MLA_ROPE = 32
MLA_QK = MLA_NOPE + MLA_ROPE
MLA_V = 64
Q_LORA = 256
KV_LORA = 128
SWA_HEADS = 8
SWA_KV_HEADS = 2
SWA_GROUP = SWA_HEADS // SWA_KV_HEADS
SWA_HD = 64
WINDOW = 128
D_FF = 4 * D_MODEL
ROPE_THETA = 10000.0
EPS = 1e-6
NEG_INF = -1e30
LOG2E = 1.4426950408889634
MLA_QSCALE = LOG2E / math.sqrt(MLA_QK)
SWA_QSCALE = LOG2E / math.sqrt(SWA_HD)

LANE = 128
TOK_TILE = 256
MLP_TILE = 512
FF_CHUNK = 1024
SWA_WIN = 512
VMEM_LIMIT = 56 * 1024 * 1024

F32 = jnp.float32
BF16 = jnp.bfloat16
NT_DIMS = (((1,), (1,)), ((), ()))


def _dot(a, b):
    return jnp.dot(a, b, preferred_element_type=F32)


def _dot_nt(a, b):
    return lax.dot_general(a, b, NT_DIMS, preferred_element_type=F32)


def _mod_kernel(cond_ref, w_ref, b_ref, o_ref):
    cnd = cond_ref[...]
    act = cnd * jax.nn.sigmoid(cnd)
    o_ref[...] = _dot(act.astype(BF16), w_ref[...].astype(BF16)) + b_ref[...]


def _modulation(cond, w_mod, b_mod):
    n = w_mod.shape[1]
    tn = 1024
    return pl.pallas_call(
        _mod_kernel,
        grid=(n // tn,),
        in_specs=[
            pl.BlockSpec((8, D_MODEL), lambda i: (0, 0)),
            pl.BlockSpec((D_MODEL, tn), lambda i: (0, i)),
            pl.BlockSpec((1, tn), lambda i: (0, i)),
        ],
        out_specs=pl.BlockSpec((8, tn), lambda i: (0, i)),
        out_shape=jax.ShapeDtypeStruct((8, n), F32),
        name="modulation",
    )(cond, w_mod, b_mod)


def _rms_rows(x, n):
    return x * lax.rsqrt(jnp.sum(x * x, axis=-1, keepdims=True) * (1.0 / n) + EPS)


def _rms_cols(x, n):
    return x * lax.rsqrt(jnp.sum(x * x, axis=0, keepdims=True) * (1.0 / n) + EPS)


def _proj_kernel(*refs, latent):
    if latent:
        (x_ref, mod_ref, gattn_ref, wtok_ref, wvs_ref, gkva_ref, wuk_ref, wuv_ref, gmk_ref, gsk_ref,
         wqs_ref, gqa_ref, wuq_ref, gmq_ref, gsq_ref,
         mcos_ref, msin_ref, scos_ref, ssin_ref, kc_ref, ks1_ref, ks2_ref, sc_ref, ss1_ref, ss2_ref,
         km_ref, vmt_ref, ks_ref, vst_ref, qmt_ref, qst_ref) = refs
    else:
        (x_ref, mod_ref, gattn_ref, wtok_ref, wvs_ref, gkva_ref, wuk_ref, wuv_ref, gmk_ref, gsk_ref,
         km_ref, vmt_ref, ks_ref, vst_ref) = refs

    x = x_ref[0]
    y = _rms_rows(x, D_MODEL) * gattn_ref[...]
    h = (y * (1.0 + mod_ref[0, 1:2, :]) + mod_ref[0, 0:1, :]).astype(BF16)

    zt = _dot(h, wtok_ref[...])
    cq = zt[:, 0:256]
    ckv = zt[:, 256:384]
    krp = zt[:, 384:512]
    ksw = zt[:, 512:640]

    ckvn = (_rms_rows(ckv, KV_LORA) * gkva_ref[...]).astype(BF16)
    kpre = _dot(ckvn, wuk_ref[...])
    vt = _dot_nt(wuv_ref[...], ckvn)
    gmk = gmk_ref[...]
    for hd in range(MLA_HEADS):
        kh = kpre[:, hd * LANE:(hd + 1) * LANE] + krp
        kn = _rms_rows(kh, MLA_QK) * gmk
        if latent:
            kn = (kn * kc_ref[...] + pltpu.roll(kn, 112, 1) * ks1_ref[...]
                  + pltpu.roll(kn, 16, 1) * ks2_ref[...])
        km_ref[0, hd] = kn.astype(BF16)
        vmt_ref[0, hd, 0] = vt[hd * MLA_V:(hd + 1) * MLA_V].astype(BF16)

    lane = lax.broadcasted_iota(jnp.int32, ksw.shape, 1)
    first = lane < SWA_HD
    sq = ksw * ksw
    ss0 = jnp.sum(jnp.where(first, sq, 0.0), axis=-1, keepdims=True)
    ss1 = jnp.sum(jnp.where(first, 0.0, sq), axis=-1, keepdims=True)
    rr = jnp.where(first, lax.rsqrt(ss0 * (1.0 / SWA_HD) + EPS), lax.rsqrt(ss1 * (1.0 / SWA_HD) + EPS))
    ksn = ksw * rr * gsk_ref[...]
    if latent:
        ksn = (ksn * sc_ref[...] + pltpu.roll(ksn, 96, 1) * ss1_ref[...]
               + pltpu.roll(ksn, 32, 1) * ss2_ref[...])
    ks_ref[0] = ksn.astype(BF16)
    vst = _dot_nt(wvs_ref[...], h)
    for g in range(SWA_KV_HEADS):
        for t in range(TOK_TILE // LANE):
            vst_ref[0, g, t] = vst[g * SWA_HD:(g + 1) * SWA_HD, t * LANE:(t + 1) * LANE].astype(BF16)

    if not latent:
        return

    cqn = (_rms_rows(cq, Q_LORA) * gqa_ref[...]).astype(BF16)
    qt = _dot_nt(wuq_ref[...], cqn)
    gmq = gmq_ref[...]
    mcos = mcos_ref[...]
    msin = msin_ref[...]
    for hd in range(MLA_HEADS):
        qn = _rms_cols(qt[hd * LANE:(hd + 1) * LANE], MLA_QK) * (gmq * MLA_QSCALE)
        x1 = qn[64:80]
        x2 = qn[80:96]
        qmt_ref[0, hd, 0:64, :] = qn[0:64].astype(BF16)
        qmt_ref[0, hd, 64:80, :] = (x1 * mcos - x2 * msin).astype(BF16)
        qmt_ref[0, hd, 80:96, :] = (x2 * mcos + x1 * msin).astype(BF16)
        qmt_ref[0, hd, 96:128, :] = jnp.zeros((32, TOK_TILE), BF16)

    qst = _dot_nt(wqs_ref[...], h)
    gsq = gsq_ref[...]
    scos = scos_ref[...]
    ssin = ssin_ref[...]
    for hd in range(SWA_HEADS):
        qn = _rms_cols(qst[hd * SWA_HD:(hd + 1) * SWA_HD], SWA_HD) * (gsq * SWA_QSCALE)
        x1 = qn[0:32]
        x2 = qn[32:64]
        g = hd // SWA_GROUP
        base = g * SWA_HD
        other = (1 - g) * SWA_HD
        qst_ref[0, hd, base:base + 32, :] = (x1 * scos - x2 * ssin).astype(BF16)
        qst_ref[0, hd, base + 32:base + 64, :] = (x2 * scos + x1 * ssin).astype(BF16)
        qst_ref[0, hd, other:other + 64, :] = jnp.zeros((64, TOK_TILE), BF16)


def _full(shape):
    nd = len(shape)
    return pl.BlockSpec(shape, lambda b, j: (0,) * nd)


def _project(x, mod3, mod_row, weights, tables, latent):
    bsz, n, _ = x.shape
    nt = n // TOK_TILE
    if mod_row is None:
        mod_map = lambda b, j: (b, 0, 0)
    else:
        mod_map = lambda b, j: (mod_row, 0, 0)
    common = [weights[k] for k in ("gattn", "wtok", "wvs", "gkva", "wuk", "wuv", "gmk", "gsk")]
    ins = [x, mod3] + common
    in_specs = [
        pl.BlockSpec((1, TOK_TILE, D_MODEL), lambda b, j: (b, j, 0)),
        pl.BlockSpec((1, 6, D_MODEL), mod_map),
    ] + [_full(w.shape) for w in common]
    out_shape = [
        jax.ShapeDtypeStruct((bsz, MLA_HEADS, n, LANE), BF16),
        jax.ShapeDtypeStruct((bsz, MLA_HEADS, nt, MLA_V, TOK_TILE), BF16),
        jax.ShapeDtypeStruct((bsz, n, LANE), BF16),
        jax.ShapeDtypeStruct((bsz, SWA_KV_HEADS, n // LANE, SWA_HD, LANE), BF16),
    ]
    out_specs = [
        pl.BlockSpec((1, MLA_HEADS, TOK_TILE, LANE), lambda b, j: (b, 0, j, 0)),
        pl.BlockSpec((1, MLA_HEADS, 1, MLA_V, TOK_TILE), lambda b, j: (b, 0, j, 0, 0)),
        pl.BlockSpec((1, TOK_TILE, LANE), lambda b, j: (b, j, 0)),
        pl.BlockSpec((1, SWA_KV_HEADS, TOK_TILE // LANE, SWA_HD, LANE), lambda b, j: (b, 0, j, 0, 0)),
    ]
    if latent:
        extra = [weights[k] for k in ("wqs", "gqa", "wuq", "gmq", "gsq")]
        ins += extra
        in_specs += [_full(w.shape) for w in extra]
        feat = [tables[k] for k in ("mcos", "msin", "scos", "ssin")]
        ins += feat
        in_specs += [pl.BlockSpec((t.shape[0], TOK_TILE), lambda b, j: (0, j)) for t in feat]
        tok = [tables[k] for k in ("kc", "ks1", "ks2", "sc", "ss1", "ss2")]
        ins += tok
        in_specs += [pl.BlockSpec((TOK_TILE, LANE), lambda b, j: (j, 0)) for _ in tok]
        out_shape += [
            jax.ShapeDtypeStruct((bsz, MLA_HEADS, LANE, n), BF16),
            jax.ShapeDtypeStruct((bsz, SWA_HEADS, LANE, n), BF16),
        ]
        out_specs += [
            pl.BlockSpec((1, MLA_HEADS, LANE, TOK_TILE), lambda b, j: (b, 0, 0, j)),
            pl.BlockSpec((1, SWA_HEADS, LANE, TOK_TILE), lambda b, j: (b, 0, 0, j)),
        ]
    return pl.pallas_call(
        functools.partial(_proj_kernel, latent=latent),
        grid=(bsz, nt),
        in_specs=in_specs,
        out_specs=out_specs,
        out_shape=out_shape,
        compiler_params=pltpu.CompilerParams(vmem_limit_bytes=VMEM_LIMIT),
        name="project_latent" if latent else "project_context",
    )(*ins)


def _mla_kernel(qt_ref, kc_ref, kl_ref, vct_ref, vlt_ref, o_ref, ot_scr, *, n_lat_tiles):
    def head_body(hd, carry):
        qt = qt_ref[0, hd]

        s = _dot(kc_ref[0, hd], qt)
        m = jnp.max(s, axis=0, keepdims=True)
        p = jnp.exp2(s - m)
        l = jnp.sum(p, axis=0, keepdims=True)
        acc = _dot(vct_ref[0, hd, 0], p.astype(BF16))

        def kt_body(kt, st):
            m, l, acc = st
            start = pl.multiple_of(kt * TOK_TILE, TOK_TILE)
            s = _dot(kl_ref[0, hd, pl.ds(start, TOK_TILE), :], qt)
            m_new = jnp.maximum(m, jnp.max(s, axis=0, keepdims=True))
            alpha = jnp.exp2(m - m_new)
            p = jnp.exp2(s - m_new)
            l = alpha * l + jnp.sum(p, axis=0, keepdims=True)
            acc = alpha * acc + _dot(vlt_ref[0, hd, kt], p.astype(BF16))
            return m_new, l, acc

        m, l, acc = lax.fori_loop(0, n_lat_tiles, kt_body, (m, l, acc))
        row = pl.multiple_of(hd * MLA_V, MLA_V)
        ot_scr[pl.ds(row, MLA_V), :] = acc / l
        return carry

    lax.fori_loop(0, MLA_HEADS, head_body, 0)
    o_ref[0] = ot_scr[...].T.astype(BF16)


def _mla_attention(qmt, km_c, km_l, vmt_c, vmt_l):
    bsz, _, _, s = qmt.shape
    n_ctx = km_c.shape[2]
    nt = s // TOK_TILE
    return pl.pallas_call(
        functools.partial(_mla_kernel, n_lat_tiles=nt),
        grid=(bsz, nt),
        in_specs=[
            pl.BlockSpec((1, MLA_HEADS, LANE, TOK_TILE), lambda b, j: (b, 0, 0, j)),
            pl.BlockSpec((1, MLA_HEADS, n_ctx, LANE), lambda b, j: (b, 0, 0, 0)),
            pl.BlockSpec((1, MLA_HEADS, s, LANE), lambda b, j: (b, 0, 0, 0)),
            pl.BlockSpec((1, MLA_HEADS, 1, MLA_V, n_ctx), lambda b, j: (b, 0, 0, 0, 0)),
            pl.BlockSpec((1, MLA_HEADS, nt, MLA_V, TOK_TILE), lambda b, j: (b, 0, 0, 0, 0)),
        ],
        out_specs=pl.BlockSpec((1, TOK_TILE, MLA_HEADS * MLA_V), lambda b, j: (b, j, 0)),
        out_shape=jax.ShapeDtypeStruct((bsz, s, MLA_HEADS * MLA_V), BF16),
        scratch_shapes=[pltpu.VMEM((MLA_HEADS * MLA_V, TOK_TILE), F32)],
        compiler_params=pltpu.CompilerParams(vmem_limit_bytes=VMEM_LIMIT),
        name="mla_attention",
    )(qmt, km_c, km_l, vmt_c, vmt_l)


def _swa_kernel(sink_ref, qt_ref, kc_ref, kl_ref, vct_ref, vlt_ref, o_ref, ot_scr, *, seq):
    g = pl.program_id(1)
    j = pl.program_id(2)
    start = jnp.clip(j * TOK_TILE - WINDOW, 0, seq - SWA_WIN)
    start = pl.multiple_of(start, LANE)
    t0 = start // LANE
    kwin = kl_ref[0, pl.ds(start, SWA_WIN), :]
    kctx = kc_ref[0]
    vwin = jnp.concatenate([vlt_ref[0, 0, t0 + i] for i in range(SWA_WIN // LANE)], axis=1)
    vctx = jnp.concatenate([vct_ref[0, 0, i] for i in range(vct_ref.shape[2])], axis=1)
    qpos = j * TOK_TILE + lax.broadcasted_iota(jnp.int32, (SWA_WIN, TOK_TILE), 1)
    kpos = start + lax.broadcasted_iota(jnp.int32, (SWA_WIN, TOK_TILE), 0)
    valid = jnp.abs(qpos - kpos) <= WINDOW
    for hh in range(SWA_GROUP):
        qt = qt_ref[0, hh]
        sb = jnp.where(valid, _dot(kwin, qt), NEG_INF)
        sc = _dot(kctx, qt)
        sk = sink_ref[g * SWA_GROUP + hh] * LOG2E
        m = jnp.maximum(jnp.maximum(jnp.max(sb, axis=0, keepdims=True),
                                    jnp.max(sc, axis=0, keepdims=True)), sk)
        pb = jnp.exp2(sb - m)
        pc = jnp.exp2(sc - m)
        l = (jnp.sum(pb, axis=0, keepdims=True) + jnp.sum(pc, axis=0, keepdims=True)
             + jnp.exp2(sk - m))
        ot = _dot(vwin, pb.astype(BF16)) + _dot(vctx, pc.astype(BF16))
        ot_scr[hh * SWA_HD:(hh + 1) * SWA_HD, :] = ot / l
    o_ref[0] = ot_scr[...].T.astype(BF16)


def _swa_attention(sink, qst, ks_c, ks_l, vst_c, vst_l):
    bsz, _, _, s = qst.shape
    n_ctx = ks_c.shape[1]
    nt = s // TOK_TILE
    return pl.pallas_call(
        functools.partial(_swa_kernel, seq=s),
        grid=(bsz, SWA_KV_HEADS, nt),
        in_specs=[
            pl.BlockSpec(memory_space=pltpu.SMEM),
            pl.BlockSpec((1, SWA_GROUP, LANE, TOK_TILE), lambda b, g, j: (b, g, 0, j)),
            pl.BlockSpec((1, n_ctx, LANE), lambda b, g, j: (b, 0, 0)),
            pl.BlockSpec((1, s, LANE), lambda b, g, j: (b, 0, 0)),
            pl.BlockSpec((1, 1, n_ctx // LANE, SWA_HD, LANE), lambda b, g, j: (b, g, 0, 0, 0)),
            pl.BlockSpec((1, 1, s // LANE, SWA_HD, LANE), lambda b, g, j: (b, g, 0, 0, 0)),
        ],
        out_specs=pl.BlockSpec((1, TOK_TILE, SWA_GROUP * SWA_HD), lambda b, g, j: (b, j, g)),
        out_shape=jax.ShapeDtypeStruct((bsz, s, SWA_HEADS * SWA_HD), BF16),
        scratch_shapes=[pltpu.VMEM((SWA_GROUP * SWA_HD, TOK_TILE), F32)],
        compiler_params=pltpu.CompilerParams(vmem_limit_bytes=VMEM_LIMIT),
        name="swa_attention",
    )(sink, qst, ks_c, ks_l, vst_c, vst_l)


def _mlp_kernel(x_ref, mm_ref, ms_ref, mod_ref, wo_ref, gmlp_ref, w1_ref, w2_ref, o_ref, acc_ref):
    half = MLA_HEADS * MLA_V
    y1 = _dot(mm_ref[0], wo_ref[0:half, :]) + _dot(ms_ref[0], wo_ref[half:, :])
    x1 = x_ref[0] + mod_ref[0, 2:3, :] * y1
    h2 = (_rms_rows(x1, D_MODEL) * gmlp_ref[...] * (1.0 + mod_ref[0, 4:5, :])
          + mod_ref[0, 3:4, :]).astype(BF16)
    for c in range(D_FF // FF_CHUNK):
        a = jnp.maximum(_dot(h2, w1_ref[:, c * FF_CHUNK:(c + 1) * FF_CHUNK]), 0.0)
        part = _dot((a * a).astype(BF16), w2_ref[c * FF_CHUNK:(c + 1) * FF_CHUNK, :])
        if c == 0:
            acc_ref[...] = part
        else:
            acc_ref[...] += part
    o_ref[0] = x1 + mod_ref[0, 5:6, :] * acc_ref[...]


def _outproj_mlp(x, mix_m, mix_s, mod3, w_out, g_mlp, w1, w2):
    bsz, s, _ = x.shape
    half = MLA_HEADS * MLA_V
    const = lambda b, j: (0, 0)
    single = pl.Buffered(1)
    return pl.pallas_call(
        _mlp_kernel,
        grid=(bsz, s // MLP_TILE),
        in_specs=[
            pl.BlockSpec((1, MLP_TILE, D_MODEL), lambda b, j: (b, j, 0)),
            pl.BlockSpec((1, MLP_TILE, half), lambda b, j: (b, j, 0)),
            pl.BlockSpec((1, MLP_TILE, half), lambda b, j: (b, j, 0)),
            pl.BlockSpec((1, 6, D_MODEL), lambda b, j: (b, 0, 0)),
            pl.BlockSpec((D_MODEL, D_MODEL), const, pipeline_mode=single),
            pl.BlockSpec((1, D_MODEL), const),
            pl.BlockSpec((D_MODEL, D_FF), const, pipeline_mode=single),
            pl.BlockSpec((D_FF, D_MODEL), const, pipeline_mode=single),
        ],
        out_specs=pl.BlockSpec((1, MLP_TILE, D_MODEL), lambda b, j: (b, j, 0)),
        out_shape=jax.ShapeDtypeStruct((bsz, s, D_MODEL), F32),
        scratch_shapes=[pltpu.VMEM((MLP_TILE, D_MODEL), F32)],
        compiler_params=pltpu.CompilerParams(vmem_limit_bytes=VMEM_LIMIT),
        name="outproj_mlp",
    )(x, mix_m, mix_s, mod3, w_out, g_mlp, w1, w2)


def _rope_tables(seq):
    def cos_sin(rot_dim):
        n_freq = rot_dim // 4
        inv = ROPE_THETA ** (-jnp.arange(n_freq, dtype=F32) / n_freq)
        rows = seq // GRID_W
        row = jnp.repeat(jnp.arange(rows, dtype=F32), GRID_W)
        col = jnp.tile(jnp.arange(GRID_W, dtype=F32), rows)
        ang = jnp.concatenate([row[:, None] * inv, col[:, None] * inv], axis=-1)
        return jnp.cos(ang), jnp.sin(ang)

    mcos, msin = cos_sin(MLA_ROPE)
    scos, ssin = cos_sin(SWA_HD)
    z16 = jnp.zeros((seq, 16), F32)
    z32 = jnp.zeros((seq, 32), F32)
    one64 = jnp.ones((seq, 64), F32)
    kc = jnp.concatenate([one64, mcos, mcos, jnp.ones((seq, 32), F32)], axis=1)
    ks1 = jnp.concatenate([jnp.zeros((seq, 64), F32), -msin, z16, z32], axis=1)
    ks2 = jnp.concatenate([jnp.zeros((seq, 64), F32), z16, msin, z32], axis=1)
    sc = jnp.concatenate([scos, scos, scos, scos], axis=1)
    ss1 = jnp.concatenate([-ssin, z32, -ssin, z32], axis=1)
    ss2 = jnp.concatenate([z32, ssin, z32, ssin], axis=1)
    return dict(mcos=mcos.T, msin=msin.T, scos=scos.T, ssin=ssin.T,
                kc=kc, ks1=ks1, ks2=ks2, sc=sc, ss1=ss1, ss2=ss2)


def _prep_weights(g_attn, w_in, g_q_a, w_uq, g_kv_a, w_ukv, g_mla_q, g_mla_k, g_swa_q, g_swa_k):
    o_ckv = Q_LORA
    o_kr = o_ckv + KV_LORA
    o_qs = o_kr + MLA_ROPE
    o_ks = o_qs + SWA_HEADS * SWA_HD
    o_vs = o_ks + SWA_KV_HEADS * SWA_HD
    w_cq = w_in[:, :o_ckv]
    w_ckv = w_in[:, o_ckv:o_kr]
    w_kr = w_in[:, o_kr:o_qs]
    w_qs = w_in[:, o_qs:o_ks]
    w_ks = w_in[:, o_ks:o_vs]
    w_vs = w_in[:, o_vs:]
    w_kr_p = jnp.pad(w_kr, ((0, 0), (MLA_NOPE, LANE - MLA_QK)))
    wtok = jnp.concatenate([w_cq, w_ckv, w_kr_p, w_ks], axis=1).astype(BF16)
    w_ukv_h = w_ukv.reshape(KV_LORA, MLA_HEADS, MLA_NOPE + MLA_V)
    wuk = jnp.pad(w_ukv_h[:, :, :MLA_NOPE], ((0, 0), (0, 0), (0, LANE - MLA_NOPE)))
    wuk = wuk.reshape(KV_LORA, MLA_HEADS * LANE).astype(BF16)
    wuv = w_ukv_h[:, :, MLA_NOPE:].reshape(KV_LORA, MLA_HEADS * MLA_V).T.astype(BF16)
    w_uq_h = jnp.pad(w_uq.reshape(Q_LORA, MLA_HEADS, MLA_QK), ((0, 0), (0, 0), (0, LANE - MLA_QK)))
    wuq = w_uq_h.reshape(Q_LORA, MLA_HEADS * LANE).T.astype(BF16)
    return dict(
        gattn=g_attn[None, :], wtok=wtok, wvs=w_vs.T.astype(BF16), gkva=g_kv_a[None, :],
        wuk=wuk, wuv=wuv, gmk=jnp.pad(g_mla_k, (0, LANE - MLA_QK))[None, :],
        gsk=jnp.tile(g_swa_k, SWA_KV_HEADS)[None, :],
        wqs=w_qs.T.astype(BF16), gqa=g_q_a[None, :], wuq=wuq,
        gmq=jnp.pad(g_mla_q, (0, LANE - MLA_QK))[:, None], gsq=g_swa_q[:, None],
    )


def kernel(x, c, ctx, c_ctx, w_mod, b_mod, g_attn, w_in, g_q_a, w_uq, g_kv_a, w_ukv, g_mla_q, g_mla_k,
           g_swa_q, g_swa_k, swa_sink, w_out, g_mlp, w_mlp1, w_mlp2):
    bsz, seq, _ = x.shape
    assert w_mod.shape[0] == 1, "single-layer block"
    assert bsz < 8 and seq % MLP_TILE == 0 and ctx.shape[1] % TOK_TILE == 0

    cond = jnp.zeros((8, D_MODEL), F32).at[:bsz].set(c).at[bsz].set(c_ctx)
    mod = _modulation(cond, w_mod[0], b_mod[0][None, :])
    mod3 = mod.reshape(8, 6, D_MODEL)

    weights = _prep_weights(g_attn[0], w_in[0], g_q_a[0], w_uq[0], g_kv_a[0], w_ukv[0],
                            g_mla_q[0], g_mla_k[0], g_swa_q[0], g_swa_k[0])
    tables = _rope_tables(seq)

    km_l, vmt_l, ks_l, vst_l, qmt, qst = _project(x, mod3, None, weights, tables, latent=True)
    km_c, vmt_c, ks_c, vst_c = _project(ctx, mod3, bsz, weights, None, latent=False)

    mix_m = _mla_attention(qmt, km_c, km_l, vmt_c, vmt_l)
    mix_s = _swa_attention(swa_sink[0], qst, ks_c, ks_l, vst_c, vst_l)

    return _outproj_mlp(x, mix_m, mix_s, mod3, w_out[0].astype(BF16), g_mlp[0][None, :],
                        w_mlp1[0].astype(BF16), w_mlp2[0].astype(BF16))
```

```python
import functools
import math

import jax
import jax.numpy as jnp
from jax import lax
from jax.experimental import pallas as pl
from jax.experimental.pallas import tpu as pltpu

D_MODEL = 1024
GRID_W = 64
MLA_HEADS = 8
MLA_NOPE = 64
MLA_ROPE = 32
MLA_QK = MLA_NOPE + MLA_ROPE
MLA_V = 64
MLA_V_AUG = MLA_V + 16
Q_LORA = 256
KV_LORA = 128
SWA_HEADS = 8
SWA_KV_HEADS = 2
SWA_GROUP = SWA_HEADS // SWA_KV_HEADS
SWA_HD = 64
WINDOW = 128
D_FF = 4 * D_MODEL
ROPE_THETA = 10000.0
EPS = 1e-6
NEG_INF = -1e30
LOG2E = 1.4426950408889634
MLA_QSCALE = LOG2E / math.sqrt(MLA_QK)
SWA_QSCALE = LOG2E / math.sqrt(SWA_HD)

LANE = 128
TOK_TILE = 256
MLP_TILE = 512
FF_CHUNK = 1024
SWA_WIN = 512
MLA_SUB = 256
MLA_STAGE_SUBS = 8
MLA_CHUNK = 32
MLA_QK_LEAD = 3
MLA_HEAD_UNROLL = 2
VMEM_LIMIT = 56 * 1024 * 1024

F32 = jnp.float32
BF16 = jnp.bfloat16
NT_DIMS = (((1,), (1,)), ((), ()))


def _dot(a, b):
    return jnp.dot(a, b, preferred_element_type=F32)


def _dot_nt(a, b):
    return lax.dot_general(a, b, NT_DIMS, preferred_element_type=F32)


def _mod_kernel(cond_ref, w_ref, b_ref, o_ref):
    cnd = cond_ref[...]
    act = cnd * jax.nn.sigmoid(cnd)
    o_ref[...] = _dot(act.astype(BF16), w_ref[...].astype(BF16)) + b_ref[...]


def _modulation(cond, w_mod, b_mod):
    n = w_mod.shape[1]
    tn = 1024
    return pl.pallas_call(
        _mod_kernel,
        grid=(n // tn,),
        in_specs=[
            pl.BlockSpec((8, D_MODEL), lambda i: (0, 0)),
            pl.BlockSpec((D_MODEL, tn), lambda i: (0, i)),
            pl.BlockSpec((1, tn), lambda i: (0, i)),
        ],
        out_specs=pl.BlockSpec((8, tn), lambda i: (0, i)),
        out_shape=jax.ShapeDtypeStruct((8, n), F32),
        name="modulation",
    )(cond, w_mod, b_mod)


def _rms_rows(x, n):
    return x * lax.rsqrt(jnp.sum(x * x, axis=-1, keepdims=True) * (1.0 / n) + EPS)


def _rms_cols(x, n):
    return x * lax.rsqrt(jnp.sum(x * x, axis=0, keepdims=True) * (1.0 / n) + EPS)


def _proj_kernel(*refs, latent):
    if latent:
        (x_ref, mod_ref, gattn_ref, wtok_ref, wvs_ref, gkva_ref, wuk_ref, wuv_ref, gmk_ref, gsk_ref,
         wqs_ref, gqa_ref, wuq_ref, gmq_ref, gsq_ref,
         mcos_ref, msin_ref, scos_ref, ssin_ref, kc_ref, ks1_ref, ks2_ref, sc_ref, ss1_ref, ss2_ref,
         km_ref, vmt_ref, ks_ref, vst_ref, qmt_ref, qst_ref) = refs
    else:
        (x_ref, mod_ref, gattn_ref, wtok_ref, wvs_ref, gkva_ref, wuk_ref, wuv_ref, gmk_ref, gsk_ref,
         km_ref, vmt_ref, ks_ref, vst_ref) = refs

    x = x_ref[0]
    y = _rms_rows(x, D_MODEL) * gattn_ref[...]
    h = (y * (1.0 + mod_ref[0, 1:2, :]) + mod_ref[0, 0:1, :]).astype(BF16)

    zt = _dot(h, wtok_ref[...])
    cq = zt[:, 0:256]
    ckv = zt[:, 256:384]
    krp = zt[:, 384:512]
    ksw = zt[:, 512:640]

    ckvn = (_rms_rows(ckv, KV_LORA) * gkva_ref[...]).astype(BF16)
    kpre = _dot(ckvn, wuk_ref[...])
    vt = _dot_nt(wuv_ref[...], ckvn)
    gmk = gmk_ref[...]
    pad_rows = MLA_V_AUG - MLA_V
    ones_row = (lax.broadcasted_iota(jnp.int32, (pad_rows, TOK_TILE), 0) == 0).astype(BF16)
    for hd in range(MLA_HEADS):
        kh = kpre[:, hd * LANE:(hd + 1) * LANE] + krp
        kn = _rms_rows(kh, MLA_QK) * gmk
        if latent:
            kn = (kn * kc_ref[...] + pltpu.roll(kn, 112, 1) * ks1_ref[...]
                  + pltpu.roll(kn, 16, 1) * ks2_ref[...])
        km_ref[0, hd] = kn.astype(BF16)
        vmt_ref[0, hd, 0:MLA_V, :] = vt[hd * MLA_V:(hd + 1) * MLA_V].astype(BF16)
        vmt_ref[0, hd, MLA_V:MLA_V_AUG, :] = ones_row

    lane = lax.broadcasted_iota(jnp.int32, ksw.shape, 1)
    first = lane < SWA_HD
    sq = ksw * ksw
    ss0 = jnp.sum(jnp.where(first, sq, 0.0), axis=-1, keepdims=True)
    ss1 = jnp.sum(jnp.where(first, 0.0, sq), axis=-1, keepdims=True)
    rr = jnp.where(first, lax.rsqrt(ss0 * (1.0 / SWA_HD) + EPS), lax.rsqrt(ss1 * (1.0 / SWA_HD) + EPS))
    ksn = ksw * rr * gsk_ref[...]
    if latent:
        ksn = (ksn * sc_ref[...] + pltpu.roll(ksn, 96, 1) * ss1_ref[...]
               + pltpu.roll(ksn, 32, 1) * ss2_ref[...])
    ks_ref[0] = ksn.astype(BF16)
    vst = _dot_nt(wvs_ref[...], h)
    for g in range(SWA_KV_HEADS):
        for t in range(TOK_TILE // LANE):
            vst_ref[0, g, t] = vst[g * SWA_HD:(g + 1) * SWA_HD, t * LANE:(t + 1) * LANE].astype(BF16)

    if not latent:
        return

    cqn = (_rms_rows(cq, Q_LORA) * gqa_ref[...]).astype(BF16)
    qt = _dot_nt(wuq_ref[...], cqn)
    gmq = gmq_ref[...]
    mcos = mcos_ref[...]
    msin = msin_ref[...]
    for hd in range(MLA_HEADS):
        qn = _rms_cols(qt[hd * LANE:(hd + 1) * LANE], MLA_QK) * (gmq * MLA_QSCALE)
        x1 = qn[64:80]
        x2 = qn[80:96]
        qmt_ref[0, hd, 0:64, :] = qn[0:64].astype(BF16)
        qmt_ref[0, hd, 64:80, :] = (x1 * mcos - x2 * msin).astype(BF16)
        qmt_ref[0, hd, 80:96, :] = (x2 * mcos + x1 * msin).astype(BF16)
        qmt_ref[0, hd, 96:128, :] = jnp.zeros((32, TOK_TILE), BF16)

    qst = _dot_nt(wqs_ref[...], h)
    gsq = gsq_ref[...]
    scos = scos_ref[...]
    ssin = ssin_ref[...]
    for hd in range(SWA_HEADS):
        qn = _rms_cols(qst[hd * SWA_HD:(hd + 1) * SWA_HD], SWA_HD) * (gsq * SWA_QSCALE)
        x1 = qn[0:32]
        x2 = qn[32:64]
        g = hd // SWA_GROUP
        base = g * SWA_HD
        other = (1 - g) * SWA_HD
        qst_ref[0, hd, base:base + 32, :] = (x1 * scos - x2 * ssin).astype(BF16)
        qst_ref[0, hd, base + 32:base + 64, :] = (x2 * scos + x1 * ssin).astype(BF16)
        qst_ref[0, hd, other:other + 64, :] = jnp.zeros((64, TOK_TILE), BF16)


def _full(shape):
    nd = len(shape)
    return pl.BlockSpec(shape, lambda b, j: (0,) * nd)


def _project(x, mod3, mod_row, weights, tables, latent):
    bsz, n, _ = x.shape
    nt = n // TOK_TILE
    if mod_row is None:
        mod_map = lambda b, j: (b, 0, 0)
    else:
        mod_map = lambda b, j: (mod_row, 0, 0)
    common = [weights[k] for k in ("gattn", "wtok", "wvs", "gkva", "wuk", "wuv", "gmk", "gsk")]
    ins = [x, mod3] + common
    in_specs = [
        pl.BlockSpec((1, TOK_TILE, D_MODEL), lambda b, j: (b, j, 0)),
        pl.BlockSpec((1, 6, D_MODEL), mod_map),
    ] + [_full(w.shape) for w in common]
    out_shape = [
        jax.ShapeDtypeStruct((bsz, MLA_HEADS, n, LANE), BF16),
        jax.ShapeDtypeStruct((bsz, MLA_HEADS, MLA_V_AUG, n), BF16),
        jax.ShapeDtypeStruct((bsz, n, LANE), BF16),
        jax.ShapeDtypeStruct((bsz, SWA_KV_HEADS, n // LANE, SWA_HD, LANE), BF16),
    ]
    out_specs = [
        pl.BlockSpec((1, MLA_HEADS, TOK_TILE, LANE), lambda b, j: (b, 0, j, 0)),
        pl.BlockSpec((1, MLA_HEADS, MLA_V_AUG, TOK_TILE), lambda b, j: (b, 0, 0, j)),
        pl.BlockSpec((1, TOK_TILE, LANE), lambda b, j: (b, j, 0)),
        pl.BlockSpec((1, SWA_KV_HEADS, TOK_TILE // LANE, SWA_HD, LANE), lambda b, j: (b, 0, j, 0, 0)),
    ]
    if latent:
        extra = [weights[k] for k in ("wqs", "gqa", "wuq", "gmq", "gsq")]
        ins += extra
        in_specs += [_full(w.shape) for w in extra]
        feat = [tables[k] for k in ("mcos", "msin", "scos", "ssin")]
        ins += feat
        in_specs += [pl.BlockSpec((t.shape[0], TOK_TILE), lambda b, j: (0, j)) for t in feat]
        tok = [tables[k] for k in ("kc", "ks1", "ks2", "sc", "ss1", "ss2")]
        ins += tok
        in_specs += [pl.BlockSpec((TOK_TILE, LANE), lambda b, j: (j, 0)) for _ in tok]
        out_shape += [
            jax.ShapeDtypeStruct((bsz, MLA_HEADS, LANE, n), BF16),
            jax.ShapeDtypeStruct((bsz, SWA_HEADS, LANE, n), BF16),
        ]
        out_specs += [
            pl.BlockSpec((1, MLA_HEADS, LANE, TOK_TILE), lambda b, j: (b, 0, 0, j)),
            pl.BlockSpec((1, SWA_HEADS, LANE, TOK_TILE), lambda b, j: (b, 0, 0, j)),
        ]
    return pl.pallas_call(
        functools.partial(_proj_kernel, latent=latent),
        grid=(bsz, nt),
        in_specs=in_specs,
        out_specs=out_specs,
        out_shape=out_shape,
        compiler_params=pltpu.CompilerParams(vmem_limit_bytes=VMEM_LIMIT),
        name="project_latent" if latent else "project_context",
    )(*ins)


def _mla_kernel(qt_ref, kc_ref, kl_ref, vct_ref, vlt_ref, o_ref, ot_scr, s_scr, p_scr, *, seq):
    n_ctx = kc_ref.shape[2]
    tq = qt_ref.shape[3]
    subs = ([(kc_ref, vct_ref, o) for o in range(0, n_ctx, MLA_SUB)]
            + [(kl_ref, vlt_ref, o) for o in range(0, seq, MLA_SUB)])
    n_stage = _mla_stage_count(len(subs))
    first = len(subs) - (n_stage - 1) * MLA_STAGE_SUBS
    bounds = [0] + [first + i * MLA_STAGE_SUBS for i in range(n_stage)]
    stage_subs = [subs[bounds[i]:bounds[i + 1]] for i in range(n_stage)]
    n_group = MLA_HEADS // MLA_HEAD_UNROLL
    assert (MLA_HEAD_UNROLL * n_stage) % 2 == 0

    def fold8(x, op):
        return op(x.reshape(x.shape[0] // 8, 8, tq), axis=0)

    def scores_into(slot, j, hd, sub, m8):
        k_ref, _, off = sub
        sj = _dot(k_ref[0, hd, off:off + MLA_SUB, :], qt_ref[0, hd])
        s_scr[slot, j * MLA_SUB:(j + 1) * MLA_SUB, :] = sj
        mj = fold8(sj, jnp.max)
        return mj if m8 is None else jnp.maximum(m8, mj)

    def group_body(i, m8_cur, last_group):
        stages = [(i * MLA_HEAD_UNROLL + u, st) for u in range(MLA_HEAD_UNROLL) for st in range(n_stage)]
        m = acc = None
        for k, (hd, st) in enumerate(stages):
            slot = k % 2
            cur = stage_subs[st]
            if k + 1 < len(stages):
                nxt = stages[k + 1]
            elif not last_group:
                nxt = ((i + 1) * MLA_HEAD_UNROLL, 0)
            else:
                nxt = None
            nxt_subs = stage_subs[nxt[1]] if nxt else []
            m_blk = jnp.max(m8_cur, axis=0, keepdims=True)
            if st == 0:
                m_new = m_blk
            else:
                m_new = jnp.maximum(m, m_blk)
                alpha = jnp.exp2(m - m_new)
            mb = jnp.broadcast_to(m_new, (MLA_CHUNK, tq))
            m8_next = None
            pv = None
            for j in range(min(MLA_QK_LEAD, len(nxt_subs))):
                m8_next = scores_into(1 - slot, j, nxt[0], nxt_subs[j], m8_next)
            for j in range(max(len(cur), len(nxt_subs) - MLA_QK_LEAD)):
                jq = j + MLA_QK_LEAD
                if j < len(cur):
                    base = j * MLA_SUB
                    for r in range(base, base + MLA_SUB, MLA_CHUNK):
                        x = s_scr[slot, r:r + MLA_CHUNK, :] - mb
                        p_scr[slot, r:r + MLA_CHUNK, :] = jnp.exp2(x).astype(BF16)
                    _, v_ref, off = cur[j]
                    d = _dot(v_ref[0, hd, :, off:off + MLA_SUB],
                             p_scr[slot, base:base + MLA_SUB, :])
                    pv = d if pv is None else pv + d
                if jq < len(nxt_subs):
                    m8_next = scores_into(1 - slot, jq, nxt[0], nxt_subs[jq], m8_next)
            acc = pv if st == 0 else alpha * acc + pv
            m = m_new
            if st == n_stage - 1:
                row = pl.multiple_of(hd * MLA_V, MLA_V)
                ot_scr[pl.ds(row, MLA_V), :] = acc[0:MLA_V] / acc[MLA_V:MLA_V + 1]
            m8_cur = m8_next
        return m8_cur

    m8_first = None
    for j, sub in enumerate(stage_subs[0]):
        m8_first = scores_into(0, j, 0, sub, m8_first)
    m8_last = lax.fori_loop(0, n_group - 1, functools.partial(group_body, last_group=False), m8_first)
    group_body(n_group - 1, m8_last, last_group=True)
    o_ref[0] = ot_scr[...].T.astype(BF16)


def _mla_stage_count(n_sub):
    return max(1, n_sub // MLA_STAGE_SUBS)


def _mla_attention(qmt, km_c, km_l, vmt_c, vmt_l):
    bsz, _, _, s = qmt.shape
    n_ctx = km_c.shape[2]
    nt = s // TOK_TILE
    n_sub = (n_ctx + s) // MLA_SUB
    stage_keys = (n_sub - (_mla_stage_count(n_sub) - 1) * MLA_STAGE_SUBS) * MLA_SUB
    return pl.pallas_call(
        functools.partial(_mla_kernel, seq=s),
        grid=(bsz, nt),
        in_specs=[
            pl.BlockSpec((1, MLA_HEADS, LANE, TOK_TILE), lambda b, j: (b, 0, 0, j)),
            pl.BlockSpec((1, MLA_HEADS, n_ctx, LANE), lambda b, j: (b, 0, 0, 0)),
            pl.BlockSpec((1, MLA_HEADS, s, LANE), lambda b, j: (b, 0, 0, 0)),
            pl.BlockSpec((1, MLA_HEADS, MLA_V_AUG, n_ctx), lambda b, j: (b, 0, 0, 0)),
            pl.BlockSpec((1, MLA_HEADS, MLA_V_AUG, s), lambda b, j: (b, 0, 0, 0)),
        ],
        out_specs=pl.BlockSpec((1, TOK_TILE, MLA_HEADS * MLA_V), lambda b, j: (b, j, 0)),
        out_shape=jax.ShapeDtypeStruct((bsz, s, MLA_HEADS * MLA_V), BF16),
        scratch_shapes=[
            pltpu.VMEM((MLA_HEADS * MLA_V, TOK_TILE), F32),
            pltpu.VMEM((2, stage_keys, TOK_TILE), F32),
            pltpu.VMEM((2, stage_keys, TOK_TILE), BF16),
        ],
        compiler_params=pltpu.CompilerParams(vmem_limit_bytes=VMEM_LIMIT),
        name="mla_attention",
    )(qmt, km_c, km_l, vmt_c, vmt_l)


def _swa_kernel(sink_ref, qt_ref, kc_ref, kl_ref, vct_ref, vlt_ref, o_ref, ot_scr, *, seq):
    g = pl.program_id(1)
    j = pl.program_id(2)
    start = jnp.clip(j * TOK_TILE - WINDOW, 0, seq - SWA_WIN)
    start = pl.multiple_of(start, LANE)
    t0 = start // LANE
    kwin = kl_ref[0, pl.ds(start, SWA_WIN), :]
    kctx = kc_ref[0]
    vwin = jnp.concatenate([vlt_ref[0, 0, t0 + i] for i in range(SWA_WIN // LANE)], axis=1)
    vctx = jnp.concatenate([vct_ref[0, 0, i] for i in range(vct_ref.shape[2])], axis=1)
    qpos = j * TOK_TILE + lax.broadcasted_iota(jnp.int32, (SWA_WIN, TOK_TILE), 1)
    kpos = start + lax.broadcasted_iota(jnp.int32, (SWA_WIN, TOK_TILE), 0)
    valid = jnp.abs(qpos - kpos) <= WINDOW
    for hh in range(SWA_GROUP):
        qt = qt_ref[0, hh]
        sb = jnp.where(valid, _dot(kwin, qt), NEG_INF)
        sc = _dot(kctx, qt)
        sk = sink_ref[g * SWA_GROUP + hh] * LOG2E
        m = jnp.maximum(jnp.maximum(jnp.max(sb, axis=0, keepdims=True),
                                    jnp.max(sc, axis=0, keepdims=True)), sk)
        pb = jnp.exp2(sb - m)
        pc = jnp.exp2(sc - m)
        l = (jnp.sum(pb, axis=0, keepdims=True) + jnp.sum(pc, axis=0, keepdims=True)
             + jnp.exp2(sk - m))
        ot = _dot(vwin, pb.astype(BF16)) + _dot(vctx, pc.astype(BF16))
        ot_scr[hh * SWA_HD:(hh + 1) * SWA_HD, :] = ot / l
    o_ref[0] = ot_scr[...].T.astype(BF16)


def _swa_attention(sink, qst, ks_c, ks_l, vst_c, vst_l):
    bsz, _, _, s = qst.shape
    n_ctx = ks_c.shape[1]
    nt = s // TOK_TILE
    return pl.pallas_call(
        functools.partial(_swa_kernel, seq=s),
        grid=(bsz, SWA_KV_HEADS, nt),
        in_specs=[
            pl.BlockSpec(memory_space=pltpu.SMEM),
            pl.BlockSpec((1, SWA_GROUP, LANE, TOK_TILE), lambda b, g, j: (b, g, 0, j)),
            pl.BlockSpec((1, n_ctx, LANE), lambda b, g, j: (b, 0, 0)),
            pl.BlockSpec((1, s, LANE), lambda b, g, j: (b, 0, 0)),
            pl.BlockSpec((1, 1, n_ctx // LANE, SWA_HD, LANE), lambda b, g, j: (b, g, 0, 0, 0)),
            pl.BlockSpec((1, 1, s // LANE, SWA_HD, LANE), lambda b, g, j: (b, g, 0, 0, 0)),
        ],
        out_specs=pl.BlockSpec((1, TOK_TILE, SWA_GROUP * SWA_HD), lambda b, g, j: (b, j, g)),
        out_shape=jax.ShapeDtypeStruct((bsz, s, SWA_HEADS * SWA_HD), BF16),
        scratch_shapes=[pltpu.VMEM((SWA_GROUP * SWA_HD, TOK_TILE), F32)],
        compiler_params=pltpu.CompilerParams(vmem_limit_bytes=VMEM_LIMIT),
        name="swa_attention",
    )(sink, qst, ks_c, ks_l, vst_c, vst_l)


def _mlp_kernel(x_ref, mm_ref, ms_ref, mod_ref, wo_ref, gmlp_ref, w1_ref, w2_ref, o_ref, acc_ref):
    half = MLA_HEADS * MLA_V
    y1 = _dot(mm_ref[0], wo_ref[0:half, :]) + _dot(ms_ref[0], wo_ref[half:, :])
    x1 = x_ref[0] + mod_ref[0, 2:3, :] * y1
    h2 = (_rms_rows(x1, D_MODEL) * gmlp_ref[...] * (1.0 + mod_ref[0, 4:5, :])
          + mod_ref[0, 3:4, :]).astype(BF16)
    for c in range(D_FF // FF_CHUNK):
        a = jnp.maximum(_dot(h2, w1_ref[:, c * FF_CHUNK:(c + 1) * FF_CHUNK]), 0.0)
        part = _dot((a * a).astype(BF16), w2_ref[c * FF_CHUNK:(c + 1) * FF_CHUNK, :])
        if c == 0:
            acc_ref[...] = part
        else:
            acc_ref[...] += part
    o_ref[0] = x1 + mod_ref[0, 5:6, :] * acc_ref[...]


def _outproj_mlp(x, mix_m, mix_s, mod3, w_out, g_mlp, w1, w2):
    bsz, s, _ = x.shape
    half = MLA_HEADS * MLA_V
    const = lambda b, j: (0, 0)
    single = pl.Buffered(1)
    return pl.pallas_call(
        _mlp_kernel,
        grid=(bsz, s // MLP_TILE),
        in_specs=[
            pl.BlockSpec((1, MLP_TILE, D_MODEL), lambda b, j: (b, j, 0)),
            pl.BlockSpec((1, MLP_TILE, half), lambda b, j: (b, j, 0)),
            pl.BlockSpec((1, MLP_TILE, half), lambda b, j: (b, j, 0)),
            pl.BlockSpec((1, 6, D_MODEL), lambda b, j: (b, 0, 0)),
            pl.BlockSpec((D_MODEL, D_MODEL), const, pipeline_mode=single),
            pl.BlockSpec((1, D_MODEL), const),
            pl.BlockSpec((D_MODEL, D_FF), const, pipeline_mode=single),
            pl.BlockSpec((D_FF, D_MODEL), const, pipeline_mode=single),
        ],
        out_specs=pl.BlockSpec((1, MLP_TILE, D_MODEL), lambda b, j: (b, j, 0)),
        out_shape=jax.ShapeDtypeStruct((bsz, s, D_MODEL), F32),
        scratch_shapes=[pltpu.VMEM((MLP_TILE, D_MODEL), F32)],
        compiler_params=pltpu.CompilerParams(vmem_limit_bytes=VMEM_LIMIT),
        name="outproj_mlp",
    )(x, mix_m, mix_s, mod3, w_out, g_mlp, w1, w2)


def _rope_tables(seq):
    def cos_sin(rot_dim):
        n_freq = rot_dim // 4
        inv = ROPE_THETA ** (-jnp.arange(n_freq, dtype=F32) / n_freq)
        rows = seq // GRID_W
        row = jnp.repeat(jnp.arange(rows, dtype=F32), GRID_W)
        col = jnp.tile(jnp.arange(GRID_W, dtype=F32), rows)
        ang = jnp.concatenate([row[:, None] * inv, col[:, None] * inv], axis=-1)
        return jnp.cos(ang), jnp.sin(ang)

    mcos, msin = cos_sin(MLA_ROPE)
    scos, ssin = cos_sin(SWA_HD)
    z16 = jnp.zeros((seq, 16), F32)
    z32 = jnp.zeros((seq, 32), F32)
    one64 = jnp.ones((seq, 64), F32)
    kc = jnp.concatenate([one64, mcos, mcos, jnp.ones((seq, 32), F32)], axis=1)
    ks1 = jnp.concatenate([jnp.zeros((seq, 64), F32), -msin, z16, z32], axis=1)
    ks2 = jnp.concatenate([jnp.zeros((seq, 64), F32), z16, msin, z32], axis=1)
    sc = jnp.concatenate([scos, scos, scos, scos], axis=1)
    ss1 = jnp.concatenate([-ssin, z32, -ssin, z32], axis=1)
    ss2 = jnp.concatenate([z32, ssin, z32, ssin], axis=1)
    return dict(mcos=mcos.T, msin=msin.T, scos=scos.T, ssin=ssin.T,
                kc=kc, ks1=ks1, ks2=ks2, sc=sc, ss1=ss1, ss2=ss2)


def _prep_weights(g_attn, w_in, g_q_a, w_uq, g_kv_a, w_ukv, g_mla_q, g_mla_k, g_swa_q, g_swa_k):
    o_ckv = Q_LORA
    o_kr = o_ckv + KV_LORA
    o_qs = o_kr + MLA_ROPE
    o_ks = o_qs + SWA_HEADS * SWA_HD
    o_vs = o_ks + SWA_KV_HEADS * SWA_HD
    w_cq = w_in[:, :o_ckv]
    w_ckv = w_in[:, o_ckv:o_kr]
    w_kr = w_in[:, o_kr:o_qs]
    w_qs = w_in[:, o_qs:o_ks]
    w_ks = w_in[:, o_ks:o_vs]
    w_vs = w_in[:, o_vs:]
    w_kr_p = jnp.pad(w_kr, ((0, 0), (MLA_NOPE, LANE - MLA_QK)))
    wtok = jnp.concatenate([w_cq, w_ckv, w_kr_p, w_ks], axis=1).astype(BF16)
    w_ukv_h = w_ukv.reshape(KV_LORA, MLA_HEADS, MLA_NOPE + MLA_V)
    wuk = jnp.pad(w_ukv_h[:, :, :MLA_NOPE], ((0, 0), (0, 0), (0, LANE - MLA_NOPE)))
    wuk = wuk.reshape(KV_LORA, MLA_HEADS * LANE).astype(BF16)
    wuv = w_ukv_h[:, :, MLA_NOPE:].reshape(KV_LORA, MLA_HEADS * MLA_V).T.astype(BF16)
    w_uq_h = jnp.pad(w_uq.reshape(Q_LORA, MLA_HEADS, MLA_QK), ((0, 0), (0, 0), (0, LANE - MLA_QK)))
    wuq = w_uq_h.reshape(Q_LORA, MLA_HEADS * LANE).T.astype(BF16)
    return dict(
        gattn=g_attn[None, :], wtok=wtok, wvs=w_vs.T.astype(BF16), gkva=g_kv_a[None, :],
        wuk=wuk, wuv=wuv, gmk=jnp.pad(g_mla_k, (0, LANE - MLA_QK))[None, :],
        gsk=jnp.tile(g_swa_k, SWA_KV_HEADS)[None, :],
        wqs=w_qs.T.astype(BF16), gqa=g_q_a[None, :], wuq=wuq,
        gmq=jnp.pad(g_mla_q, (0, LANE - MLA_QK))[:, None], gsq=g_swa_q[:, None],
    )


def kernel(x, c, ctx, c_ctx, w_mod, b_mod, g_attn, w_in, g_q_a, w_uq, g_kv_a, w_ukv, g_mla_q, g_mla_k,
           g_swa_q, g_swa_k, swa_sink, w_out, g_mlp, w_mlp1, w_mlp2):
    bsz, seq, _ = x.shape
    assert w_mod.shape[0] == 1, "single-layer block"
    assert bsz < 8 and seq % MLP_TILE == 0 and ctx.shape[1] % TOK_TILE == 0

    cond = jnp.zeros((8, D_MODEL), F32).at[:bsz].set(c).at[bsz].set(c_ctx)
    mod = _modulation(cond, w_mod[0], b_mod[0][None, :])
    mod3 = mod.reshape(8, 6, D_MODEL)

    weights = _prep_weights(g_attn[0], w_in[0], g_q_a[0], w_uq[0], g_kv_a[0], w_ukv[0],
                            g_mla_q[0], g_mla_k[0], g_swa_q[0], g_swa_k[0])
    tables = _rope_tables(seq)

    km_l, vmt_l, ks_l, vst_l, qmt, qst = _project(x, mod3, None, weights, tables, latent=True)
    km_c, vmt_c, ks_c, vst_c = _project(ctx, mod3, bsz, weights, None, latent=False)

    mix_m = _mla_attention(qmt, km_c, km_l, vmt_c, vmt_l)
    mix_s = _swa_attention(swa_sink[0], qst, ks_c, ks_l, vst_c, vst_l)

    return _outproj_mlp(x, mix_m, mix_s, mod3, w_out[0].astype(BF16), g_mlp[0][None, :],
                        w_mlp1[0].astype(BF16), w_mlp2[0].astype(BF16))
```

```python
import functools
import math

import jax
import jax.numpy as jnp
from jax import lax
from jax.experimental import pallas as pl
from jax.experimental.pallas import tpu as pltpu

D_MODEL = 1024
GRID_W = 64
MLA_HEADS = 8
MLA_NOPE = 64
MLA_ROPE = 32
MLA_QK = MLA_NOPE + MLA_ROPE
MLA_V = 64
MLA_V_AUG = MLA_V + 16
Q_LORA = 256
KV_LORA = 128
SWA_HEADS = 8
SWA_KV_HEADS = 2
SWA_GROUP = SWA_HEADS // SWA_KV_HEADS
SWA_HD = 64
WINDOW = 128
D_FF = 4 * D_MODEL
ROPE_THETA = 10000.0
EPS = 1e-6
NEG_INF = -1e30
LOG2E = 1.4426950408889634
MLA_QSCALE = LOG2E / math.sqrt(MLA_QK)
SWA_QSCALE = LOG2E / math.sqrt(SWA_HD)

LANE = 128
TOK_TILE = 256
MLP_TILE = 512
FF_CHUNK = 1024
SWA_WIN = 512
MLA_SUB = 256
MLA_CHUNK = 32
MLA_QK_LEAD = 3
VMEM_LIMIT = 56 * 1024 * 1024

F32 = jnp.float32
BF16 = jnp.bfloat16
NT_DIMS = (((1,), (1,)), ((), ()))


def _dot(a, b):
    return jnp.dot(a, b, preferred_element_type=F32)


def _dot_nt(a, b):
    return lax.dot_general(a, b, NT_DIMS, preferred_element_type=F32)


def _mod_kernel(cond_ref, w_ref, b_ref, o_ref):
    cnd = cond_ref[...]
    act = cnd * jax.nn.sigmoid(cnd)
    o_ref[...] = _dot(act.astype(BF16), w_ref[...].astype(BF16)) + b_ref[...]


def _modulation(cond, w_mod, b_mod):
    n = w_mod.shape[1]
    tn = 1024
    return pl.pallas_call(
        _mod_kernel,
        grid=(n // tn,),
        in_specs=[
            pl.BlockSpec((8, D_MODEL), lambda i: (0, 0)),
            pl.BlockSpec((D_MODEL, tn), lambda i: (0, i)),
            pl.BlockSpec((1, tn), lambda i: (0, i)),
        ],
        out_specs=pl.BlockSpec((8, tn), lambda i: (0, i)),
        out_shape=jax.ShapeDtypeStruct((8, n), F32),
        name="modulation",
    )(cond, w_mod, b_mod)


def _rms_rows(x, n):
    return x * lax.rsqrt(jnp.sum(x * x, axis=-1, keepdims=True) * (1.0 / n) + EPS)


def _rms_cols(x, n):
    return x * lax.rsqrt(jnp.sum(x * x, axis=0, keepdims=True) * (1.0 / n) + EPS)


def _proj_kernel(*refs, latent):
    if latent:
        (x_ref, mod_ref, gattn_ref, wtok_ref, wvs_ref, gkva_ref, wuk_ref, wuv_ref, gmk_ref, gsk_ref,
         wqs_ref, gqa_ref, wuq_ref, gmq_ref, gsq_ref,
         mcos_ref, msin_ref, scos_ref, ssin_ref, kc_ref, ks1_ref, ks2_ref, sc_ref, ss1_ref, ss2_ref,
         km_ref, vmt_ref, ks_ref, vst_ref, qmt_ref, qst_ref) = refs
    else:
        (x_ref, mod_ref, gattn_ref, wtok_ref, wvs_ref, gkva_ref, wuk_ref, wuv_ref, gmk_ref, gsk_ref,
         km_ref, vmt_ref, ks_ref, vst_ref) = refs

    x = x_ref[0]
    y = _rms_rows(x, D_MODEL) * gattn_ref[...]
    h = (y * (1.0 + mod_ref[0, 1:2, :]) + mod_ref[0, 0:1, :]).astype(BF16)

    zt = _dot(h, wtok_ref[...])
    cq = zt[:, 0:256]
    ckv = zt[:, 256:384]
    krp = zt[:, 384:512]
    ksw = zt[:, 512:640]

    ckvn = (_rms_rows(ckv, KV_LORA) * gkva_ref[...]).astype(BF16)
    kpre = _dot(ckvn, wuk_ref[...])
    vt = _dot_nt(wuv_ref[...], ckvn)
    gmk = gmk_ref[...]
    pad_rows = MLA_V_AUG - MLA_V
    ones_row = (lax.broadcasted_iota(jnp.int32, (pad_rows, TOK_TILE), 0) == 0).astype(BF16)
    for hd in range(MLA_HEADS):
        kh = kpre[:, hd * LANE:(hd + 1) * LANE] + krp
        kn = _rms_rows(kh, MLA_QK) * gmk
        if latent:
            kn = (kn * kc_ref[...] + pltpu.roll(kn, 112, 1) * ks1_ref[...]
                  + pltpu.roll(kn, 16, 1) * ks2_ref[...])
        km_ref[0, hd] = kn.astype(BF16)
        vmt_ref[0, hd, 0:MLA_V, :] = vt[hd * MLA_V:(hd + 1) * MLA_V].astype(BF16)
        vmt_ref[0, hd, MLA_V:MLA_V_AUG, :] = ones_row

    lane = lax.broadcasted_iota(jnp.int32, ksw.shape, 1)
    first = lane < SWA_HD
    sq = ksw * ksw
    ss0 = jnp.sum(jnp.where(first, sq, 0.0), axis=-1, keepdims=True)
    ss1 = jnp.sum(jnp.where(first, 0.0, sq), axis=-1, keepdims=True)
    rr = jnp.where(first, lax.rsqrt(ss0 * (1.0 / SWA_HD) + EPS), lax.rsqrt(ss1 * (1.0 / SWA_HD) + EPS))
    ksn = ksw * rr * gsk_ref[...]
    if latent:
        ksn = (ksn * sc_ref[...] + pltpu.roll(ksn, 96, 1) * ss1_ref[...]
               + pltpu.roll(ksn, 32, 1) * ss2_ref[...])
    ks_ref[0] = ksn.astype(BF16)
    vst = _dot_nt(wvs_ref[...], h)
    for g in range(SWA_KV_HEADS):
        for t in range(TOK_TILE // LANE):
            vst_ref[0, g, t] = vst[g * SWA_HD:(g + 1) * SWA_HD, t * LANE:(t + 1) * LANE].astype(BF16)

    if not latent:
        return

    cqn = (_rms_rows(cq, Q_LORA) * gqa_ref[...]).astype(BF16)
    qt = _dot_nt(wuq_ref[...], cqn)
    gmq = gmq_ref[...]
    mcos = mcos_ref[...]
    msin = msin_ref[...]
    for hd in range(MLA_HEADS):
        qn = _rms_cols(qt[hd * LANE:(hd + 1) * LANE], MLA_QK) * (gmq * MLA_QSCALE)
        x1 = qn[64:80]
        x2 = qn[80:96]
        qmt_ref[0, hd, 0:64, :] = qn[0:64].astype(BF16)
        qmt_ref[0, hd, 64:80, :] = (x1 * mcos - x2 * msin).astype(BF16)
        qmt_ref[0, hd, 80:96, :] = (x2 * mcos + x1 * msin).astype(BF16)
        qmt_ref[0, hd, 96:128, :] = jnp.zeros((32, TOK_TILE), BF16)

    qst = _dot_nt(wqs_ref[...], h)
    gsq = gsq_ref[...]
    scos = scos_ref[...]
    ssin = ssin_ref[...]
    for hd in range(SWA_HEADS):
        qn = _rms_cols(qst[hd * SWA_HD:(hd + 1) * SWA_HD], SWA_HD) * (gsq * SWA_QSCALE)
        x1 = qn[0:32]
        x2 = qn[32:64]
        g = hd // SWA_GROUP
        base = g * SWA_HD
        other = (1 - g) * SWA_HD
        qst_ref[0, hd, base:base + 32, :] = (x1 * scos - x2 * ssin).astype(BF16)
        qst_ref[0, hd, base + 32:base + 64, :] = (x2 * scos + x1 * ssin).astype(BF16)
        qst_ref[0, hd, other:other + 64, :] = jnp.zeros((64, TOK_TILE), BF16)


def _full(shape):
    nd = len(shape)
    return pl.BlockSpec(shape, lambda b, j: (0,) * nd)


def _project(x, mod3, mod_row, weights, tables, latent):
    bsz, n, _ = x.shape
    nt = n // TOK_TILE
    if mod_row is None:
        mod_map = lambda b, j: (b, 0, 0)
    else:
        mod_map = lambda b, j: (mod_row, 0, 0)
    common = [weights[k] for k in ("gattn", "wtok", "wvs", "gkva", "wuk", "wuv", "gmk", "gsk")]
    ins = [x, mod3] + common
    in_specs = [
        pl.BlockSpec((1, TOK_TILE, D_MODEL), lambda b, j: (b, j, 0)),
        pl.BlockSpec((1, 6, D_MODEL), mod_map),
    ] + [_full(w.shape) for w in common]
    out_shape = [
        jax.ShapeDtypeStruct((bsz, MLA_HEADS, n, LANE), BF16),
        jax.ShapeDtypeStruct((bsz, MLA_HEADS, MLA_V_AUG, n), BF16),
        jax.ShapeDtypeStruct((bsz, n, LANE), BF16),
        jax.ShapeDtypeStruct((bsz, SWA_KV_HEADS, n // LANE, SWA_HD, LANE), BF16),
    ]
    out_specs = [
        pl.BlockSpec((1, MLA_HEADS, TOK_TILE, LANE), lambda b, j: (b, 0, j, 0)),
        pl.BlockSpec((1, MLA_HEADS, MLA_V_AUG, TOK_TILE), lambda b, j: (b, 0, 0, j)),
        pl.BlockSpec((1, TOK_TILE, LANE), lambda b, j: (b, j, 0)),
        pl.BlockSpec((1, SWA_KV_HEADS, TOK_TILE // LANE, SWA_HD, LANE), lambda b, j: (b, 0, j, 0, 0)),
    ]
    if latent:
        extra = [weights[k] for k in ("wqs", "gqa", "wuq", "gmq", "gsq")]
        ins += extra
        in_specs += [_full(w.shape) for w in extra]
        feat = [tables[k] for k in ("mcos", "msin", "scos", "ssin")]
        ins += feat
        in_specs += [pl.BlockSpec((t.shape[0], TOK_TILE), lambda b, j: (0, j)) for t in feat]
        tok = [tables[k] for k in ("kc", "ks1", "ks2", "sc", "ss1", "ss2")]
        ins += tok
        in_specs += [pl.BlockSpec((TOK_TILE, LANE), lambda b, j: (j, 0)) for _ in tok]
        out_shape += [
            jax.ShapeDtypeStruct((bsz, MLA_HEADS, LANE, n), BF16),
            jax.ShapeDtypeStruct((bsz, SWA_HEADS, LANE, n), BF16),
        ]
        out_specs += [
            pl.BlockSpec((1, MLA_HEADS, LANE, TOK_TILE), lambda b, j: (b, 0, 0, j)),
            pl.BlockSpec((1, SWA_HEADS, LANE, TOK_TILE), lambda b, j: (b, 0, 0, j)),
        ]
    return pl.pallas_call(
        functools.partial(_proj_kernel, latent=latent),
        grid=(bsz, nt),
        in_specs=in_specs,
        out_specs=out_specs,
        out_shape=out_shape,
        compiler_params=pltpu.CompilerParams(vmem_limit_bytes=VMEM_LIMIT),
        name="project_latent" if latent else "project_context",
    )(*ins)


def _mla_kernel(qt_ref, kc_ref, kl_ref, vct_ref, vlt_ref, o_ref, ot_scr, s_scr, p_scr, *, seq):
    n_ctx = kc_ref.shape[2]
    tq = qt_ref.shape[3]
    subs = ([(kc_ref, vct_ref, o) for o in range(0, n_ctx, MLA_SUB)]
            + [(kl_ref, vlt_ref, o) for o in range(0, seq, MLA_SUB)])
    n_sub = len(subs)

    def scores_into(slot, j, hd, m8):
        k_ref, _, off = subs[j]
        sj = _dot(k_ref[0, hd, off:off + MLA_SUB, :], qt_ref[0, hd])
        s_scr[slot, j * MLA_SUB:(j + 1) * MLA_SUB, :] = sj
        mj = jnp.max(sj.reshape(MLA_SUB // 8, 8, tq), axis=0)
        return mj if m8 is None else jnp.maximum(m8, mj)

    def head_body(hd, m8_cur, slot, has_next):
        mb = jnp.broadcast_to(jnp.max(m8_cur, axis=0, keepdims=True), (MLA_CHUNK, tq))
        n_q = n_sub if has_next else 0
        m8_next = None
        acc = None
        for j in range(min(MLA_QK_LEAD, n_q)):
            m8_next = scores_into(1 - slot, j, hd + 1, m8_next)
        for j in range(n_sub):
            base = j * MLA_SUB
            for r in range(base, base + MLA_SUB, MLA_CHUNK):
                x = s_scr[slot, r:r + MLA_CHUNK, :] - mb
                p_scr[r:r + MLA_CHUNK, :] = jnp.exp2(x).astype(BF16)
            _, v_ref, off = subs[j]
            d = _dot(v_ref[0, hd, :, off:off + MLA_SUB], p_scr[base:base + MLA_SUB, :])
            acc = d if acc is None else acc + d
            if j + MLA_QK_LEAD < n_q:
                m8_next = scores_into(1 - slot, j + MLA_QK_LEAD, hd + 1, m8_next)
        row = pl.multiple_of(hd * MLA_V, MLA_V)
        ot_scr[pl.ds(row, MLA_V), :] = acc[0:MLA_V] / acc[MLA_V:MLA_V + 1]
        return m8_next

    def head_step(hd, m8):
        return lax.cond(lax.rem(hd, 2) == 0,
                        lambda m: head_body(hd, m, 0, True),
                        lambda m: head_body(hd, m, 1, True), m8)

    m8_first = None
    for j in range(n_sub):
        m8_first = scores_into(0, j, 0, m8_first)
    m8_last = lax.fori_loop(0, MLA_HEADS - 1, head_step, m8_first)
    head_body(MLA_HEADS - 1, m8_last, (MLA_HEADS - 1) % 2, False)
    o_ref[0] = ot_scr[...].T.astype(BF16)


def _mla_attention(qmt, km_c, km_l, vmt_c, vmt_l):
    bsz, _, _, s = qmt.shape
    n_ctx = km_c.shape[2]
    nt = s // TOK_TILE
    n_keys = n_ctx + s
    return pl.pallas_call(
        functools.partial(_mla_kernel, seq=s),
        grid=(bsz, nt),
        in_specs=[
            pl.BlockSpec((1, MLA_HEADS, LANE, TOK_TILE), lambda b, j: (b, 0, 0, j)),
            pl.BlockSpec((1, MLA_HEADS, n_ctx, LANE), lambda b, j: (b, 0, 0, 0)),
            pl.BlockSpec((1, MLA_HEADS, s, LANE), lambda b, j: (b, 0, 0, 0)),
            pl.BlockSpec((1, MLA_HEADS, MLA_V_AUG, n_ctx), lambda b, j: (b, 0, 0, 0)),
            pl.BlockSpec((1, MLA_HEADS, MLA_V_AUG, s), lambda b, j: (b, 0, 0, 0)),
        ],
        out_specs=pl.BlockSpec((1, TOK_TILE, MLA_HEADS * MLA_V), lambda b, j: (b, j, 0)),
        out_shape=jax.ShapeDtypeStruct((bsz, s, MLA_HEADS * MLA_V), BF16),
        scratch_shapes=[
            pltpu.VMEM((MLA_HEADS * MLA_V, TOK_TILE), F32),
            pltpu.VMEM((2, n_keys, TOK_TILE), F32),
            pltpu.VMEM((n_keys, TOK_TILE), BF16),
        ],
        compiler_params=pltpu.CompilerParams(vmem_limit_bytes=VMEM_LIMIT),
        name="mla_attention",
    )(qmt, km_c, km_l, vmt_c, vmt_l)


def _swa_kernel(sink_ref, qt_ref, kc_ref, kl_ref, vct_ref, vlt_ref, o_ref, ot_scr, *, seq):
    g = pl.program_id(1)
    j = pl.program_id(2)
    start = jnp.clip(j * TOK_TILE - WINDOW, 0, seq - SWA_WIN)
    start = pl.multiple_of(start, LANE)
    t0 = start // LANE
    kwin = kl_ref[0, pl.ds(start, SWA_WIN), :]
    kctx = kc_ref[0]
    vwin = jnp.concatenate([vlt_ref[0, 0, t0 + i] for i in range(SWA_WIN // LANE)], axis=1)
    vctx = jnp.concatenate([vct_ref[0, 0, i] for i in range(vct_ref.shape[2])], axis=1)
    qpos = j * TOK_TILE + lax.broadcasted_iota(jnp.int32, (SWA_WIN, TOK_TILE), 1)
    kpos = start + lax.broadcasted_iota(jnp.int32, (SWA_WIN, TOK_TILE), 0)
    valid = jnp.abs(qpos - kpos) <= WINDOW
    for hh in range(SWA_GROUP):
        qt = qt_ref[0, hh]
        sb = jnp.where(valid, _dot(kwin, qt), NEG_INF)
        sc = _dot(kctx, qt)
        sk = sink_ref[g * SWA_GROUP + hh] * LOG2E
        m = jnp.maximum(jnp.maximum(jnp.max(sb, axis=0, keepdims=True),
                                    jnp.max(sc, axis=0, keepdims=True)), sk)
        pb = jnp.exp2(sb - m)
        pc = jnp.exp2(sc - m)
        l = (jnp.sum(pb, axis=0, keepdims=True) + jnp.sum(pc, axis=0, keepdims=True)
             + jnp.exp2(sk - m))
        ot = _dot(vwin, pb.astype(BF16)) + _dot(vctx, pc.astype(BF16))
        ot_scr[hh * SWA_HD:(hh + 1) * SWA_HD, :] = ot / l
    o_ref[0] = ot_scr[...].T.astype(BF16)


def _swa_attention(sink, qst, ks_c, ks_l, vst_c, vst_l):
    bsz, _, _, s = qst.shape
    n_ctx = ks_c.shape[1]
    nt = s // TOK_TILE
    return pl.pallas_call(
        functools.partial(_swa_kernel, seq=s),
        grid=(bsz, SWA_KV_HEADS, nt),
        in_specs=[
            pl.BlockSpec(memory_space=pltpu.SMEM),
            pl.BlockSpec((1, SWA_GROUP, LANE, TOK_TILE), lambda b, g, j: (b, g, 0, j)),
            pl.BlockSpec((1, n_ctx, LANE), lambda b, g, j: (b, 0, 0)),
            pl.BlockSpec((1, s, LANE), lambda b, g, j: (b, 0, 0)),
            pl.BlockSpec((1, 1, n_ctx // LANE, SWA_HD, LANE), lambda b, g, j: (b, g, 0, 0, 0)),
            pl.BlockSpec((1, 1, s // LANE, SWA_HD, LANE), lambda b, g, j: (b, g, 0, 0, 0)),
        ],
        out_specs=pl.BlockSpec((1, TOK_TILE, SWA_GROUP * SWA_HD), lambda b, g, j: (b, j, g)),
        out_shape=jax.ShapeDtypeStruct((bsz, s, SWA_HEADS * SWA_HD), BF16),
        scratch_shapes=[pltpu.VMEM((SWA_GROUP * SWA_HD, TOK_TILE), F32)],
        compiler_params=pltpu.CompilerParams(vmem_limit_bytes=VMEM_LIMIT),
        name="swa_attention",
    )(sink, qst, ks_c, ks_l, vst_c, vst_l)


def _mlp_kernel(x_ref, mm_ref, ms_ref, mod_ref, wo_ref, gmlp_ref, w1_ref, w2_ref, o_ref, acc_ref):
    half = MLA_HEADS * MLA_V
    y1 = _dot(mm_ref[0], wo_ref[0:half, :]) + _dot(ms_ref[0], wo_ref[half:, :])
    x1 = x_ref[0] + mod_ref[0, 2:3, :] * y1
    h2 = (_rms_rows(x1, D_MODEL) * gmlp_ref[...] * (1.0 + mod_ref[0, 4:5, :])
          + mod_ref[0, 3:4, :]).astype(BF16)
    for c in range(D_FF // FF_CHUNK):
        a = jnp.maximum(_dot(h2, w1_ref[:, c * FF_CHUNK:(c + 1) * FF_CHUNK]), 0.0)
        part = _dot((a * a).astype(BF16), w2_ref[c * FF_CHUNK:(c + 1) * FF_CHUNK, :])
        if c == 0:
            acc_ref[...] = part
        else:
            acc_ref[...] += part
    o_ref[0] = x1 + mod_ref[0, 5:6, :] * acc_ref[...]


def _outproj_mlp(x, mix_m, mix_s, mod3, w_out, g_mlp, w1, w2):
    bsz, s, _ = x.shape
    half = MLA_HEADS * MLA_V
    const = lambda b, j: (0, 0)
    single = pl.Buffered(1)
    return pl.pallas_call(
        _mlp_kernel,
        grid=(bsz, s // MLP_TILE),
        in_specs=[
            pl.BlockSpec((1, MLP_TILE, D_MODEL), lambda b, j: (b, j, 0)),
            pl.BlockSpec((1, MLP_TILE, half), lambda b, j: (b, j, 0)),
            pl.BlockSpec((1, MLP_TILE, half), lambda b, j: (b, j, 0)),
            pl.BlockSpec((1, 6, D_MODEL), lambda b, j: (b, 0, 0)),
            pl.BlockSpec((D_MODEL, D_MODEL), const, pipeline_mode=single),
            pl.BlockSpec((1, D_MODEL), const),
            pl.BlockSpec((D_MODEL, D_FF), const, pipeline_mode=single),
            pl.BlockSpec((D_FF, D_MODEL), const, pipeline_mode=single),
        ],
        out_specs=pl.BlockSpec((1, MLP_TILE, D_MODEL), lambda b, j: (b, j, 0)),
        out_shape=jax.ShapeDtypeStruct((bsz, s, D_MODEL), F32),
        scratch_shapes=[pltpu.VMEM((MLP_TILE, D_MODEL), F32)],
        compiler_params=pltpu.CompilerParams(vmem_limit_bytes=VMEM_LIMIT),
        name="outproj_mlp",
    )(x, mix_m, mix_s, mod3, w_out, g_mlp, w1, w2)


def _rope_tables(seq):
    def cos_sin(rot_dim):
        n_freq = rot_dim // 4
        inv = ROPE_THETA ** (-jnp.arange(n_freq, dtype=F32) / n_freq)
        rows = seq // GRID_W
        row = jnp.repeat(jnp.arange(rows, dtype=F32), GRID_W)
        col = jnp.tile(jnp.arange(GRID_W, dtype=F32), rows)
        ang = jnp.concatenate([row[:, None] * inv, col[:, None] * inv], axis=-1)
        return jnp.cos(ang), jnp.sin(ang)

    mcos, msin = cos_sin(MLA_ROPE)
    scos, ssin = cos_sin(SWA_HD)
    z16 = jnp.zeros((seq, 16), F32)
    z32 = jnp.zeros((seq, 32), F32)
    one64 = jnp.ones((seq, 64), F32)
    kc = jnp.concatenate([one64, mcos, mcos, jnp.ones((seq, 32), F32)], axis=1)
    ks1 = jnp.concatenate([jnp.zeros((seq, 64), F32), -msin, z16, z32], axis=1)
    ks2 = jnp.concatenate([jnp.zeros((seq, 64), F32), z16, msin, z32], axis=1)
    sc = jnp.concatenate([scos, scos, scos, scos], axis=1)
    ss1 = jnp.concatenate([-ssin, z32, -ssin, z32], axis=1)
    ss2 = jnp.concatenate([z32, ssin, z32, ssin], axis=1)
    return dict(mcos=mcos.T, msin=msin.T, scos=scos.T, ssin=ssin.T,
                kc=kc, ks1=ks1, ks2=ks2, sc=sc, ss1=ss1, ss2=ss2)


def _prep_weights(g_attn, w_in, g_q_a, w_uq, g_kv_a, w_ukv, g_mla_q, g_mla_k, g_swa_q, g_swa_k):
    o_ckv = Q_LORA
    o_kr = o_ckv + KV_LORA
    o_qs = o_kr + MLA_ROPE
    o_ks = o_qs + SWA_HEADS * SWA_HD
    o_vs = o_ks + SWA_KV_HEADS * SWA_HD
    w_cq = w_in[:, :o_ckv]
    w_ckv = w_in[:, o_ckv:o_kr]
    w_kr = w_in[:, o_kr:o_qs]
    w_qs = w_in[:, o_qs:o_ks]
    w_ks = w_in[:, o_ks:o_vs]
    w_vs = w_in[:, o_vs:]
    w_kr_p = jnp.pad(w_kr, ((0, 0), (MLA_NOPE, LANE - MLA_QK)))
    wtok = jnp.concatenate([w_cq, w_ckv, w_kr_p, w_ks], axis=1).astype(BF16)
    w_ukv_h = w_ukv.reshape(KV_LORA, MLA_HEADS, MLA_NOPE + MLA_V)
    wuk = jnp.pad(w_ukv_h[:, :, :MLA_NOPE], ((0, 0), (0, 0), (0, LANE - MLA_NOPE)))
    wuk = wuk.reshape(KV_LORA, MLA_HEADS * LANE).astype(BF16)
    wuv = w_ukv_h[:, :, MLA_NOPE:].reshape(KV_LORA, MLA_HEADS * MLA_V).T.astype(BF16)
    w_uq_h = jnp.pad(w_uq.reshape(Q_LORA, MLA_HEADS, MLA_QK), ((0, 0), (0, 0), (0, LANE - MLA_QK)))
    wuq = w_uq_h.reshape(Q_LORA, MLA_HEADS * LANE).T.astype(BF16)
    return dict(
        gattn=g_attn[None, :], wtok=wtok, wvs=w_vs.T.astype(BF16), gkva=g_kv_a[None, :],
        wuk=wuk, wuv=wuv, gmk=jnp.pad(g_mla_k, (0, LANE - MLA_QK))[None, :],
        gsk=jnp.tile(g_swa_k, SWA_KV_HEADS)[None, :],
        wqs=w_qs.T.astype(BF16), gqa=g_q_a[None, :], wuq=wuq,
        gmq=jnp.pad(g_mla_q, (0, LANE - MLA_QK))[:, None], gsq=g_swa_q[:, None],
    )


def kernel(x, c, ctx, c_ctx, w_mod, b_mod, g_attn, w_in, g_q_a, w_uq, g_kv_a, w_ukv, g_mla_q, g_mla_k,
           g_swa_q, g_swa_k, swa_sink, w_out, g_mlp, w_mlp1, w_mlp2):
    bsz, seq, _ = x.shape
    assert w_mod.shape[0] == 1, "single-layer block"
    assert bsz < 8 and seq % MLP_TILE == 0 and ctx.shape[1] % TOK_TILE == 0

    cond = jnp.zeros((8, D_MODEL), F32).at[:bsz].set(c).at[bsz].set(c_ctx)
    mod = _modulation(cond, w_mod[0], b_mod[0][None, :])
    mod3 = mod.reshape(8, 6, D_MODEL)

    weights = _prep_weights(g_attn[0], w_in[0], g_q_a[0], w_uq[0], g_kv_a[0], w_ukv[0],
                            g_mla_q[0], g_mla_k[0], g_swa_q[0], g_swa_k[0])
    tables = _rope_tables(seq)

    km_l, vmt_l, ks_l, vst_l, qmt, qst = _project(x, mod3, None, weights, tables, latent=True)
    km_c, vmt_c, ks_c, vst_c = _project(ctx, mod3, bsz, weights, None, latent=False)

    mix_m = _mla_attention(qmt, km_c, km_l, vmt_c, vmt_l)
    mix_s = _swa_attention(swa_sink[0], qst, ks_c, ks_l, vst_c, vst_l)

    return _outproj_mlp(x, mix_m, mix_s, mod3, w_out[0].astype(BF16), g_mlp[0][None, :],
                        w_mlp1[0].astype(BF16), w_mlp2[0].astype(BF16))
```

```python
import functools
import math

import jax
import jax.numpy as jnp
from jax import lax
from jax.experimental import pallas as pl
from jax.experimental.pallas import tpu as pltpu

D_MODEL = 1024
GRID_W = 64
MLA_HEADS = 8
MLA_NOPE = 64
MLA_ROPE = 32
MLA_QK = MLA_NOPE + MLA_ROPE
MLA_V = 64
MLA_V_AUG = MLA_V + 16
Q_LORA = 256
KV_LORA = 128
SWA_HEADS = 8
SWA_KV_HEADS = 2
SWA_GROUP = SWA_HEADS // SWA_KV_HEADS
SWA_HD = 64
SWA_V_AUG = SWA_HD + 16
WINDOW = 128
D_FF = 4 * D_MODEL
ROPE_THETA = 10000.0
EPS = 1e-6
NEG_INF = -1e30
LOG2E = 1.4426950408889634
MLA_QSCALE = LOG2E / math.sqrt(MLA_QK)
SWA_QSCALE = LOG2E / math.sqrt(SWA_HD)

LANE = 128
TOK_TILE = 256
MLP_TILE = 512
FF_CHUNK = 1024
SWA_WIN = 512
MLA_SUB = 256
SWA_SUB = 256
SWA_CHUNK = 32
SWA_QK_LEAD = 1
MLA_CHUNK = 32
MLA_QK_LEAD = 3
VMEM_LIMIT = 56 * 1024 * 1024

F32 = jnp.float32
BF16 = jnp.bfloat16
NT_DIMS = (((1,), (1,)), ((), ()))


def _dot(a, b):
    return jnp.dot(a, b, preferred_element_type=F32)


def _dot_nt(a, b):
    return lax.dot_general(a, b, NT_DIMS, preferred_element_type=F32)


def _mod_kernel(cond_ref, w_ref, b_ref, o_ref):
    cnd = cond_ref[...]
    act = cnd * jax.nn.sigmoid(cnd)
    o_ref[...] = _dot(act.astype(BF16), w_ref[...].astype(BF16)) + b_ref[...]


def _modulation(cond, w_mod, b_mod):
    n = w_mod.shape[1]
    tn = 1024
    return pl.pallas_call(
        _mod_kernel,
        grid=(n // tn,),
        in_specs=[
            pl.BlockSpec((8, D_MODEL), lambda i: (0, 0)),
            pl.BlockSpec((D_MODEL, tn), lambda i: (0, i)),
            pl.BlockSpec((1, tn), lambda i: (0, i)),
        ],
        out_specs=pl.BlockSpec((8, tn), lambda i: (0, i)),
        out_shape=jax.ShapeDtypeStruct((8, n), F32),
        name="modulation",
    )(cond, w_mod, b_mod)


def _rms_rows(x, n):
    return x * lax.rsqrt(jnp.sum(x * x, axis=-1, keepdims=True) * (1.0 / n) + EPS)


def _rms_cols(x, n):
    return x * lax.rsqrt(jnp.sum(x * x, axis=0, keepdims=True) * (1.0 / n) + EPS)


def _proj_kernel(*refs, latent):
    if latent:
        (x_ref, mod_ref, gattn_ref, wtok_ref, wvs_ref, gkva_ref, wuk_ref, wuv_ref, gmk_ref, gsk_ref,
         wqs_ref, gqa_ref, wuq_ref, gmq_ref, gsq_ref,
         mcos_ref, msin_ref, scos_ref, ssin_ref, kc_ref, ks1_ref, ks2_ref, sc_ref, ss1_ref, ss2_ref,
         km_ref, vmt_ref, ks_ref, vst_ref, qmt_ref, qst_ref) = refs
    else:
        (x_ref, mod_ref, gattn_ref, wtok_ref, wvs_ref, gkva_ref, wuk_ref, wuv_ref, gmk_ref, gsk_ref,
         km_ref, vmt_ref, ks_ref, vst_ref) = refs

    x = x_ref[0]
    y = _rms_rows(x, D_MODEL) * gattn_ref[...]
    h = (y * (1.0 + mod_ref[0, 1:2, :]) + mod_ref[0, 0:1, :]).astype(BF16)

    zt = _dot(h, wtok_ref[...])
    cq = zt[:, 0:256]
    ckv = zt[:, 256:384]
    krp = zt[:, 384:512]
    ksw = zt[:, 512:640]

    ckvn = (_rms_rows(ckv, KV_LORA) * gkva_ref[...]).astype(BF16)
    kpre = _dot(ckvn, wuk_ref[...])
    vt = _dot_nt(wuv_ref[...], ckvn)
    gmk = gmk_ref[...]
    pad_rows = MLA_V_AUG - MLA_V
    ones_row = (lax.broadcasted_iota(jnp.int32, (pad_rows, TOK_TILE), 0) == 0).astype(BF16)
    for hd in range(MLA_HEADS):
        kh = kpre[:, hd * LANE:(hd + 1) * LANE] + krp
        kn = _rms_rows(kh, MLA_QK) * gmk
        if latent:
            kn = (kn * kc_ref[...] + pltpu.roll(kn, 112, 1) * ks1_ref[...]
                  + pltpu.roll(kn, 16, 1) * ks2_ref[...])
        km_ref[0, hd] = kn.astype(BF16)
        vmt_ref[0, hd, 0:MLA_V, :] = vt[hd * MLA_V:(hd + 1) * MLA_V].astype(BF16)
        vmt_ref[0, hd, MLA_V:MLA_V_AUG, :] = ones_row

    lane = lax.broadcasted_iota(jnp.int32, ksw.shape, 1)
    first = lane < SWA_HD
    sq = ksw * ksw
    ss0 = jnp.sum(jnp.where(first, sq, 0.0), axis=-1, keepdims=True)
    ss1 = jnp.sum(jnp.where(first, 0.0, sq), axis=-1, keepdims=True)
    rr = jnp.where(first, lax.rsqrt(ss0 * (1.0 / SWA_HD) + EPS), lax.rsqrt(ss1 * (1.0 / SWA_HD) + EPS))
    ksn = ksw * rr * gsk_ref[...]
    if latent:
        ksn = (ksn * sc_ref[...] + pltpu.roll(ksn, 96, 1) * ss1_ref[...]
               + pltpu.roll(ksn, 32, 1) * ss2_ref[...])
    ks_ref[0] = ksn.astype(BF16)
    vst = _dot_nt(wvs_ref[...], h)
    for g in range(SWA_KV_HEADS):
        for t in range(TOK_TILE // LANE):
            vst_ref[0, g, t, 0:SWA_HD, :] = vst[g * SWA_HD:(g + 1) * SWA_HD, t * LANE:(t + 1) * LANE].astype(BF16)
            vst_ref[0, g, t, SWA_HD:SWA_V_AUG, :] = ones_row[:, 0:LANE]

    if not latent:
        return

    cqn = (_rms_rows(cq, Q_LORA) * gqa_ref[...]).astype(BF16)
    qt = _dot_nt(wuq_ref[...], cqn)
    gmq = gmq_ref[...]
    mcos = mcos_ref[...]
    msin = msin_ref[...]
    for hd in range(MLA_HEADS):
        qn = _rms_cols(qt[hd * LANE:(hd + 1) * LANE], MLA_QK) * (gmq * MLA_QSCALE)
        x1 = qn[64:80]
        x2 = qn[80:96]
        qmt_ref[0, hd, 0:64, :] = qn[0:64].astype(BF16)
        qmt_ref[0, hd, 64:80, :] = (x1 * mcos - x2 * msin).astype(BF16)
        qmt_ref[0, hd, 80:96, :] = (x2 * mcos + x1 * msin).astype(BF16)
        qmt_ref[0, hd, 96:128, :] = jnp.zeros((32, TOK_TILE), BF16)

    qst = _dot_nt(wqs_ref[...], h)
    gsq = gsq_ref[...]
    scos = scos_ref[...]
    ssin = ssin_ref[...]
    for hd in range(SWA_HEADS):
        qn = _rms_cols(qst[hd * SWA_HD:(hd + 1) * SWA_HD], SWA_HD) * (gsq * SWA_QSCALE)
        x1 = qn[0:32]
        x2 = qn[32:64]
        g = hd // SWA_GROUP
        base = g * SWA_HD
        other = (1 - g) * SWA_HD
        qst_ref[0, hd, base:base + 32, :] = (x1 * scos - x2 * ssin).astype(BF16)
        qst_ref[0, hd, base + 32:base + 64, :] = (x2 * scos + x1 * ssin).astype(BF16)
        qst_ref[0, hd, other:other + 64, :] = jnp.zeros((64, TOK_TILE), BF16)


def _full(shape):
    nd = len(shape)
    return pl.BlockSpec(shape, lambda b, j: (0,) * nd)


def _project(x, mod3, mod_row, weights, tables, latent):
    bsz, n, _ = x.shape
    nt = n // TOK_TILE
    if mod_row is None:
        mod_map = lambda b, j: (b, 0, 0)
    else:
        mod_map = lambda b, j: (mod_row, 0, 0)
    common = [weights[k] for k in ("gattn", "wtok", "wvs", "gkva", "wuk", "wuv", "gmk", "gsk")]
    ins = [x, mod3] + common
    in_specs = [
        pl.BlockSpec((1, TOK_TILE, D_MODEL), lambda b, j: (b, j, 0)),
        pl.BlockSpec((1, 6, D_MODEL), mod_map),
    ] + [_full(w.shape) for w in common]
    out_shape = [
        jax.ShapeDtypeStruct((bsz, MLA_HEADS, n, LANE), BF16),
        jax.ShapeDtypeStruct((bsz, MLA_HEADS, MLA_V_AUG, n), BF16),
        jax.ShapeDtypeStruct((bsz, n, LANE), BF16),
        jax.ShapeDtypeStruct((bsz, SWA_KV_HEADS, n // LANE, SWA_V_AUG, LANE), BF16),
    ]
    out_specs = [
        pl.BlockSpec((1, MLA_HEADS, TOK_TILE, LANE), lambda b, j: (b, 0, j, 0)),
        pl.BlockSpec((1, MLA_HEADS, MLA_V_AUG, TOK_TILE), lambda b, j: (b, 0, 0, j)),
        pl.BlockSpec((1, TOK_TILE, LANE), lambda b, j: (b, j, 0)),
        pl.BlockSpec((1, SWA_KV_HEADS, TOK_TILE // LANE, SWA_V_AUG, LANE), lambda b, j: (b, 0, j, 0, 0)),
    ]
    if latent:
        extra = [weights[k] for k in ("wqs", "gqa", "wuq", "gmq", "gsq")]
        ins += extra
        in_specs += [_full(w.shape) for w in extra]
        feat = [tables[k] for k in ("mcos", "msin", "scos", "ssin")]
        ins += feat
        in_specs += [pl.BlockSpec((t.shape[0], TOK_TILE), lambda b, j: (0, j)) for t in feat]
        tok = [tables[k] for k in ("kc", "ks1", "ks2", "sc", "ss1", "ss2")]
        ins += tok
        in_specs += [pl.BlockSpec((TOK_TILE, LANE), lambda b, j: (j, 0)) for _ in tok]
        out_shape += [
            jax.ShapeDtypeStruct((bsz, MLA_HEADS, LANE, n), BF16),
            jax.ShapeDtypeStruct((bsz, SWA_HEADS, LANE, n), BF16),
        ]
        out_specs += [
            pl.BlockSpec((1, MLA_HEADS, LANE, TOK_TILE), lambda b, j: (b, 0, 0, j)),
            pl.BlockSpec((1, SWA_HEADS, LANE, TOK_TILE), lambda b, j: (b, 0, 0, j)),
        ]
    return pl.pallas_call(
        functools.partial(_proj_kernel, latent=latent),
        grid=(bsz, nt),
        in_specs=in_specs,
        out_specs=out_specs,
        out_shape=out_shape,
        compiler_params=pltpu.CompilerParams(vmem_limit_bytes=VMEM_LIMIT),
        name="project_latent" if latent else "project_context",
    )(*ins)


def _mla_kernel(qt_ref, kc_ref, kl_ref, vct_ref, vlt_ref, o_ref, ot_scr, s_scr, p_scr, *, seq):
    n_ctx = kc_ref.shape[2]
    tq = qt_ref.shape[3]
    subs = ([(kc_ref, vct_ref, o) for o in range(0, n_ctx, MLA_SUB)]
            + [(kl_ref, vlt_ref, o) for o in range(0, seq, MLA_SUB)])
    n_sub = len(subs)

    def scores_into(slot, j, hd, m8):
        k_ref, _, off = subs[j]
        sj = _dot(k_ref[0, hd, off:off + MLA_SUB, :], qt_ref[0, hd])
        s_scr[slot, j * MLA_SUB:(j + 1) * MLA_SUB, :] = sj
        mj = jnp.max(sj.reshape(MLA_SUB // 8, 8, tq), axis=0)
        return mj if m8 is None else jnp.maximum(m8, mj)

    def head_body(hd, m8_cur, slot, has_next):
        mb = jnp.broadcast_to(jnp.max(m8_cur, axis=0, keepdims=True), (MLA_CHUNK, tq))
        n_q = n_sub if has_next else 0
        m8_next = None
        acc = None
        for j in range(min(MLA_QK_LEAD, n_q)):
            m8_next = scores_into(1 - slot, j, hd + 1, m8_next)
        for j in range(n_sub):
            base = j * MLA_SUB
            for r in range(base, base + MLA_SUB, MLA_CHUNK):
                x = s_scr[slot, r:r + MLA_CHUNK, :] - mb
                p_scr[r:r + MLA_CHUNK, :] = jnp.exp2(x).astype(BF16)
            _, v_ref, off = subs[j]
            d = _dot(v_ref[0, hd, :, off:off + MLA_SUB], p_scr[base:base + MLA_SUB, :])
            acc = d if acc is None else acc + d
            if j + MLA_QK_LEAD < n_q:
                m8_next = scores_into(1 - slot, j + MLA_QK_LEAD, hd + 1, m8_next)
        row = pl.multiple_of(hd * MLA_V, MLA_V)
        ot_scr[pl.ds(row, MLA_V), :] = acc[0:MLA_V] / acc[MLA_V:MLA_V + 1]
        return m8_next

    def head_step(hd, m8):
        return lax.cond(lax.rem(hd, 2) == 0,
                        lambda m: head_body(hd, m, 0, True),
                        lambda m: head_body(hd, m, 1, True), m8)

    m8_first = None
    for j in range(n_sub):
        m8_first = scores_into(0, j, 0, m8_first)
    m8_last = lax.fori_loop(0, MLA_HEADS - 1, head_step, m8_first)
    head_body(MLA_HEADS - 1, m8_last, (MLA_HEADS - 1) % 2, False)
    o_ref[0] = ot_scr[...].T.astype(BF16)


def _mla_attention(qmt, km_c, km_l, vmt_c, vmt_l):
    bsz, _, _, s = qmt.shape
    n_ctx = km_c.shape[2]
    nt = s // TOK_TILE
    n_keys = n_ctx + s
    return pl.pallas_call(
        functools.partial(_mla_kernel, seq=s),
        grid=(bsz, nt),
        in_specs=[
            pl.BlockSpec((1, MLA_HEADS, LANE, TOK_TILE), lambda b, j: (b, 0, 0, j)),
            pl.BlockSpec((1, MLA_HEADS, n_ctx, LANE), lambda b, j: (b, 0, 0, 0)),
            pl.BlockSpec((1, MLA_HEADS, s, LANE), lambda b, j: (b, 0, 0, 0)),
            pl.BlockSpec((1, MLA_HEADS, MLA_V_AUG, n_ctx), lambda b, j: (b, 0, 0, 0)),
            pl.BlockSpec((1, MLA_HEADS, MLA_V_AUG, s), lambda b, j: (b, 0, 0, 0)),
        ],
        out_specs=pl.BlockSpec((1, TOK_TILE, MLA_HEADS * MLA_V), lambda b, j: (b, j, 0)),
        out_shape=jax.ShapeDtypeStruct((bsz, s, MLA_HEADS * MLA_V), BF16),
        scratch_shapes=[
            pltpu.VMEM((MLA_HEADS * MLA_V, TOK_TILE), F32),
            pltpu.VMEM((2, n_keys, TOK_TILE), F32),
            pltpu.VMEM((n_keys, TOK_TILE), BF16),
        ],
        compiler_params=pltpu.CompilerParams(vmem_limit_bytes=VMEM_LIMIT),
        name="mla_attention",
    )(qmt, km_c, km_l, vmt_c, vmt_l)


def _swa_kernel(sink_ref, bias_ref, qt_ref, kc_ref, kl_ref, vct_ref, vlt_ref, o_ref,
                ot_scr, k_scr, v_scr, s_scr, p_scr, *, seq):
    j = pl.program_id(1)
    n_ctx = kc_ref.shape[1]
    n_keys = SWA_WIN + n_ctx
    n_sub = n_keys // SWA_SUB
    tq = qt_ref.shape[3]

    start = pl.multiple_of(jnp.clip(j * TOK_TILE - WINDOW, 0, seq - SWA_WIN), LANE)
    t0 = start // LANE
    k_scr[0:SWA_WIN, :] = kl_ref[0, pl.ds(start, SWA_WIN), :]
    k_scr[SWA_WIN:n_keys, :] = kc_ref[0]
    for g in range(SWA_KV_HEADS):
        for i in range(SWA_WIN // LANE):
            v_scr[g, :, i * LANE:(i + 1) * LANE] = vlt_ref[0, g, t0 + i]
        for i in range(n_ctx // LANE):
            v_scr[g, :, SWA_WIN + i * LANE:SWA_WIN + (i + 1) * LANE] = vct_ref[0, g, i]

    def scores_into(slot, t, hd, m8):
        rows = slice(t * SWA_SUB, (t + 1) * SWA_SUB)
        sj = _dot(k_scr[rows, :], qt_ref[0, hd])
        if t * SWA_SUB < SWA_WIN:
            sj = sj + bias_ref[0, rows, :]
        s_scr[slot, rows, :] = sj
        mj = jnp.max(sj.reshape(SWA_SUB // 8, 8, tq), axis=0)
        return mj if m8 is None else jnp.maximum(m8, mj)

    def head_body(hd, m8_cur, slot, has_next):
        sk = sink_ref[hd] * LOG2E
        m = jnp.maximum(jnp.max(m8_cur, axis=0, keepdims=True), sk)
        mb = jnp.broadcast_to(m, (SWA_CHUNK, tq))
        g = hd // SWA_GROUP
        n_q = n_sub if has_next else 0
        m8_next = None
        acc = None
        for t in range(min(SWA_QK_LEAD, n_q)):
            m8_next = scores_into(1 - slot, t, hd + 1, m8_next)
        for t in range(n_sub):
            base = t * SWA_SUB
            for r in range(base, base + SWA_SUB, SWA_CHUNK):
                x = s_scr[slot, r:r + SWA_CHUNK, :] - mb
                p_scr[r:r + SWA_CHUNK, :] = jnp.exp2(x).astype(BF16)
            d = _dot(v_scr[g, :, base:base + SWA_SUB], p_scr[base:base + SWA_SUB, :])
            acc = d if acc is None else acc + d
            if t + SWA_QK_LEAD < n_q:
                m8_next = scores_into(1 - slot, t + SWA_QK_LEAD, hd + 1, m8_next)
        denom = acc[SWA_HD:SWA_HD + 1] + jnp.exp2(sk - m)
        row = pl.multiple_of(hd * SWA_HD, SWA_HD)
        ot_scr[pl.ds(row, SWA_HD), :] = acc[0:SWA_HD] / denom
        return m8_next

    def head_step(hd, m8):
        return lax.cond(lax.rem(hd, 2) == 0,
                        lambda mm: head_body(hd, mm, 0, True),
                        lambda mm: head_body(hd, mm, 1, True), m8)

    m8_first = None
    for t in range(n_sub):
        m8_first = scores_into(0, t, 0, m8_first)
    m8_last = lax.fori_loop(0, SWA_HEADS - 1, head_step, m8_first)
    head_body(SWA_HEADS - 1, m8_last, (SWA_HEADS - 1) % 2, False)
    o_ref[0] = ot_scr[...].T.astype(BF16)


def _swa_bias(seq):
    import numpy as np
    nt = seq // TOK_TILE
    out = []
    for j in (0, 1, nt - 1):
        start = min(max(j * TOK_TILE - WINDOW, 0), seq - SWA_WIN)
        qpos = j * TOK_TILE + np.arange(TOK_TILE)[None, :]
        kpos = start + np.arange(SWA_WIN)[:, None]
        out.append(np.where(np.abs(qpos - kpos) <= WINDOW, 0.0, NEG_INF))
    return jnp.asarray(np.stack(out), F32)


def _swa_attention(sink, qst, ks_c, ks_l, vst_c, vst_l):
    bsz, _, _, s = qst.shape
    n_ctx = ks_c.shape[1]
    nt = s // TOK_TILE
    assert nt >= 3 and s >= SWA_WIN and n_ctx % SWA_SUB == 0
    n_keys = SWA_WIN + n_ctx
    bias_idx = lambda b, j: (jnp.where(j == 0, 0, jnp.where(j == nt - 1, 2, 1)), 0, 0)
    return pl.pallas_call(
        functools.partial(_swa_kernel, seq=s),
        grid=(bsz, nt),
        in_specs=[
            pl.BlockSpec(memory_space=pltpu.SMEM),
            pl.BlockSpec((1, SWA_WIN, TOK_TILE), bias_idx),
            pl.BlockSpec((1, SWA_HEADS, LANE, TOK_TILE), lambda b, j: (b, 0, 0, j)),
            pl.BlockSpec((1, n_ctx, LANE), lambda b, j: (b, 0, 0)),
            pl.BlockSpec((1, s, LANE), lambda b, j: (b, 0, 0)),
            pl.BlockSpec((1, SWA_KV_HEADS, n_ctx // LANE, SWA_V_AUG, LANE), lambda b, j: (b, 0, 0, 0, 0)),
            pl.BlockSpec((1, SWA_KV_HEADS, s // LANE, SWA_V_AUG, LANE), lambda b, j: (b, 0, 0, 0, 0)),
        ],
        out_specs=pl.BlockSpec((1, TOK_TILE, SWA_HEADS * SWA_HD), lambda b, j: (b, j, 0)),
        out_shape=jax.ShapeDtypeStruct((bsz, s, SWA_HEADS * SWA_HD), BF16),
        scratch_shapes=[
            pltpu.VMEM((SWA_HEADS * SWA_HD, TOK_TILE), F32),
            pltpu.VMEM((n_keys, LANE), BF16),
            pltpu.VMEM((SWA_KV_HEADS, SWA_V_AUG, n_keys), BF16),
            pltpu.VMEM((2, n_keys, TOK_TILE), F32),
            pltpu.VMEM((n_keys, TOK_TILE), BF16),
        ],
        compiler_params=pltpu.CompilerParams(vmem_limit_bytes=VMEM_LIMIT),
        name="swa_attention",
    )(sink, _swa_bias(s), qst, ks_c, ks_l, vst_c, vst_l)


def _mlp_kernel(x_ref, mm_ref, ms_ref, mod_ref, wo_ref, gmlp_ref, w1_ref, w2_ref, o_ref, acc_ref):
    half = MLA_HEADS * MLA_V
    y1 = _dot(mm_ref[0], wo_ref[0:half, :]) + _dot(ms_ref[0], wo_ref[half:, :])
    x1 = x_ref[0] + mod_ref[0, 2:3, :] * y1
    h2 = (_rms_rows(x1, D_MODEL) * gmlp_ref[...] * (1.0 + mod_ref[0, 4:5, :])
          + mod_ref[0, 3:4, :]).astype(BF16)
    for c in range(D_FF // FF_CHUNK):
        a = jnp.maximum(_dot(h2, w1_ref[:, c * FF_CHUNK:(c + 1) * FF_CHUNK]), 0.0)
        part = _dot((a * a).astype(BF16), w2_ref[c * FF_CHUNK:(c + 1) * FF_CHUNK, :])
        if c == 0:
            acc_ref[...] = part
        else:
            acc_ref[...] += part
    o_ref[0] = x1 + mod_ref[0, 5:6, :] * acc_ref[...]


def _outproj_mlp(x, mix_m, mix_s, mod3, w_out, g_mlp, w1, w2):
    bsz, s, _ = x.shape
    half = MLA_HEADS * MLA_V
    const = lambda b, j: (0, 0)
    single = pl.Buffered(1)
    return pl.pallas_call(
        _mlp_kernel,
        grid=(bsz, s // MLP_TILE),
        in_specs=[
            pl.BlockSpec((1, MLP_TILE, D_MODEL), lambda b, j: (b, j, 0)),
            pl.BlockSpec((1, MLP_TILE, half), lambda b, j: (b, j, 0)),
            pl.BlockSpec((1, MLP_TILE, half), lambda b, j: (b, j, 0)),
            pl.BlockSpec((1, 6, D_MODEL), lambda b, j: (b, 0, 0)),
            pl.BlockSpec((D_MODEL, D_MODEL), const, pipeline_mode=single),
            pl.BlockSpec((1, D_MODEL), const),
            pl.BlockSpec((D_MODEL, D_FF), const, pipeline_mode=single),
            pl.BlockSpec((D_FF, D_MODEL), const, pipeline_mode=single),
        ],
        out_specs=pl.BlockSpec((1, MLP_TILE, D_MODEL), lambda b, j: (b, j, 0)),
        out_shape=jax.ShapeDtypeStruct((bsz, s, D_MODEL), F32),
        scratch_shapes=[pltpu.VMEM((MLP_TILE, D_MODEL), F32)],
        compiler_params=pltpu.CompilerParams(vmem_limit_bytes=VMEM_LIMIT),
        name="outproj_mlp",
    )(x, mix_m, mix_s, mod3, w_out, g_mlp, w1, w2)


def _rope_tables(seq):
    def cos_sin(rot_dim):
        n_freq = rot_dim // 4
        inv = ROPE_THETA ** (-jnp.arange(n_freq, dtype=F32) / n_freq)
        rows = seq // GRID_W
        row = jnp.repeat(jnp.arange(rows, dtype=F32), GRID_W)
        col = jnp.tile(jnp.arange(GRID_W, dtype=F32), rows)
        ang = jnp.concatenate([row[:, None] * inv, col[:, None] * inv], axis=-1)
        return jnp.cos(ang), jnp.sin(ang)

    mcos, msin = cos_sin(MLA_ROPE)
    scos, ssin = cos_sin(SWA_HD)
    z16 = jnp.zeros((seq, 16), F32)
    z32 = jnp.zeros((seq, 32), F32)
    one64 = jnp.ones((seq, 64), F32)
    kc = jnp.concatenate([one64, mcos, mcos, jnp.ones((seq, 32), F32)], axis=1)
    ks1 = jnp.concatenate([jnp.zeros((seq, 64), F32), -msin, z16, z32], axis=1)
    ks2 = jnp.concatenate([jnp.zeros((seq, 64), F32), z16, msin, z32], axis=1)
    sc = jnp.concatenate([scos, scos, scos, scos], axis=1)
    ss1 = jnp.concatenate([-ssin, z32, -ssin, z32], axis=1)
    ss2 = jnp.concatenate([z32, ssin, z32, ssin], axis=1)
    return dict(mcos=mcos.T, msin=msin.T, scos=scos.T, ssin=ssin.T,
                kc=kc, ks1=ks1, ks2=ks2, sc=sc, ss1=ss1, ss2=ss2)


def _prep_weights(g_attn, w_in, g_q_a, w_uq, g_kv_a, w_ukv, g_mla_q, g_mla_k, g_swa_q, g_swa_k):
    o_ckv = Q_LORA
    o_kr = o_ckv + KV_LORA
    o_qs = o_kr + MLA_ROPE
    o_ks = o_qs + SWA_HEADS * SWA_HD
    o_vs = o_ks + SWA_KV_HEADS * SWA_HD
    w_cq = w_in[:, :o_ckv]
    w_ckv = w_in[:, o_ckv:o_kr]
    w_kr = w_in[:, o_kr:o_qs]
    w_qs = w_in[:, o_qs:o_ks]
    w_ks = w_in[:, o_ks:o_vs]
    w_vs = w_in[:, o_vs:]
    w_kr_p = jnp.pad(w_kr, ((0, 0), (MLA_NOPE, LANE - MLA_QK)))
    wtok = jnp.concatenate([w_cq, w_ckv, w_kr_p, w_ks], axis=1).astype(BF16)
    w_ukv_h = w_ukv.reshape(KV_LORA, MLA_HEADS, MLA_NOPE + MLA_V)
    wuk = jnp.pad(w_ukv_h[:, :, :MLA_NOPE], ((0, 0), (0, 0), (0, LANE - MLA_NOPE)))
    wuk = wuk.reshape(KV_LORA, MLA_HEADS * LANE).astype(BF16)
    wuv = w_ukv_h[:, :, MLA_NOPE:].reshape(KV_LORA, MLA_HEADS * MLA_V).T.astype(BF16)
    w_uq_h = jnp.pad(w_uq.reshape(Q_LORA, MLA_HEADS, MLA_QK), ((0, 0), (0, 0), (0, LANE - MLA_QK)))
    wuq = w_uq_h.reshape(Q_LORA, MLA_HEADS * LANE).T.astype(BF16)
    return dict(
        gattn=g_attn[None, :], wtok=wtok, wvs=w_vs.T.astype(BF16), gkva=g_kv_a[None, :],
        wuk=wuk, wuv=wuv, gmk=jnp.pad(g_mla_k, (0, LANE - MLA_QK))[None, :],
        gsk=jnp.tile(g_swa_k, SWA_KV_HEADS)[None, :],
        wqs=w_qs.T.astype(BF16), gqa=g_q_a[None, :], wuq=wuq,
        gmq=jnp.pad(g_mla_q, (0, LANE - MLA_QK))[:, None], gsq=g_swa_q[:, None],
    )


def kernel(x, c, ctx, c_ctx, w_mod, b_mod, g_attn, w_in, g_q_a, w_uq, g_kv_a, w_ukv, g_mla_q, g_mla_k,
           g_swa_q, g_swa_k, swa_sink, w_out, g_mlp, w_mlp1, w_mlp2):
    bsz, seq, _ = x.shape
    assert w_mod.shape[0] == 1, "single-layer block"
    assert bsz < 8 and seq % MLP_TILE == 0 and ctx.shape[1] % TOK_TILE == 0

    cond = jnp.zeros((8, D_MODEL), F32).at[:bsz].set(c).at[bsz].set(c_ctx)
    mod = _modulation(cond, w_mod[0], b_mod[0][None, :])
    mod3 = mod.reshape(8, 6, D_MODEL)

    weights = _prep_weights(g_attn[0], w_in[0], g_q_a[0], w_uq[0], g_kv_a[0], w_ukv[0],
                            g_mla_q[0], g_mla_k[0], g_swa_q[0], g_swa_k[0])
    tables = _rope_tables(seq)

    km_l, vmt_l, ks_l, vst_l, qmt, qst = _project(x, mod3, None, weights, tables, latent=True)
    km_c, vmt_c, ks_c, vst_c = _project(ctx, mod3, bsz, weights, None, latent=False)

    mix_m = _mla_attention(qmt, km_c, km_l, vmt_c, vmt_l)
    mix_s = _swa_attention(swa_sink[0], qst, ks_c, ks_l, vst_c, vst_l)

    return _outproj_mlp(x, mix_m, mix_s, mod3, w_out[0].astype(BF16), g_mlp[0][None, :],
                        w_mlp1[0].astype(BF16), w_mlp2[0].astype(BF16))
```

```python
import functools
import math

import jax
import jax.numpy as jnp
import numpy as np
from jax import lax
from jax.experimental import pallas as pl
from jax.experimental.pallas import tpu as pltpu

D_MODEL = 1024
GRID_W = 64
MLA_HEADS = 8
MLA_NOPE = 64
MLA_ROPE = 32
MLA_QK = MLA_NOPE + MLA_ROPE
MLA_V = 64
MLA_V_AUG = MLA_V + 16
Q_LORA = 256
KV_LORA = 128
SWA_HEADS = 8
SWA_KV_HEADS = 2
SWA_GROUP = SWA_HEADS // SWA_KV_HEADS
SWA_HD = 64
SWA_V_AUG = SWA_HD + 16
WINDOW = 128
D_FF = 4 * D_MODEL
ROPE_THETA = 10000.0
EPS = 1e-6
NEG_INF = -1e30
LOG2E = 1.4426950408889634
MLA_QSCALE = LOG2E / math.sqrt(MLA_QK)
SWA_QSCALE = LOG2E / math.sqrt(SWA_HD)

LANE = 128
TOK_TILE = 256
PROJ_TILE = 512
MLP_TILE = 512
FF_CHUNK = 1024
SWA_WIN = 512
MLA_SUB = 256
SWA_SUB = 256
SWA_CHUNK = 32
SWA_QK_LEAD = 1
MLA_CHUNK = 32
MLA_QK_LEAD = 3
VMEM_LIMIT = 56 * 1024 * 1024

F32 = jnp.float32
BF16 = jnp.bfloat16
NT_DIMS = (((1,), (1,)), ((), ()))


def _dot(a, b):
    return jnp.dot(a, b, preferred_element_type=F32)


def _dot_nt(a, b):
    return lax.dot_general(a, b, NT_DIMS, preferred_element_type=F32)


def _mod_kernel(cond_ref, w_ref, b_ref, o_ref):
    cnd = cond_ref[...]
    act = cnd * jax.nn.sigmoid(cnd)
    o_ref[...] = _dot(act.astype(BF16), w_ref[...].astype(BF16)) + b_ref[...]


def _modulation(cond, w_mod, b_mod):
    n = w_mod.shape[1]
    tn = 1024
    return pl.pallas_call(
        _mod_kernel,
        grid=(n // tn,),
        in_specs=[
            pl.BlockSpec((8, D_MODEL), lambda i: (0, 0)),
            pl.BlockSpec((D_MODEL, tn), lambda i: (0, i)),
            pl.BlockSpec((1, tn), lambda i: (0, i)),
        ],
        out_specs=pl.BlockSpec((8, tn), lambda i: (0, i)),
        out_shape=jax.ShapeDtypeStruct((8, n), F32),
        name="modulation",
    )(cond, w_mod, b_mod)


def _rms_rows(x, n):
    return x * lax.rsqrt(jnp.sum(x * x, axis=-1, keepdims=True) * (1.0 / n) + EPS)


def _rms_cols(x, n):
    return x * lax.rsqrt(jnp.sum(x * x, axis=0, keepdims=True) * (1.0 / n) + EPS)


def _proj_kernel(*refs, latent):
    if latent:
        (x_ref, mod_ref, gattn_ref, wtok_ref, wvs_ref, gkva_ref, wuk_ref, wuv_ref, gmk_ref, gsk_ref,
         wqs_ref, gqa_ref, wuq_ref, gmq_ref, gsq_ref,
         mcos_ref, msin_ref, scos_ref, ssin_ref, kc_ref, ks1_ref, ks2_ref, sc_ref, ss1_ref, ss2_ref,
         km_ref, vmt_ref, ks_ref, vst_ref, qmt_ref, qst_ref) = refs
    else:
        (x_ref, mod_ref, gattn_ref, wtok_ref, wvs_ref, gkva_ref, wuk_ref, wuv_ref, gmk_ref, gsk_ref,
         km_ref, vmt_ref, ks_ref, vst_ref) = refs

    tile = x_ref.shape[1]

    x = x_ref[0]
    gain = gattn_ref[...] * (1.0 + mod_ref[0, 1:2, :])
    h = (_rms_rows(x, D_MODEL) * gain + mod_ref[0, 0:1, :]).astype(BF16)

    zt = _dot(h, wtok_ref[...])
    cq = zt[:, 0:256]
    ckv = zt[:, 256:384]
    krp = zt[:, 384:512]
    ksw = zt[:, 512:640]

    ckvn = (_rms_rows(ckv, KV_LORA) * gkva_ref[...]).astype(BF16)
    kpre = _dot(ckvn, wuk_ref[...])
    vt = _dot_nt(wuv_ref[...], ckvn)
    gmk = gmk_ref[...]
    kr = krp * gmk
    if latent:
        kr = kr * kc_ref[...] + pltpu.roll(kr, 112, 1) * ks1_ref[...] + pltpu.roll(kr, 16, 1) * ks2_ref[...]
    ss_kr = jnp.sum(krp * krp, axis=-1, keepdims=True)
    pad_rows = MLA_V_AUG - MLA_V
    ones_row = (lax.broadcasted_iota(jnp.int32, (pad_rows, tile), 0) == 0).astype(BF16)
    for hd in range(MLA_HEADS):
        kp = kpre[:, hd * LANE:(hd + 1) * LANE]
        ss = jnp.sum(kp * kp, axis=-1, keepdims=True) + ss_kr
        kn = (kp * gmk + kr) * lax.rsqrt(ss * (1.0 / MLA_QK) + EPS)
        km_ref[0, hd] = kn.astype(BF16)
        vmt_ref[0, hd, 0:MLA_V, :] = vt[hd * MLA_V:(hd + 1) * MLA_V].astype(BF16)
        vmt_ref[0, hd, MLA_V:MLA_V_AUG, :] = ones_row

    lane = lax.broadcasted_iota(jnp.int32, ksw.shape, 1)
    first = lane < SWA_HD
    sq = ksw * ksw
    ss0 = jnp.sum(jnp.where(first, sq, 0.0), axis=-1, keepdims=True)
    ss1 = jnp.sum(jnp.where(first, 0.0, sq), axis=-1, keepdims=True)
    rr = jnp.where(first, lax.rsqrt(ss0 * (1.0 / SWA_HD) + EPS), lax.rsqrt(ss1 * (1.0 / SWA_HD) + EPS))
    ksn = ksw * rr * gsk_ref[...]
    if latent:
        ksn = (ksn * sc_ref[...] + pltpu.roll(ksn, 96, 1) * ss1_ref[...]
               + pltpu.roll(ksn, 32, 1) * ss2_ref[...])
    ks_ref[0] = ksn.astype(BF16)
    vst = _dot_nt(wvs_ref[...], h)
    for g in range(SWA_KV_HEADS):
        for t in range(tile // LANE):
            vst_ref[0, g, t, 0:SWA_HD, :] = vst[g * SWA_HD:(g + 1) * SWA_HD, t * LANE:(t + 1) * LANE].astype(BF16)
            vst_ref[0, g, t, SWA_HD:SWA_V_AUG, :] = ones_row[:, 0:LANE]

    if not latent:
        return

    cqn = (_rms_rows(cq, Q_LORA) * gqa_ref[...]).astype(BF16)
    qt = _dot_nt(wuq_ref[...], cqn)
    gmq = gmq_ref[...] * MLA_QSCALE
    mcos = mcos_ref[...]
    msin = msin_ref[...]
    for hd in range(MLA_HEADS):
        qn = _rms_cols(qt[hd * LANE:(hd + 1) * LANE], MLA_QK) * gmq
        x1 = qn[64:80]
        x2 = qn[80:96]
        qmt_ref[0, hd, 0:64, :] = qn[0:64].astype(BF16)
        qmt_ref[0, hd, 64:80, :] = (x1 * mcos - x2 * msin).astype(BF16)
        qmt_ref[0, hd, 80:96, :] = (x2 * mcos + x1 * msin).astype(BF16)
        qmt_ref[0, hd, 96:128, :] = jnp.zeros((32, tile), BF16)

    qst = _dot_nt(wqs_ref[...], h)
    gsq = gsq_ref[...] * SWA_QSCALE
    scos = scos_ref[...]
    ssin = ssin_ref[...]
    for hd in range(SWA_HEADS):
        qn = _rms_cols(qst[hd * SWA_HD:(hd + 1) * SWA_HD], SWA_HD) * gsq
        x1 = qn[0:32]
        x2 = qn[32:64]
        g = hd // SWA_GROUP
        base = g * SWA_HD
        other = (1 - g) * SWA_HD
        qst_ref[0, hd, base:base + 32, :] = (x1 * scos - x2 * ssin).astype(BF16)
        qst_ref[0, hd, base + 32:base + 64, :] = (x2 * scos + x1 * ssin).astype(BF16)
        qst_ref[0, hd, other:other + 64, :] = jnp.zeros((64, tile), BF16)


def _full(shape):
    nd = len(shape)
    return pl.BlockSpec(shape, lambda b, j: (0,) * nd)


def _project(x, mod3, mod_row, weights, tables, latent, tile):
    bsz, n, _ = x.shape
    nt = n // tile
    if mod_row is None:
        mod_map = lambda b, j: (b, 0, 0)
    else:
        mod_map = lambda b, j: (mod_row, 0, 0)
    common = [weights[k] for k in ("gattn", "wtok", "wvs", "gkva", "wuk", "wuv", "gmk", "gsk")]
    ins = [x, mod3] + common
    in_specs = [
        pl.BlockSpec((1, tile, D_MODEL), lambda b, j: (b, j, 0)),
        pl.BlockSpec((1, 6, D_MODEL), mod_map),
    ] + [_full(w.shape) for w in common]
    out_shape = [
        jax.ShapeDtypeStruct((bsz, MLA_HEADS, n, LANE), BF16),
        jax.ShapeDtypeStruct((bsz, MLA_HEADS, MLA_V_AUG, n), BF16),
        jax.ShapeDtypeStruct((bsz, n, LANE), BF16),
        jax.ShapeDtypeStruct((bsz, SWA_KV_HEADS, n // LANE, SWA_V_AUG, LANE), BF16),
    ]
    out_specs = [
        pl.BlockSpec((1, MLA_HEADS, tile, LANE), lambda b, j: (b, 0, j, 0)),
        pl.BlockSpec((1, MLA_HEADS, MLA_V_AUG, tile), lambda b, j: (b, 0, 0, j)),
        pl.BlockSpec((1, tile, LANE), lambda b, j: (b, j, 0)),
        pl.BlockSpec((1, SWA_KV_HEADS, tile // LANE, SWA_V_AUG, LANE), lambda b, j: (b, 0, j, 0, 0)),
    ]
    if latent:
        extra = [weights[k] for k in ("wqs", "gqa", "wuq", "gmq", "gsq")]
        ins += extra
        in_specs += [_full(w.shape) for w in extra]
        feat = [tables[k] for k in ("mcos", "msin", "scos", "ssin")]
        ins += feat
        in_specs += [pl.BlockSpec((t.shape[0], tile), lambda b, j: (0, j)) for t in feat]
        tok = [tables[k] for k in ("kc", "ks1", "ks2", "sc", "ss1", "ss2")]
        ins += tok
        in_specs += [pl.BlockSpec((tile, LANE), lambda b, j: (j, 0)) for _ in tok]
        out_shape += [
            jax.ShapeDtypeStruct((bsz, MLA_HEADS, LANE, n), BF16),
            jax.ShapeDtypeStruct((bsz, SWA_HEADS, LANE, n), BF16),
        ]
        out_specs += [
            pl.BlockSpec((1, MLA_HEADS, LANE, tile), lambda b, j: (b, 0, 0, j)),
            pl.BlockSpec((1, SWA_HEADS, LANE, tile), lambda b, j: (b, 0, 0, j)),
        ]
    return pl.pallas_call(
        functools.partial(_proj_kernel, latent=latent),
        grid=(bsz, nt),
        in_specs=in_specs,
        out_specs=out_specs,
        out_shape=out_shape,
        compiler_params=pltpu.CompilerParams(vmem_limit_bytes=VMEM_LIMIT),
        name="project_latent" if latent else "project_context",
    )(*ins)


def _mla_kernel(qt_ref, kc_ref, kl_ref, vct_ref, vlt_ref, o_ref, ot_scr, s_scr, p_scr, *, seq):
    n_ctx = kc_ref.shape[2]
    tq = qt_ref.shape[3]
    subs = ([(kc_ref, vct_ref, o) for o in range(0, n_ctx, MLA_SUB)]
            + [(kl_ref, vlt_ref, o) for o in range(0, seq, MLA_SUB)])
    n_sub = len(subs)

    def scores_into(slot, j, hd, m8):
        k_ref, _, off = subs[j]
        sj = _dot(k_ref[0, hd, off:off + MLA_SUB, :], qt_ref[0, hd])
        s_scr[slot, j * MLA_SUB:(j + 1) * MLA_SUB, :] = sj
        mj = jnp.max(sj.reshape(MLA_SUB // 8, 8, tq), axis=0)
        return mj if m8 is None else jnp.maximum(m8, mj)

    def head_body(hd, m8_cur, slot, has_next):
        mb = jnp.broadcast_to(jnp.max(m8_cur, axis=0, keepdims=True), (MLA_CHUNK, tq))
        n_q = n_sub if has_next else 0
        m8_next = None
        acc = None
        for j in range(min(MLA_QK_LEAD, n_q)):
            m8_next = scores_into(1 - slot, j, hd + 1, m8_next)
        for j in range(n_sub):
            base = j * MLA_SUB
            for r in range(base, base + MLA_SUB, MLA_CHUNK):
                x = s_scr[slot, r:r + MLA_CHUNK, :] - mb
                p_scr[r:r + MLA_CHUNK, :] = jnp.exp2(x).astype(BF16)
            _, v_ref, off = subs[j]
            d = _dot(v_ref[0, hd, :, off:off + MLA_SUB], p_scr[base:base + MLA_SUB, :])
            acc = d if acc is None else acc + d
            if j + MLA_QK_LEAD < n_q:
                m8_next = scores_into(1 - slot, j + MLA_QK_LEAD, hd + 1, m8_next)
        row = pl.multiple_of(hd * MLA_V, MLA_V)
        ot_scr[pl.ds(row, MLA_V), :] = acc[0:MLA_V] / acc[MLA_V:MLA_V + 1]
        return m8_next

    def head_step(hd, m8):
        return lax.cond(lax.rem(hd, 2) == 0,
                        lambda m: head_body(hd, m, 0, True),
                        lambda m: head_body(hd, m, 1, True), m8)

    m8_first = None
    for j in range(n_sub):
        m8_first = scores_into(0, j, 0, m8_first)
    m8_last = lax.fori_loop(0, MLA_HEADS - 1, head_step, m8_first)
    head_body(MLA_HEADS - 1, m8_last, (MLA_HEADS - 1) % 2, False)
    o_ref[0] = ot_scr[...].T.astype(BF16)


def _mla_attention(qmt, km_c, km_l, vmt_c, vmt_l):
    bsz, _, _, s = qmt.shape
    n_ctx = km_c.shape[2]
    nt = s // TOK_TILE
    n_keys = n_ctx + s
    return pl.pallas_call(
        functools.partial(_mla_kernel, seq=s),
        grid=(bsz, nt),
        in_specs=[
            pl.BlockSpec((1, MLA_HEADS, LANE, TOK_TILE), lambda b, j: (b, 0, 0, j)),
            pl.BlockSpec((1, MLA_HEADS, n_ctx, LANE), lambda b, j: (b, 0, 0, 0)),
            pl.BlockSpec((1, MLA_HEADS, s, LANE), lambda b, j: (b, 0, 0, 0)),
            pl.BlockSpec((1, MLA_HEADS, MLA_V_AUG, n_ctx), lambda b, j: (b, 0, 0, 0)),
            pl.BlockSpec((1, MLA_HEADS, MLA_V_AUG, s), lambda b, j: (b, 0, 0, 0)),
        ],
        out_specs=pl.BlockSpec((1, TOK_TILE, MLA_HEADS * MLA_V), lambda b, j: (b, j, 0)),
        out_shape=jax.ShapeDtypeStruct((bsz, s, MLA_HEADS * MLA_V), BF16),
        scratch_shapes=[
            pltpu.VMEM((MLA_HEADS * MLA_V, TOK_TILE), F32),
            pltpu.VMEM((2, n_keys, TOK_TILE), F32),
            pltpu.VMEM((n_keys, TOK_TILE), BF16),
        ],
        compiler_params=pltpu.CompilerParams(vmem_limit_bytes=VMEM_LIMIT),
        name="mla_attention",
    )(qmt, km_c, km_l, vmt_c, vmt_l)


def _swa_kernel(sink_ref, bias_ref, qt_ref, kc_ref, kl_ref, vct_ref, vlt_ref, o_ref,
                ot_scr, k_scr, v_scr, s_scr, p_scr, *, seq):
    j = pl.program_id(1)
    n_ctx = kc_ref.shape[1]
    n_keys = SWA_WIN + n_ctx
    n_sub = n_keys // SWA_SUB
    tq = qt_ref.shape[3]

    start = pl.multiple_of(jnp.clip(j * TOK_TILE - WINDOW, 0, seq - SWA_WIN), LANE)
    t0 = start // LANE
    k_scr[0:SWA_WIN, :] = kl_ref[0, pl.ds(start, SWA_WIN), :]
    k_scr[SWA_WIN:n_keys, :] = kc_ref[0]
    for g in range(SWA_KV_HEADS):
        for i in range(SWA_WIN // LANE):
            v_scr[g, :, i * LANE:(i + 1) * LANE] = vlt_ref[0, g, t0 + i]
        for i in range(n_ctx // LANE):
            v_scr[g, :, SWA_WIN + i * LANE:SWA_WIN + (i + 1) * LANE] = vct_ref[0, g, i]

    def scores_into(slot, t, hd, m8):
        rows = slice(t * SWA_SUB, (t + 1) * SWA_SUB)
        sj = _dot(k_scr[rows, :], qt_ref[0, hd])
        if t * SWA_SUB < SWA_WIN:
            sj = sj + bias_ref[0, rows, :]
        s_scr[slot, rows, :] = sj
        mj = jnp.max(sj.reshape(SWA_SUB // 8, 8, tq), axis=0)
        return mj if m8 is None else jnp.maximum(m8, mj)

    def head_body(hd, m8_cur, slot, has_next):
        sk = sink_ref[hd] * LOG2E
        m = jnp.maximum(jnp.max(m8_cur, axis=0, keepdims=True), sk)
        mb = jnp.broadcast_to(m, (SWA_CHUNK, tq))
        g = hd // SWA_GROUP
        n_q = n_sub if has_next else 0
        m8_next = None
        acc = None
        for t in range(min(SWA_QK_LEAD, n_q)):
            m8_next = scores_into(1 - slot, t, hd + 1, m8_next)
        for t in range(n_sub):
            base = t * SWA_SUB
            for r in range(base, base + SWA_SUB, SWA_CHUNK):
                x = s_scr[slot, r:r + SWA_CHUNK, :] - mb
                p_scr[r:r + SWA_CHUNK, :] = jnp.exp2(x).astype(BF16)
            d = _dot(v_scr[g, :, base:base + SWA_SUB], p_scr[base:base + SWA_SUB, :])
            acc = d if acc is None else acc + d
            if t + SWA_QK_LEAD < n_q:
                m8_next = scores_into(1 - slot, t + SWA_QK_LEAD, hd + 1, m8_next)
        denom = acc[SWA_HD:SWA_HD + 1] + jnp.exp2(sk - m)
        row = pl.multiple_of(hd * SWA_HD, SWA_HD)
        ot_scr[pl.ds(row, SWA_HD), :] = acc[0:SWA_HD] / denom
        return m8_next

    def head_step(hd, m8):
        return lax.cond(lax.rem(hd, 2) == 0,
                        lambda mm: head_body(hd, mm, 0, True),
                        lambda mm: head_body(hd, mm, 1, True), m8)

    m8_first = None
    for t in range(n_sub):
        m8_first = scores_into(0, t, 0, m8_first)
    m8_last = lax.fori_loop(0, SWA_HEADS - 1, head_step, m8_first)
    head_body(SWA_HEADS - 1, m8_last, (SWA_HEADS - 1) % 2, False)
    o_ref[0] = ot_scr[...].T.astype(BF16)


def _swa_bias(seq):
    nt = seq // TOK_TILE
    out = []
    for j in (0, 1, nt - 1):
        start = min(max(j * TOK_TILE - WINDOW, 0), seq - SWA_WIN)
        qpos = j * TOK_TILE + np.arange(TOK_TILE)[None, :]
        kpos = start + np.arange(SWA_WIN)[:, None]
        out.append(np.where(np.abs(qpos - kpos) <= WINDOW, 0.0, NEG_INF))
    return jnp.asarray(np.stack(out), F32)


def _swa_attention(sink, qst, ks_c, ks_l, vst_c, vst_l):
    bsz, _, _, s = qst.shape
    n_ctx = ks_c.shape[1]
    nt = s // TOK_TILE
    assert nt >= 3 and s >= SWA_WIN and n_ctx % SWA_SUB == 0
    n_keys = SWA_WIN + n_ctx
    bias_idx = lambda b, j: (jnp.where(j == 0, 0, jnp.where(j == nt - 1, 2, 1)), 0, 0)
    return pl.pallas_call(
        functools.partial(_swa_kernel, seq=s),
        grid=(bsz, nt),
        in_specs=[
            pl.BlockSpec(memory_space=pltpu.SMEM),
            pl.BlockSpec((1, SWA_WIN, TOK_TILE), bias_idx),
            pl.BlockSpec((1, SWA_HEADS, LANE, TOK_TILE), lambda b, j: (b, 0, 0, j)),
            pl.BlockSpec((1, n_ctx, LANE), lambda b, j: (b, 0, 0)),
            pl.BlockSpec((1, s, LANE), lambda b, j: (b, 0, 0)),
            pl.BlockSpec((1, SWA_KV_HEADS, n_ctx // LANE, SWA_V_AUG, LANE), lambda b, j: (b, 0, 0, 0, 0)),
            pl.BlockSpec((1, SWA_KV_HEADS, s // LANE, SWA_V_AUG, LANE), lambda b, j: (b, 0, 0, 0, 0)),
        ],
        out_specs=pl.BlockSpec((1, TOK_TILE, SWA_HEADS * SWA_HD), lambda b, j: (b, j, 0)),
        out_shape=jax.ShapeDtypeStruct((bsz, s, SWA_HEADS * SWA_HD), BF16),
        scratch_shapes=[
            pltpu.VMEM((SWA_HEADS * SWA_HD, TOK_TILE), F32),
            pltpu.VMEM((n_keys, LANE), BF16),
            pltpu.VMEM((SWA_KV_HEADS, SWA_V_AUG, n_keys), BF16),
            pltpu.VMEM((2, n_keys, TOK_TILE), F32),
            pltpu.VMEM((n_keys, TOK_TILE), BF16),
        ],
        compiler_params=pltpu.CompilerParams(vmem_limit_bytes=VMEM_LIMIT),
        name="swa_attention",
    )(sink, _swa_bias(s), qst, ks_c, ks_l, vst_c, vst_l)


def _mlp_kernel(x_ref, mm_ref, ms_ref, mod_ref, wo_ref, gmlp_ref, w1_ref, w2_ref, o_ref, acc_ref):
    half = MLA_HEADS * MLA_V
    y1 = _dot(mm_ref[0], wo_ref[0:half, :]) + _dot(ms_ref[0], wo_ref[half:, :])
    x1 = x_ref[0] + mod_ref[0, 2:3, :] * y1
    h2 = (_rms_rows(x1, D_MODEL) * gmlp_ref[...] * (1.0 + mod_ref[0, 4:5, :])
          + mod_ref[0, 3:4, :]).astype(BF16)
    for c in range(D_FF // FF_CHUNK):
        a = jnp.maximum(_dot(h2, w1_ref[:, c * FF_CHUNK:(c + 1) * FF_CHUNK]), 0.0)
        part = _dot((a * a).astype(BF16), w2_ref[c * FF_CHUNK:(c + 1) * FF_CHUNK, :])
        if c == 0:
            acc_ref[...] = part
        else:
            acc_ref[...] += part
    o_ref[0] = x1 + mod_ref[0, 5:6, :] * acc_ref[...]


def _outproj_mlp(x, mix_m, mix_s, mod3, w_out, g_mlp, w1, w2):
    bsz, s, _ = x.shape
    half = MLA_HEADS * MLA_V
    const = lambda b, j: (0, 0)
    single = pl.Buffered(1)
    return pl.pallas_call(
        _mlp_kernel,
        grid=(bsz, s // MLP_TILE),
        in_specs=[
            pl.BlockSpec((1, MLP_TILE, D_MODEL), lambda b, j: (b, j, 0)),
            pl.BlockSpec((1, MLP_TILE, half), lambda b, j: (b, j, 0)),
            pl.BlockSpec((1, MLP_TILE, half), lambda b, j: (b, j, 0)),
            pl.BlockSpec((1, 6, D_MODEL), lambda b, j: (b, 0, 0)),
            pl.BlockSpec((D_MODEL, D_MODEL), const, pipeline_mode=single),
            pl.BlockSpec((1, D_MODEL), const),
            pl.BlockSpec((D_MODEL, D_FF), const, pipeline_mode=single),
            pl.BlockSpec((D_FF, D_MODEL), const, pipeline_mode=single),
        ],
        out_specs=pl.BlockSpec((1, MLP_TILE, D_MODEL), lambda b, j: (b, j, 0)),
        out_shape=jax.ShapeDtypeStruct((bsz, s, D_MODEL), F32),
        scratch_shapes=[pltpu.VMEM((MLP_TILE, D_MODEL), F32)],
        compiler_params=pltpu.CompilerParams(vmem_limit_bytes=VMEM_LIMIT),
        name="outproj_mlp",
    )(x, mix_m, mix_s, mod3, w_out, g_mlp, w1, w2)


def _rope_tables(seq):
    f32 = np.float32

    def cos_sin(rot_dim):
        n_freq = rot_dim // 4
        inv = f32(ROPE_THETA) ** (-np.arange(n_freq, dtype=f32) / f32(n_freq))
        rows = seq // GRID_W
        row = np.repeat(np.arange(rows, dtype=f32), GRID_W)
        col = np.tile(np.arange(GRID_W, dtype=f32), rows)
        ang = np.concatenate([row[:, None] * inv, col[:, None] * inv], axis=-1).astype(f32)
        return np.cos(ang).astype(f32), np.sin(ang).astype(f32)

    mcos, msin = cos_sin(MLA_ROPE)
    scos, ssin = cos_sin(SWA_HD)
    z16 = np.zeros((seq, 16), f32)
    z32 = np.zeros((seq, 32), f32)
    z64 = np.zeros((seq, 64), f32)
    kc = np.concatenate([np.ones((seq, 64), f32), mcos, mcos, np.ones((seq, 32), f32)], axis=1)
    ks1 = np.concatenate([z64, -msin, z16, z32], axis=1)
    ks2 = np.concatenate([z64, z16, msin, z32], axis=1)
    sc = np.concatenate([scos, scos, scos, scos], axis=1)
    ss1 = np.concatenate([-ssin, z32, -ssin, z32], axis=1)
    ss2 = np.concatenate([z32, ssin, z32, ssin], axis=1)
    tabs = dict(mcos=mcos.T, msin=msin.T, scos=scos.T, ssin=ssin.T,
                kc=kc, ks1=ks1, ks2=ks2, sc=sc, ss1=ss1, ss2=ss2)
    return {k: jnp.asarray(np.ascontiguousarray(v)) for k, v in tabs.items()}


def _prep_weights(g_attn, w_in, g_q_a, w_uq, g_kv_a, w_ukv, g_mla_q, g_mla_k, g_swa_q, g_swa_k):
    o_ckv = Q_LORA
    o_kr = o_ckv + KV_LORA
    o_qs = o_kr + MLA_ROPE
    o_ks = o_qs + SWA_HEADS * SWA_HD
    o_vs = o_ks + SWA_KV_HEADS * SWA_HD
    w_cq = w_in[:, :o_ckv]
    w_ckv = w_in[:, o_ckv:o_kr]
    w_kr = w_in[:, o_kr:o_qs]
    w_qs = w_in[:, o_qs:o_ks]
    w_ks = w_in[:, o_ks:o_vs]
    w_vs = w_in[:, o_vs:]
    w_kr_p = jnp.pad(w_kr, ((0, 0), (MLA_NOPE, LANE - MLA_QK)))
    wtok = jnp.concatenate([w_cq, w_ckv, w_kr_p, w_ks], axis=1).astype(BF16)
    w_ukv_h = w_ukv.reshape(KV_LORA, MLA_HEADS, MLA_NOPE + MLA_V)
    wuk = jnp.pad(w_ukv_h[:, :, :MLA_NOPE], ((0, 0), (0, 0), (0, LANE - MLA_NOPE)))
    wuk = wuk.reshape(KV_LORA, MLA_HEADS * LANE).astype(BF16)
    wuv = w_ukv_h[:, :, MLA_NOPE:].reshape(KV_LORA, MLA_HEADS * MLA_V).T.astype(BF16)
    w_uq_h = jnp.pad(w_uq.reshape(Q_LORA, MLA_HEADS, MLA_QK), ((0, 0), (0, 0), (0, LANE - MLA_QK)))
    wuq = w_uq_h.reshape(Q_LORA, MLA_HEADS * LANE).T.astype(BF16)
    return dict(
        gattn=g_attn[None, :], wtok=wtok, wvs=w_vs.T.astype(BF16), gkva=g_kv_a[None, :],
        wuk=wuk, wuv=wuv, gmk=jnp.pad(g_mla_k, (0, LANE - MLA_QK))[None, :],
        gsk=jnp.tile(g_swa_k, SWA_KV_HEADS)[None, :],
        wqs=w_qs.T.astype(BF16), gqa=g_q_a[None, :], wuq=wuq,
        gmq=jnp.pad(g_mla_q, (0, LANE - MLA_QK))[:, None], gsq=g_swa_q[:, None],
    )


def kernel(x, c, ctx, c_ctx, w_mod, b_mod, g_attn, w_in, g_q_a, w_uq, g_kv_a, w_ukv, g_mla_q, g_mla_k,
           g_swa_q, g_swa_k, swa_sink, w_out, g_mlp, w_mlp1, w_mlp2):
    bsz, seq, _ = x.shape
    assert w_mod.shape[0] == 1, "single-layer block"
    assert bsz < 8 and seq % MLP_TILE == 0 and ctx.shape[1] % TOK_TILE == 0

    cond = jnp.zeros((8, D_MODEL), F32).at[:bsz].set(c).at[bsz].set(c_ctx)
    mod = _modulation(cond, w_mod[0], b_mod[0][None, :])
    mod3 = mod.reshape(8, 6, D_MODEL)

    weights = _prep_weights(g_attn[0], w_in[0], g_q_a[0], w_uq[0], g_kv_a[0], w_ukv[0],
                            g_mla_q[0], g_mla_k[0], g_swa_q[0], g_swa_k[0])
    tables = _rope_tables(seq)

    km_l, vmt_l, ks_l, vst_l, qmt, qst = _project(x, mod3, None, weights, tables, latent=True, tile=PROJ_TILE)
    km_c, vmt_c, ks_c, vst_c = _project(ctx, mod3, bsz, weights, None, latent=False, tile=TOK_TILE)

    mix_m = _mla_attention(qmt, km_c, km_l, vmt_c, vmt_l)
    mix_s = _swa_attention(swa_sink[0], qst, ks_c, ks_l, vst_c, vst_l)

    return _outproj_mlp(x, mix_m, mix_s, mod3, w_out[0].astype(BF16), g_mlp[0][None, :],
                        w_mlp1[0].astype(BF16), w_mlp2[0].astype(BF16))
```

```python
import functools
import math

import jax
import jax.numpy as jnp
import numpy as np
from jax import lax
from jax.experimental import pallas as pl
from jax.experimental.pallas import tpu as pltpu

D_MODEL = 1024
GRID_W = 64
MLA_HEADS = 8
MLA_NOPE = 64
MLA_ROPE = 32
MLA_QK = MLA_NOPE + MLA_ROPE
MLA_V = 64
MLA_V_AUG = MLA_V + 16
Q_LORA = 256
KV_LORA = 128
SWA_HEADS = 8
SWA_KV_HEADS = 2
SWA_GROUP = SWA_HEADS // SWA_KV_HEADS
SWA_HD = 64
SWA_V_AUG = SWA_HD + 16
WINDOW = 128
D_FF = 4 * D_MODEL
ROPE_THETA = 10000.0
EPS = 1e-6
NEG_INF = -1e30
LOG2E = 1.4426950408889634
MLA_QSCALE = LOG2E / math.sqrt(MLA_QK)
SWA_QSCALE = LOG2E / math.sqrt(SWA_HD)

LANE = 128
TOK_TILE = 256
PROJ_TILE = 512
MLP_TILE = 512
FF_CHUNK = 1024
SWA_WIN = 512
MLA_SUB = 256
SWA_SUB = 256
SWA_CHUNK = 32
SWA_QK_LEAD = 1
MLA_CHUNK = 32
MLA_QK_LEAD = 3
VMEM_LIMIT = 56 * 1024 * 1024

F32 = jnp.float32
BF16 = jnp.bfloat16
NT_DIMS = (((1,), (1,)), ((), ()))


def _dot(a, b):
    return jnp.dot(a, b, preferred_element_type=F32)


def _dot_nt(a, b):
    return lax.dot_general(a, b, NT_DIMS, preferred_element_type=F32)


def _mod_kernel(cond_ref, w_ref, b_ref, o_ref):
    cnd = cond_ref[...]
    act = cnd * jax.nn.sigmoid(cnd)
    o_ref[...] = _dot(act.astype(BF16), w_ref[...].astype(BF16)) + b_ref[...]


def _modulation(cond, w_mod, b_mod):
    n = w_mod.shape[1]
    tn = 1024
    return pl.pallas_call(
        _mod_kernel,
        grid=(n // tn,),
        in_specs=[
            pl.BlockSpec((8, D_MODEL), lambda i: (0, 0)),
            pl.BlockSpec((D_MODEL, tn), lambda i: (0, i)),
            pl.BlockSpec((1, tn), lambda i: (0, i)),
        ],
        out_specs=pl.BlockSpec((8, tn), lambda i: (0, i)),
        out_shape=jax.ShapeDtypeStruct((8, n), F32),
        name="modulation",
    )(cond, w_mod, b_mod)


def _rms_rows(x, n):
    return x * lax.rsqrt(jnp.sum(x * x, axis=-1, keepdims=True) * (1.0 / n) + EPS)


def _rms_cols(x, n):
    return x * lax.rsqrt(jnp.sum(x * x, axis=0, keepdims=True) * (1.0 / n) + EPS)


def _proj_kernel(*refs, latent):
    if latent:
        (x_ref, mod_ref, gattn_ref, wtok_ref, wvs_ref, gkva_ref, wuk_ref, wuv_ref, gmk_ref, gsk_ref,
         wqs_ref, gqa_ref, wuq_ref, gmq_ref, gsq_ref,
         mcos_ref, msin_ref, scos_ref, ssin_ref, kc_ref, ks1_ref, ks2_ref, sc_ref, ss1_ref, ss2_ref,
         km_ref, vmt_ref, ks_ref, vst_ref, qmt_ref, qst_ref) = refs
    else:
        (x_ref, mod_ref, gattn_ref, wtok_ref, wvs_ref, gkva_ref, wuk_ref, wuv_ref, gmk_ref, gsk_ref,
         km_ref, vmt_ref, ks_ref, vst_ref) = refs

    tile = x_ref.shape[1]

    x = x_ref[0]
    gain = gattn_ref[...] * (1.0 + mod_ref[0, 1:2, :])
    h = (_rms_rows(x, D_MODEL) * gain + mod_ref[0, 0:1, :]).astype(BF16)

    zt = _dot(h, wtok_ref[...])
    cq = zt[:, 0:256]
    ckv = zt[:, 256:384]
    krp = zt[:, 384:512]
    ksw = zt[:, 512:640]

    ckvn = (_rms_rows(ckv, KV_LORA) * gkva_ref[...]).astype(BF16)
    kpre = _dot(ckvn, wuk_ref[...])
    vt = _dot_nt(wuv_ref[...], ckvn)
    gmk = gmk_ref[...]
    kr = krp * gmk
    if latent:
        kr = kr * kc_ref[...] + pltpu.roll(kr, 112, 1) * ks1_ref[...] + pltpu.roll(kr, 16, 1) * ks2_ref[...]
    ss_kr = jnp.sum(krp * krp, axis=-1, keepdims=True)
    pad_rows = MLA_V_AUG - MLA_V
    ones_row = (lax.broadcasted_iota(jnp.int32, (pad_rows, tile), 0) == 0).astype(BF16)
    for hd in range(MLA_HEADS):
        kp = kpre[:, hd * LANE:(hd + 1) * LANE]
        ss = jnp.sum(kp * kp, axis=-1, keepdims=True) + ss_kr
        kn = (kp * gmk + kr) * lax.rsqrt(ss * (1.0 / MLA_QK) + EPS)
        km_ref[0, hd] = kn.astype(BF16)
        vmt_ref[0, hd, 0:MLA_V, :] = vt[hd * MLA_V:(hd + 1) * MLA_V].astype(BF16)
        vmt_ref[0, hd, MLA_V:MLA_V_AUG, :] = ones_row

    lane = lax.broadcasted_iota(jnp.int32, ksw.shape, 1)
    first = lane < SWA_HD
    sq = ksw * ksw
    ss0 = jnp.sum(jnp.where(first, sq, 0.0), axis=-1, keepdims=True)
    ss1 = jnp.sum(jnp.where(first, 0.0, sq), axis=-1, keepdims=True)
    rr = jnp.where(first, lax.rsqrt(ss0 * (1.0 / SWA_HD) + EPS), lax.rsqrt(ss1 * (1.0 / SWA_HD) + EPS))
    ksn = ksw * rr * gsk_ref[...]
    if latent:
        ksn = (ksn * sc_ref[...] + pltpu.roll(ksn, 96, 1) * ss1_ref[...]
               + pltpu.roll(ksn, 32, 1) * ss2_ref[...])
    ks_ref[0] = ksn.astype(BF16)
    vst = _dot_nt(wvs_ref[...], h)
    for g in range(SWA_KV_HEADS):
        for t in range(tile // LANE):
            vst_ref[0, g, t, 0:SWA_HD, :] = vst[g * SWA_HD:(g + 1) * SWA_HD, t * LANE:(t + 1) * LANE].astype(BF16)
            vst_ref[0, g, t, SWA_HD:SWA_V_AUG, :] = ones_row[:, 0:LANE]

    if not latent:
        return

    cqn = (_rms_rows(cq, Q_LORA) * gqa_ref[...]).astype(BF16)
    qt = _dot_nt(wuq_ref[...], cqn)
    gmq = gmq_ref[...] * MLA_QSCALE
    mcos = mcos_ref[...]
    msin = msin_ref[...]
    for hd in range(MLA_HEADS):
        qn = _rms_cols(qt[hd * LANE:(hd + 1) * LANE], MLA_QK) * gmq
        x1 = qn[64:80]
        x2 = qn[80:96]
        qmt_ref[0, hd, 0:64, :] = qn[0:64].astype(BF16)
        qmt_ref[0, hd, 64:80, :] = (x1 * mcos - x2 * msin).astype(BF16)
        qmt_ref[0, hd, 80:96, :] = (x2 * mcos + x1 * msin).astype(BF16)
        qmt_ref[0, hd, 96:128, :] = jnp.zeros((32, tile), BF16)

    qst = _dot_nt(wqs_ref[...], h)
    gsq = gsq_ref[...] * SWA_QSCALE
    scos = scos_ref[...]
    ssin = ssin_ref[...]
    for hd in range(SWA_HEADS):
        qn = _rms_cols(qst[hd * SWA_HD:(hd + 1) * SWA_HD], SWA_HD) * gsq
        x1 = qn[0:32]
        x2 = qn[32:64]
        g = hd // SWA_GROUP
        base = g * SWA_HD
        other = (1 - g) * SWA_HD
        qst_ref[0, hd, base:base + 32, :] = (x1 * scos - x2 * ssin).astype(BF16)
        qst_ref[0, hd, base + 32:base + 64, :] = (x2 * scos + x1 * ssin).astype(BF16)
        qst_ref[0, hd, other:other + 64, :] = jnp.zeros((64, tile), BF16)


def _full(shape):
    nd = len(shape)
    return pl.BlockSpec(shape, lambda b, j: (0,) * nd)


def _project(x, mod3, mod_row, weights, tables, latent, tile):
    bsz, n, _ = x.shape
    nt = n // tile
    if mod_row is None:
        mod_map = lambda b, j: (b, 0, 0)
    else:
        mod_map = lambda b, j: (mod_row, 0, 0)
    common = [weights[k] for k in ("gattn", "wtok", "wvs", "gkva", "wuk", "wuv", "gmk", "gsk")]
    ins = [x, mod3] + common
    in_specs = [
        pl.BlockSpec((1, tile, D_MODEL), lambda b, j: (b, j, 0)),
        pl.BlockSpec((1, 6, D_MODEL), mod_map),
    ] + [_full(w.shape) for w in common]
    out_shape = [
        jax.ShapeDtypeStruct((bsz, MLA_HEADS, n, LANE), BF16),
        jax.ShapeDtypeStruct((bsz, MLA_HEADS, MLA_V_AUG, n), BF16),
        jax.ShapeDtypeStruct((bsz, n, LANE), BF16),
        jax.ShapeDtypeStruct((bsz, SWA_KV_HEADS, n // LANE, SWA_V_AUG, LANE), BF16),
    ]
    out_specs = [
        pl.BlockSpec((1, MLA_HEADS, tile, LANE), lambda b, j: (b, 0, j, 0)),
        pl.BlockSpec((1, MLA_HEADS, MLA_V_AUG, tile), lambda b, j: (b, 0, 0, j)),
        pl.BlockSpec((1, tile, LANE), lambda b, j: (b, j, 0)),
        pl.BlockSpec((1, SWA_KV_HEADS, tile // LANE, SWA_V_AUG, LANE), lambda b, j: (b, 0, j, 0, 0)),
    ]
    if latent:
        extra = [weights[k] for k in ("wqs", "gqa", "wuq", "gmq", "gsq")]
        ins += extra
        in_specs += [_full(w.shape) for w in extra]
        feat = [tables[k] for k in ("mcos", "msin", "scos", "ssin")]
        ins += feat
        in_specs += [pl.BlockSpec((t.shape[0], tile), lambda b, j: (0, j)) for t in feat]
        tok = [tables[k] for k in ("kc", "ks1", "ks2", "sc", "ss1", "ss2")]
        ins += tok
        in_specs += [pl.BlockSpec((tile, LANE), lambda b, j: (j, 0)) for _ in tok]
        out_shape += [
            jax.ShapeDtypeStruct((bsz, MLA_HEADS, LANE, n), BF16),
            jax.ShapeDtypeStruct((bsz, SWA_HEADS, LANE, n), BF16),
        ]
        out_specs += [
            pl.BlockSpec((1, MLA_HEADS, LANE, tile), lambda b, j: (b, 0, 0, j)),
            pl.BlockSpec((1, SWA_HEADS, LANE, tile), lambda b, j: (b, 0, 0, j)),
        ]
    return pl.pallas_call(
        functools.partial(_proj_kernel, latent=latent),
        grid=(bsz, nt),
        in_specs=in_specs,
        out_specs=out_specs,
        out_shape=out_shape,
        compiler_params=pltpu.CompilerParams(vmem_limit_bytes=VMEM_LIMIT),
        name="project_latent" if latent else "project_context",
    )(*ins)


def _mla_kernel(qt_ref, qtn_ref, kc_ref, kl_ref, vct_ref, vlt_ref, o_ref, ot_scr, s_scr, p_scr, m8_scr, *, seq):
    tile_idx = pl.program_id(1)
    n_tile = pl.num_programs(1)
    n_ctx = kc_ref.shape[2]
    tq = qt_ref.shape[3]
    subs = ([(kc_ref, vct_ref, o) for o in range(0, n_ctx, MLA_SUB)]
            + [(kl_ref, vlt_ref, o) for o in range(0, seq, MLA_SUB)])
    n_sub = len(subs)

    def scores_into(slot, j, nxt, m8):
        q_ref, hd = nxt
        k_ref, _, off = subs[j]
        sj = _dot(k_ref[0, hd, off:off + MLA_SUB, :], q_ref[0, hd])
        s_scr[slot, j * MLA_SUB:(j + 1) * MLA_SUB, :] = sj
        mj = jnp.max(sj.reshape(MLA_SUB // 8, 8, tq), axis=0)
        return mj if m8 is None else jnp.maximum(m8, mj)

    def head_body(hd, m8_cur, slot, nxt):
        mb = jnp.broadcast_to(jnp.max(m8_cur, axis=0, keepdims=True), (MLA_CHUNK, tq))
        n_q = n_sub if nxt is not None else 0
        m8_next = None
        acc = None
        for j in range(min(MLA_QK_LEAD, n_q)):
            m8_next = scores_into(1 - slot, j, nxt, m8_next)
        for j in range(n_sub):
            base = j * MLA_SUB
            for r in range(base, base + MLA_SUB, MLA_CHUNK):
                x = s_scr[slot, r:r + MLA_CHUNK, :] - mb
                p_scr[r:r + MLA_CHUNK, :] = jnp.exp2(x).astype(BF16)
            _, v_ref, off = subs[j]
            d = _dot(v_ref[0, hd, :, off:off + MLA_SUB], p_scr[base:base + MLA_SUB, :])
            acc = d if acc is None else acc + d
            if j + MLA_QK_LEAD < n_q:
                m8_next = scores_into(1 - slot, j + MLA_QK_LEAD, nxt, m8_next)
        row = pl.multiple_of(hd * MLA_V, MLA_V)
        ot_scr[pl.ds(row, MLA_V), :] = acc[0:MLA_V] / acc[MLA_V:MLA_V + 1]
        return m8_next

    def head_step(hd, m8):
        return lax.cond(lax.rem(hd, 2) == 0,
                        lambda m: head_body(hd, m, 0, (qt_ref, hd + 1)),
                        lambda m: head_body(hd, m, 1, (qt_ref, hd + 1)), m8)

    last = MLA_HEADS - 1
    assert MLA_HEADS % 2 == 0

    @pl.when(tile_idx == 0)
    def _():
        m8 = None
        for j in range(n_sub):
            m8 = scores_into(0, j, (qt_ref, 0), m8)
        m8_scr[...] = m8

    m8_last = lax.fori_loop(0, last, head_step, m8_scr[...])

    @pl.when(tile_idx < n_tile - 1)
    def _():
        m8_scr[...] = head_body(last, m8_last, last % 2, (qtn_ref, 0))

    @pl.when(tile_idx == n_tile - 1)
    def _():
        head_body(last, m8_last, last % 2, None)

    o_ref[0] = ot_scr[...].T.astype(BF16)


def _mla_attention(qmt, km_c, km_l, vmt_c, vmt_l):
    bsz, _, _, s = qmt.shape
    n_ctx = km_c.shape[2]
    nt = s // TOK_TILE
    n_keys = n_ctx + s
    return pl.pallas_call(
        functools.partial(_mla_kernel, seq=s),
        grid=(bsz, nt),
        in_specs=[
            pl.BlockSpec((1, MLA_HEADS, LANE, TOK_TILE), lambda b, j: (b, 0, 0, j)),
            pl.BlockSpec((1, 1, LANE, TOK_TILE), lambda b, j: (b, 0, 0, jnp.minimum(j + 1, nt - 1))),
            pl.BlockSpec((1, MLA_HEADS, n_ctx, LANE), lambda b, j: (b, 0, 0, 0)),
            pl.BlockSpec((1, MLA_HEADS, s, LANE), lambda b, j: (b, 0, 0, 0)),
            pl.BlockSpec((1, MLA_HEADS, MLA_V_AUG, n_ctx), lambda b, j: (b, 0, 0, 0)),
            pl.BlockSpec((1, MLA_HEADS, MLA_V_AUG, s), lambda b, j: (b, 0, 0, 0)),
        ],
        out_specs=pl.BlockSpec((1, TOK_TILE, MLA_HEADS * MLA_V), lambda b, j: (b, j, 0)),
        out_shape=jax.ShapeDtypeStruct((bsz, s, MLA_HEADS * MLA_V), BF16),
        scratch_shapes=[
            pltpu.VMEM((MLA_HEADS * MLA_V, TOK_TILE), F32),
            pltpu.VMEM((2, n_keys, TOK_TILE), F32),
            pltpu.VMEM((n_keys, TOK_TILE), BF16),
            pltpu.VMEM((8, TOK_TILE), F32),
        ],
        compiler_params=pltpu.CompilerParams(vmem_limit_bytes=VMEM_LIMIT,
                                             dimension_semantics=("arbitrary", "arbitrary")),
        name="mla_attention",
    )(qmt, qmt, km_c, km_l, vmt_c, vmt_l)


def _swa_kernel(sink_ref, bias_ref, qt_ref, qtn_ref, kc_ref, kl_ref, vct_ref, vlt_ref, o_ref,
                ot_scr, s_scr, p_scr, m8_scr, *, seq):
    j = pl.program_id(1)
    nt = pl.num_programs(1)
    n_ctx = kc_ref.shape[1]
    tq = qt_ref.shape[3]
    n_band = SWA_WIN // SWA_SUB
    n_t = n_band + n_ctx // SWA_SUB
    lanes_per_sub = SWA_SUB // LANE
    subs = [(hh, t) for hh in range(SWA_GROUP) for t in range(n_t)]

    def window(jj):
        start = pl.multiple_of(jnp.clip(jj * TOK_TILE - WINDOW, 0, seq - SWA_WIN), LANE)
        variant = jnp.where(jj == 0, 0, jnp.where(jj == nt - 1, 2, 1))
        return start, variant

    def rows_of(idx):
        return slice(idx * SWA_SUB, (idx + 1) * SWA_SUB)

    def scores_into(slot, idx, q_ref, g, win):
        hh, t = subs[idx]
        start, variant = win
        if t < n_band:
            k = kl_ref[0, pl.ds(start + t * SWA_SUB, SWA_SUB), :]
        else:
            k = kc_ref[0, (t - n_band) * SWA_SUB:(t - n_band + 1) * SWA_SUB, :]
        sj = _dot(k, q_ref[0, g * SWA_GROUP + hh])
        if t < n_band:
            sj = sj + bias_ref[variant, t * SWA_SUB:(t + 1) * SWA_SUB, :]
        s_scr[slot, rows_of(idx), :] = sj
        return jnp.max(sj.reshape(SWA_SUB // 8, 8, tq), axis=0)

    def values_t(g, t, t0):
        if t < n_band:
            tiles = [vlt_ref[0, g, t0 + t * lanes_per_sub + i] for i in range(lanes_per_sub)]
        else:
            tiles = [vct_ref[0, g, (t - n_band) * lanes_per_sub + i] for i in range(lanes_per_sub)]
        return jnp.concatenate(tiles, axis=1)

    def next_scores(slot, idx, nxt, m8n):
        q_ref, g_n, win_n = nxt
        mj = scores_into(slot, idx, q_ref, g_n, win_n)
        hh, t = subs[idx]
        m8n[hh] = mj if t == 0 else jnp.maximum(m8n[hh], mj)
        if t == n_t - 1:
            m8_scr[slot, hh] = m8n[hh]

    def stage_body(g, win_cur, nxt):
        slot = g
        t0 = win_cur[0] // LANE
        n_q = len(subs) if nxt is not None else 0
        m8n = {}
        for idx in range(min(SWA_QK_LEAD, n_q)):
            next_scores(1 - slot, idx, nxt, m8n)
        for idx, (hh, t) in enumerate(subs):
            hd = g * SWA_GROUP + hh
            if t == 0:
                sk = sink_ref[hd] * LOG2E
                m = jnp.maximum(jnp.max(m8_scr[slot, hh], axis=0, keepdims=True), sk)
                mb = jnp.broadcast_to(m, (SWA_CHUNK, tq))
                acc = None
            base = idx * SWA_SUB
            for r in range(base, base + SWA_SUB, SWA_CHUNK):
                x = s_scr[slot, r:r + SWA_CHUNK, :] - mb
                p_scr[r:r + SWA_CHUNK, :] = jnp.exp2(x).astype(BF16)
            d = _dot(values_t(g, t, t0), p_scr[base:base + SWA_SUB, :])
            acc = d if acc is None else acc + d
            if t == n_t - 1:
                denom = acc[SWA_HD:SWA_HD + 1] + jnp.exp2(sk - m)
                ot_scr[hd * SWA_HD:(hd + 1) * SWA_HD, :] = acc[0:SWA_HD] / denom
            if idx + SWA_QK_LEAD < n_q:
                next_scores(1 - slot, idx + SWA_QK_LEAD, nxt, m8n)

    win = window(j)

    @pl.when(j == 0)
    def _():
        m8n = {}
        for idx in range(len(subs)):
            next_scores(0, idx, (qt_ref, 0, win), m8n)

    stage_body(0, win, (qt_ref, 1, win))

    @pl.when(j < nt - 1)
    def _():
        stage_body(1, win, (qtn_ref, 0, window(j + 1)))

    @pl.when(j == nt - 1)
    def _():
        stage_body(1, win, None)

    o_ref[0] = ot_scr[...].T.astype(BF16)


def _swa_bias(seq):
    nt = seq // TOK_TILE
    out = []
    for j in (0, 1, nt - 1):
        start = min(max(j * TOK_TILE - WINDOW, 0), seq - SWA_WIN)
        qpos = j * TOK_TILE + np.arange(TOK_TILE)[None, :]
        kpos = start + np.arange(SWA_WIN)[:, None]
        out.append(np.where(np.abs(qpos - kpos) <= WINDOW, 0.0, NEG_INF))
    return jnp.asarray(np.stack(out), F32)


def _swa_attention(sink, qst, ks_c, ks_l, vst_c, vst_l):
    bsz, _, _, s = qst.shape
    n_ctx = ks_c.shape[1]
    nt = s // TOK_TILE
    assert nt >= 3 and s >= SWA_WIN and n_ctx % SWA_SUB == 0
    stage_rows = SWA_GROUP * (SWA_WIN + n_ctx)
    return pl.pallas_call(
        functools.partial(_swa_kernel, seq=s),
        grid=(bsz, nt),
        in_specs=[
            pl.BlockSpec(memory_space=pltpu.SMEM),
            pl.BlockSpec((3, SWA_WIN, TOK_TILE), lambda b, j: (0, 0, 0)),
            pl.BlockSpec((1, SWA_HEADS, LANE, TOK_TILE), lambda b, j: (b, 0, 0, j)),
            pl.BlockSpec((1, SWA_GROUP, LANE, TOK_TILE), lambda b, j: (b, 0, 0, jnp.minimum(j + 1, nt - 1))),
            pl.BlockSpec((1, n_ctx, LANE), lambda b, j: (b, 0, 0)),
            pl.BlockSpec((1, s, LANE), lambda b, j: (b, 0, 0)),
            pl.BlockSpec((1, SWA_KV_HEADS, n_ctx // LANE, SWA_V_AUG, LANE), lambda b, j: (b, 0, 0, 0, 0)),
            pl.BlockSpec((1, SWA_KV_HEADS, s // LANE, SWA_V_AUG, LANE), lambda b, j: (b, 0, 0, 0, 0)),
        ],
        out_specs=pl.BlockSpec((1, TOK_TILE, SWA_HEADS * SWA_HD), lambda b, j: (b, j, 0)),
        out_shape=jax.ShapeDtypeStruct((bsz, s, SWA_HEADS * SWA_HD), BF16),
        scratch_shapes=[
            pltpu.VMEM((SWA_HEADS * SWA_HD, TOK_TILE), F32),
            pltpu.VMEM((2, stage_rows, TOK_TILE), F32),
            pltpu.VMEM((stage_rows, TOK_TILE), BF16),
            pltpu.VMEM((2, SWA_GROUP, 8, TOK_TILE), F32),
        ],
        compiler_params=pltpu.CompilerParams(vmem_limit_bytes=VMEM_LIMIT,
                                             dimension_semantics=("arbitrary", "arbitrary")),
        name="swa_attention",
    )(sink, _swa_bias(s), qst, qst, ks_c, ks_l, vst_c, vst_l)


def _mlp_kernel(x_ref, mm_ref, ms_ref, mod_ref, wo_ref, gmlp_ref, w1_ref, w2_ref, o_ref, acc_ref):
    half = MLA_HEADS * MLA_V
    y1 = _dot(mm_ref[0], wo_ref[0:half, :]) + _dot(ms_ref[0], wo_ref[half:, :])
    x1 = x_ref[0] + mod_ref[0, 2:3, :] * y1
    h2 = (_rms_rows(x1, D_MODEL) * gmlp_ref[...] * (1.0 + mod_ref[0, 4:5, :])
          + mod_ref[0, 3:4, :]).astype(BF16)
    for c in range(D_FF // FF_CHUNK):
        a = jnp.maximum(_dot(h2, w1_ref[:, c * FF_CHUNK:(c + 1) * FF_CHUNK]), 0.0)
        part = _dot((a * a).astype(BF16), w2_ref[c * FF_CHUNK:(c + 1) * FF_CHUNK, :])
        if c == 0:
            acc_ref[...] = part
        else:
            acc_ref[...] += part
    o_ref[0] = x1 + mod_ref[0, 5:6, :] * acc_ref[...]


def _outproj_mlp(x, mix_m, mix_s, mod3, w_out, g_mlp, w1, w2):
    bsz, s, _ = x.shape
    half = MLA_HEADS * MLA_V
    const = lambda b, j: (0, 0)
    single = pl.Buffered(1)
    return pl.pallas_call(
        _mlp_kernel,
        grid=(bsz, s // MLP_TILE),
        in_specs=[
            pl.BlockSpec((1, MLP_TILE, D_MODEL), lambda b, j: (b, j, 0)),
            pl.BlockSpec((1, MLP_TILE, half), lambda b, j: (b, j, 0)),
            pl.BlockSpec((1, MLP_TILE, half), lambda b, j: (b, j, 0)),
            pl.BlockSpec((1, 6, D_MODEL), lambda b, j: (b, 0, 0)),
            pl.BlockSpec((D_MODEL, D_MODEL), const, pipeline_mode=single),
            pl.BlockSpec((1, D_MODEL), const),
            pl.BlockSpec((D_MODEL, D_FF), const, pipeline_mode=single),
            pl.BlockSpec((D_FF, D_MODEL), const, pipeline_mode=single),
        ],
        out_specs=pl.BlockSpec((1, MLP_TILE, D_MODEL), lambda b, j: (b, j, 0)),
        out_shape=jax.ShapeDtypeStruct((bsz, s, D_MODEL), F32),
        scratch_shapes=[pltpu.VMEM((MLP_TILE, D_MODEL), F32)],
        compiler_params=pltpu.CompilerParams(vmem_limit_bytes=VMEM_LIMIT),
        name="outproj_mlp",
    )(x, mix_m, mix_s, mod3, w_out, g_mlp, w1, w2)


def _rope_tables(seq):
    f32 = np.float32

    def cos_sin(rot_dim):
        n_freq = rot_dim // 4
        inv = f32(ROPE_THETA) ** (-np.arange(n_freq, dtype=f32) / f32(n_freq))
        rows = seq // GRID_W
        row = np.repeat(np.arange(rows, dtype=f32), GRID_W)
        col = np.tile(np.arange(GRID_W, dtype=f32), rows)
        ang = np.concatenate([row[:, None] * inv, col[:, None] * inv], axis=-1).astype(f32)
        return np.cos(ang).astype(f32), np.sin(ang).astype(f32)

    mcos, msin = cos_sin(MLA_ROPE)
    scos, ssin = cos_sin(SWA_HD)
    z16 = np.zeros((seq, 16), f32)
    z32 = np.zeros((seq, 32), f32)
    z64 = np.zeros((seq, 64), f32)
    kc = np.concatenate([np.ones((seq, 64), f32), mcos, mcos, np.ones((seq, 32), f32)], axis=1)
    ks1 = np.concatenate([z64, -msin, z16, z32], axis=1)
    ks2 = np.concatenate([z64, z16, msin, z32], axis=1)
    sc = np.concatenate([scos, scos, scos, scos], axis=1)
    ss1 = np.concatenate([-ssin, z32, -ssin, z32], axis=1)
    ss2 = np.concatenate([z32, ssin, z32, ssin], axis=1)
    tabs = dict(mcos=mcos.T, msin=msin.T, scos=scos.T, ssin=ssin.T,
                kc=kc, ks1=ks1, ks2=ks2, sc=sc, ss1=ss1, ss2=ss2)
    return {k: jnp.asarray(np.ascontiguousarray(v)) for k, v in tabs.items()}


def _prep_weights(g_attn, w_in, g_q_a, w_uq, g_kv_a, w_ukv, g_mla_q, g_mla_k, g_swa_q, g_swa_k):
    o_ckv = Q_LORA
    o_kr = o_ckv + KV_LORA
    o_qs = o_kr + MLA_ROPE
    o_ks = o_qs + SWA_HEADS * SWA_HD
    o_vs = o_ks + SWA_KV_HEADS * SWA_HD
    w_cq = w_in[:, :o_ckv]
    w_ckv = w_in[:, o_ckv:o_kr]
    w_kr = w_in[:, o_kr:o_qs]
    w_qs = w_in[:, o_qs:o_ks]
    w_ks = w_in[:, o_ks:o_vs]
    w_vs = w_in[:, o_vs:]
    w_kr_p = jnp.pad(w_kr, ((0, 0), (MLA_NOPE, LANE - MLA_QK)))
    wtok = jnp.concatenate([w_cq, w_ckv, w_kr_p, w_ks], axis=1).astype(BF16)
    w_ukv_h = w_ukv.reshape(KV_LORA, MLA_HEADS, MLA_NOPE + MLA_V)
    wuk = jnp.pad(w_ukv_h[:, :, :MLA_NOPE], ((0, 0), (0, 0), (0, LANE - MLA_NOPE)))
    wuk = wuk.reshape(KV_LORA, MLA_HEADS * LANE).astype(BF16)
    wuv = w_ukv_h[:, :, MLA_NOPE:].reshape(KV_LORA, MLA_HEADS * MLA_V).T.astype(BF16)
    w_uq_h = jnp.pad(w_uq.reshape(Q_LORA, MLA_HEADS, MLA_QK), ((0, 0), (0, 0), (0, LANE - MLA_QK)))
    wuq = w_uq_h.reshape(Q_LORA, MLA_HEADS * LANE).T.astype(BF16)
    return dict(
        gattn=g_attn[None, :], wtok=wtok, wvs=w_vs.T.astype(BF16), gkva=g_kv_a[None, :],
        wuk=wuk, wuv=wuv, gmk=jnp.pad(g_mla_k, (0, LANE - MLA_QK))[None, :],
        gsk=jnp.tile(g_swa_k, SWA_KV_HEADS)[None, :],
        wqs=w_qs.T.astype(BF16), gqa=g_q_a[None, :], wuq=wuq,
        gmq=jnp.pad(g_mla_q, (0, LANE - MLA_QK))[:, None], gsq=g_swa_q[:, None],
    )


def kernel(x, c, ctx, c_ctx, w_mod, b_mod, g_attn, w_in, g_q_a, w_uq, g_kv_a, w_ukv, g_mla_q, g_mla_k,
           g_swa_q, g_swa_k, swa_sink, w_out, g_mlp, w_mlp1, w_mlp2):
    bsz, seq, _ = x.shape
    assert w_mod.shape[0] == 1, "single-layer block"
    assert bsz < 8 and seq % MLP_TILE == 0 and ctx.shape[1] % TOK_TILE == 0

    cond = jnp.zeros((8, D_MODEL), F32).at[:bsz].set(c).at[bsz].set(c_ctx)
    mod = _modulation(cond, w_mod[0], b_mod[0][None, :])
    mod3 = mod.reshape(8, 6, D_MODEL)

    weights = _prep_weights(g_attn[0], w_in[0], g_q_a[0], w_uq[0], g_kv_a[0], w_ukv[0],
                            g_mla_q[0], g_mla_k[0], g_swa_q[0], g_swa_k[0])
    tables = _rope_tables(seq)

    km_l, vmt_l, ks_l, vst_l, qmt, qst = _project(x, mod3, None, weights, tables, latent=True, tile=PROJ_TILE)
    km_c, vmt_c, ks_c, vst_c = _project(ctx, mod3, bsz, weights, None, latent=False, tile=TOK_TILE)

    mix_m = _mla_attention(qmt, km_c, km_l, vmt_c, vmt_l)
    mix_s = _swa_attention(swa_sink[0], qst, ks_c, ks_l, vst_c, vst_l)

    return _outproj_mlp(x, mix_m, mix_s, mod3, w_out[0].astype(BF16), g_mlp[0][None, :],
                        w_mlp1[0].astype(BF16), w_mlp2[0].astype(BF16))
```

```python
import functools
import math

import jax
import jax.numpy as jnp
import numpy as np
from jax import lax
from jax.experimental import pallas as pl
from jax.experimental.pallas import tpu as pltpu

D_MODEL = 1024
GRID_W = 64
MLA_HEADS = 8
MLA_NOPE = 64
MLA_ROPE = 32
MLA_QK = MLA_NOPE + MLA_ROPE
MLA_V = 64
MLA_V_AUG = MLA_V + 16
Q_LORA = 256
KV_LORA = 128
SWA_HEADS = 8
SWA_KV_HEADS = 2
SWA_GROUP = SWA_HEADS // SWA_KV_HEADS
SWA_HD = 64
SWA_V_AUG = SWA_HD + 16
WINDOW = 128
D_FF = 4 * D_MODEL
ROPE_THETA = 10000.0
EPS = 1e-6
NEG_INF = -1e30
LOG2E = 1.4426950408889634
MLA_QSCALE = LOG2E / math.sqrt(MLA_QK)
SWA_QSCALE = LOG2E / math.sqrt(SWA_HD)

LANE = 128
TOK_TILE = 256
PROJ_TILE = 512
MLP_TILE = 512
FF_CHUNK = 1024
SWA_WIN = 512
MLA_SUB = 256
SWA_SUB = 256
SWA_CHUNK = 32
SWA_QK_LEAD = 1
MLA_CHUNK = 32
MLA_QK_LEAD = 3
VMEM_LIMIT = 56 * 1024 * 1024

F32 = jnp.float32
BF16 = jnp.bfloat16
NT_DIMS = (((1,), (1,)), ((), ()))


def _dot(a, b):
    return jnp.dot(a, b, preferred_element_type=F32)


def _dot_nt(a, b):
    return lax.dot_general(a, b, NT_DIMS, preferred_element_type=F32)


def _mod_kernel(cond_ref, w_ref, b_ref, o_ref):
    cnd = cond_ref[...]
    act = cnd * jax.nn.sigmoid(cnd)
    o_ref[...] = _dot(act.astype(BF16), w_ref[...].astype(BF16)) + b_ref[...]


def _modulation(cond, w_mod, b_mod):
    n = w_mod.shape[1]
    tn = 1024
    return pl.pallas_call(
        _mod_kernel,
        grid=(n // tn,),
        in_specs=[
            pl.BlockSpec((8, D_MODEL), lambda i: (0, 0)),
            pl.BlockSpec((D_MODEL, tn), lambda i: (0, i)),
            pl.BlockSpec((1, tn), lambda i: (0, i)),
        ],
        out_specs=pl.BlockSpec((8, tn), lambda i: (0, i)),
        out_shape=jax.ShapeDtypeStruct((8, n), F32),
        name="modulation",
    )(cond, w_mod, b_mod)


def _rms_rows(x, n):
    return x * lax.rsqrt(jnp.sum(x * x, axis=-1, keepdims=True) * (1.0 / n) + EPS)


def _rms_cols(x, n):
    return x * lax.rsqrt(jnp.sum(x * x, axis=0, keepdims=True) * (1.0 / n) + EPS)


def _proj_kernel(*refs, latent):
    if latent:
        (x_ref, mod_ref, gattn_ref, wtok_ref, wvs_ref, gkva_ref, wuk_ref, wuv_ref, gmk_ref, gsk_ref,
         wqs_ref, gqa_ref, wuq_ref, gmq_ref, gsq_ref,
         mcos_ref, msin_ref, scos_ref, ssin_ref, kc_ref, ks1_ref, ks2_ref, sc_ref, ss1_ref, ss2_ref,
         km_ref, vmt_ref, ks_ref, vst_ref, qmt_ref, qst_ref) = refs
    else:
        (x_ref, mod_ref, gattn_ref, wtok_ref, wvs_ref, gkva_ref, wuk_ref, wuv_ref, gmk_ref, gsk_ref,
         km_ref, vmt_ref, ks_ref, vst_ref) = refs

    tile = x_ref.shape[1]

    x = x_ref[0]
    gain = gattn_ref[...] * (1.0 + mod_ref[0, 1:2, :])
    h = (_rms_rows(x, D_MODEL) * gain + mod_ref[0, 0:1, :]).astype(BF16)

    zt = _dot(h, wtok_ref[...])
    cq = zt[:, 0:256]
    ckv = zt[:, 256:384]
    krp = zt[:, 384:512]
    ksw = zt[:, 512:640]

    ckvn = (_rms_rows(ckv, KV_LORA) * gkva_ref[...]).astype(BF16)
    kpre = _dot(ckvn, wuk_ref[...])
    vt = _dot_nt(wuv_ref[...], ckvn)
    gmk = gmk_ref[...]
    kr = krp * gmk
    if latent:
        kr = kr * kc_ref[...] + pltpu.roll(kr, 112, 1) * ks1_ref[...] + pltpu.roll(kr, 16, 1) * ks2_ref[...]
    ss_kr = jnp.sum(krp * krp, axis=-1, keepdims=True)
    pad_rows = MLA_V_AUG - MLA_V
    ones_row = (lax.broadcasted_iota(jnp.int32, (pad_rows, tile), 0) == 0).astype(BF16)
    for hd in range(MLA_HEADS):
        kp = kpre[:, hd * LANE:(hd + 1) * LANE]
        ss = jnp.sum(kp * kp, axis=-1, keepdims=True) + ss_kr
        kn = (kp * gmk + kr) * lax.rsqrt(ss * (1.0 / MLA_QK) + EPS)
        km_ref[0, hd] = kn.astype(BF16)
        vmt_ref[0, hd, 0:MLA_V, :] = vt[hd * MLA_V:(hd + 1) * MLA_V].astype(BF16)
        vmt_ref[0, hd, MLA_V:MLA_V_AUG, :] = ones_row

    lane = lax.broadcasted_iota(jnp.int32, ksw.shape, 1)
    first = lane < SWA_HD
    sq = ksw * ksw
    ss0 = jnp.sum(jnp.where(first, sq, 0.0), axis=-1, keepdims=True)
    ss1 = jnp.sum(jnp.where(first, 0.0, sq), axis=-1, keepdims=True)
    rr = jnp.where(first, lax.rsqrt(ss0 * (1.0 / SWA_HD) + EPS), lax.rsqrt(ss1 * (1.0 / SWA_HD) + EPS))
    ksn = ksw * rr * gsk_ref[...]
    if latent:
        ksn = (ksn * sc_ref[...] + pltpu.roll(ksn, 96, 1) * ss1_ref[...]
               + pltpu.roll(ksn, 32, 1) * ss2_ref[...])
    ks_ref[0] = ksn.astype(BF16)
    vst = _dot_nt(wvs_ref[...], h)
    for g in range(SWA_KV_HEADS):
        for t in range(tile // LANE):
            vst_ref[0, g, t, 0:SWA_HD, :] = vst[g * SWA_HD:(g + 1) * SWA_HD, t * LANE:(t + 1) * LANE].astype(BF16)
            vst_ref[0, g, t, SWA_HD:SWA_V_AUG, :] = ones_row[:, 0:LANE]

    if not latent:
        return

    cqn = (_rms_rows(cq, Q_LORA) * gqa_ref[...]).astype(BF16)
    qt = _dot_nt(wuq_ref[...], cqn)
    gmq = gmq_ref[...] * MLA_QSCALE
    mcos = mcos_ref[...]
    msin = msin_ref[...]
    for hd in range(MLA_HEADS):
        qn = _rms_cols(qt[hd * LANE:(hd + 1) * LANE], MLA_QK) * gmq
        x1 = qn[64:80]
        x2 = qn[80:96]
        qmt_ref[0, hd, 0:64, :] = qn[0:64].astype(BF16)
        qmt_ref[0, hd, 64:80, :] = (x1 * mcos - x2 * msin).astype(BF16)
        qmt_ref[0, hd, 80:96, :] = (x2 * mcos + x1 * msin).astype(BF16)
        qmt_ref[0, hd, 96:128, :] = jnp.zeros((32, tile), BF16)

    qst = _dot_nt(wqs_ref[...], h)
    gsq = gsq_ref[...] * SWA_QSCALE
    scos = scos_ref[...]
    ssin = ssin_ref[...]
    for hd in range(SWA_HEADS):
        qn = _rms_cols(qst[hd * SWA_HD:(hd + 1) * SWA_HD], SWA_HD) * gsq
        x1 = qn[0:32]
        x2 = qn[32:64]
        g = hd // SWA_GROUP
        base = g * SWA_HD
        other = (1 - g) * SWA_HD
        qst_ref[0, hd, base:base + 32, :] = (x1 * scos - x2 * ssin).astype(BF16)
        qst_ref[0, hd, base + 32:base + 64, :] = (x2 * scos + x1 * ssin).astype(BF16)
        qst_ref[0, hd, other:other + 64, :] = jnp.zeros((64, tile), BF16)


def _full(shape):
    nd = len(shape)
    return pl.BlockSpec(shape, lambda b, j: (0,) * nd)


def _project(x, mod3, mod_row, weights, tables, latent, tile):
    bsz, n, _ = x.shape
    nt = n // tile
    if mod_row is None:
        mod_map = lambda b, j: (b, 0, 0)
    else:
        mod_map = lambda b, j: (mod_row, 0, 0)
    common = [weights[k] for k in ("gattn", "wtok", "wvs", "gkva", "wuk", "wuv", "gmk", "gsk")]
    ins = [x, mod3] + common
    in_specs = [
        pl.BlockSpec((1, tile, D_MODEL), lambda b, j: (b, j, 0)),
        pl.BlockSpec((1, 6, D_MODEL), mod_map),
    ] + [_full(w.shape) for w in common]
    out_shape = [
        jax.ShapeDtypeStruct((bsz, MLA_HEADS, n, LANE), BF16),
        jax.ShapeDtypeStruct((bsz, MLA_HEADS, MLA_V_AUG, n), BF16),
        jax.ShapeDtypeStruct((bsz, n, LANE), BF16),
        jax.ShapeDtypeStruct((bsz, SWA_KV_HEADS, n // LANE, SWA_V_AUG, LANE), BF16),
    ]
    out_specs = [
        pl.BlockSpec((1, MLA_HEADS, tile, LANE), lambda b, j: (b, 0, j, 0)),
        pl.BlockSpec((1, MLA_HEADS, MLA_V_AUG, tile), lambda b, j: (b, 0, 0, j)),
        pl.BlockSpec((1, tile, LANE), lambda b, j: (b, j, 0)),
        pl.BlockSpec((1, SWA_KV_HEADS, tile // LANE, SWA_V_AUG, LANE), lambda b, j: (b, 0, j, 0, 0)),
    ]
    if latent:
        extra = [weights[k] for k in ("wqs", "gqa", "wuq", "gmq", "gsq")]
        ins += extra
        in_specs += [_full(w.shape) for w in extra]
        feat = [tables[k] for k in ("mcos", "msin", "scos", "ssin")]
        ins += feat
        in_specs += [pl.BlockSpec((t.shape[0], tile), lambda b, j: (0, j)) for t in feat]
        tok = [tables[k] for k in ("kc", "ks1", "ks2", "sc", "ss1", "ss2")]
        ins += tok
        in_specs += [pl.BlockSpec((tile, LANE), lambda b, j: (j, 0)) for _ in tok]
        out_shape += [
            jax.ShapeDtypeStruct((bsz, MLA_HEADS, LANE, n), BF16),
            jax.ShapeDtypeStruct((bsz, SWA_HEADS, LANE, n), BF16),
        ]
        out_specs += [
            pl.BlockSpec((1, MLA_HEADS, LANE, tile), lambda b, j: (b, 0, 0, j)),
            pl.BlockSpec((1, SWA_HEADS, LANE, tile), lambda b, j: (b, 0, 0, j)),
        ]
    return pl.pallas_call(
        functools.partial(_proj_kernel, latent=latent),
        grid=(bsz, nt),
        in_specs=in_specs,
        out_specs=out_specs,
        out_shape=out_shape,
        compiler_params=pltpu.CompilerParams(vmem_limit_bytes=VMEM_LIMIT),
        name="project_latent" if latent else "project_context",
    )(*ins)


def _mla_kernel(qt_ref, kc_ref, kl_ref, vct_ref, vlt_ref, o_ref, ot_scr, s_scr, p_scr, *, seq):
    n_ctx = kc_ref.shape[2]
    tq = qt_ref.shape[3]
    subs = ([(kc_ref, vct_ref, o) for o in range(0, n_ctx, MLA_SUB)]
            + [(kl_ref, vlt_ref, o) for o in range(0, seq, MLA_SUB)])
    n_sub = len(subs)

    def scores_into(slot, j, nxt, m8):
        q_ref, hd = nxt
        k_ref, _, off = subs[j]
        sj = _dot(k_ref[0, hd, off:off + MLA_SUB, :], q_ref[0, hd])
        s_scr[slot, j * MLA_SUB:(j + 1) * MLA_SUB, :] = sj
        mj = jnp.max(sj.reshape(MLA_SUB // 8, 8, tq), axis=0)
        return mj if m8 is None else jnp.maximum(m8, mj)

    def head_body(hd, m8_cur, slot, nxt):
        mb = jnp.broadcast_to(jnp.max(m8_cur, axis=0, keepdims=True), (MLA_CHUNK, tq))
        n_q = n_sub if nxt is not None else 0
        m8_next = None
        acc = None
        for j in range(min(MLA_QK_LEAD, n_q)):
            m8_next = scores_into(1 - slot, j, nxt, m8_next)
        for j in range(n_sub):
            base = j * MLA_SUB
            for r in range(base, base + MLA_SUB, MLA_CHUNK):
                x = s_scr[slot, r:r + MLA_CHUNK, :] - mb
                p_scr[r:r + MLA_CHUNK, :] = jnp.exp2(x).astype(BF16)
            _, v_ref, off = subs[j]
            d = _dot(v_ref[0, hd, :, off:off + MLA_SUB], p_scr[base:base + MLA_SUB, :])
            acc = d if acc is None else acc + d
            if j + MLA_QK_LEAD < n_q:
                m8_next = scores_into(1 - slot, j + MLA_QK_LEAD, nxt, m8_next)
        row = pl.multiple_of(hd * MLA_V, MLA_V)
        ot_scr[pl.ds(row, MLA_V), :] = acc[0:MLA_V] / acc[MLA_V:MLA_V + 1]
        return m8_next

    def head_step(hd, m8):
        return lax.cond(lax.rem(hd, 2) == 0,
                        lambda m: head_body(hd, m, 0, (qt_ref, hd + 1)),
                        lambda m: head_body(hd, m, 1, (qt_ref, hd + 1)), m8)

    last = MLA_HEADS - 1
    m8_first = None
    for j in range(n_sub):
        m8_first = scores_into(0, j, (qt_ref, 0), m8_first)
    m8_last = lax.fori_loop(0, last, head_step, m8_first)
    head_body(last, m8_last, last % 2, None)
    o_ref[0] = ot_scr[...].T.astype(BF16)


def _mla_attention(qmt, km_c, km_l, vmt_c, vmt_l):
    bsz, _, _, s = qmt.shape
    n_ctx = km_c.shape[2]
    nt = s // TOK_TILE
    n_keys = n_ctx + s
    return pl.pallas_call(
        functools.partial(_mla_kernel, seq=s),
        grid=(bsz, nt),
        in_specs=[
            pl.BlockSpec((1, MLA_HEADS, LANE, TOK_TILE), lambda b, j: (b, 0, 0, j)),
            pl.BlockSpec((1, MLA_HEADS, n_ctx, LANE), lambda b, j: (b, 0, 0, 0)),
            pl.BlockSpec((1, MLA_HEADS, s, LANE), lambda b, j: (b, 0, 0, 0)),
            pl.BlockSpec((1, MLA_HEADS, MLA_V_AUG, n_ctx), lambda b, j: (b, 0, 0, 0)),
            pl.BlockSpec((1, MLA_HEADS, MLA_V_AUG, s), lambda b, j: (b, 0, 0, 0)),
        ],
        out_specs=pl.BlockSpec((1, TOK_TILE, MLA_HEADS * MLA_V), lambda b, j: (b, j, 0)),
        out_shape=jax.ShapeDtypeStruct((bsz, s, MLA_HEADS * MLA_V), BF16),
        scratch_shapes=[
            pltpu.VMEM((MLA_HEADS * MLA_V, TOK_TILE), F32),
            pltpu.VMEM((2, n_keys, TOK_TILE), F32),
            pltpu.VMEM((n_keys, TOK_TILE), BF16),
        ],
        compiler_params=pltpu.CompilerParams(vmem_limit_bytes=VMEM_LIMIT),
        name="mla_attention",
    )(qmt, km_c, km_l, vmt_c, vmt_l)


def _swa_kernel(sink_ref, bias_ref, qt_ref, qtn_ref, kc_ref, kl_ref, vct_ref, vlt_ref, o_ref,
                ot_scr, s_scr, p_scr, m8_scr, *, seq):
    j = pl.program_id(1)
    nt = pl.num_programs(1)
    n_ctx = kc_ref.shape[1]
    tq = qt_ref.shape[3]
    n_band = SWA_WIN // SWA_SUB
    n_t = n_band + n_ctx // SWA_SUB
    lanes_per_sub = SWA_SUB // LANE
    subs = [(hh, t) for hh in range(SWA_GROUP) for t in range(n_t)]

    def window(jj):
        start = pl.multiple_of(jnp.clip(jj * TOK_TILE - WINDOW, 0, seq - SWA_WIN), LANE)
        variant = jnp.where(jj == 0, 0, jnp.where(jj == nt - 1, 2, 1))
        return start, variant

    def rows_of(idx):
        return slice(idx * SWA_SUB, (idx + 1) * SWA_SUB)

    def scores_into(slot, idx, q_ref, g, win):
        hh, t = subs[idx]
        start, variant = win
        if t < n_band:
            k = kl_ref[0, pl.ds(start + t * SWA_SUB, SWA_SUB), :]
        else:
            k = kc_ref[0, (t - n_band) * SWA_SUB:(t - n_band + 1) * SWA_SUB, :]
        sj = _dot(k, q_ref[0, g * SWA_GROUP + hh])
        if t < n_band:
            sj = sj + bias_ref[variant, t * SWA_SUB:(t + 1) * SWA_SUB, :]
        s_scr[slot, rows_of(idx), :] = sj
        return jnp.max(sj.reshape(SWA_SUB // 8, 8, tq), axis=0)

    def values_t(g, t, t0):
        if t < n_band:
            tiles = [vlt_ref[0, g, t0 + t * lanes_per_sub + i] for i in range(lanes_per_sub)]
        else:
            tiles = [vct_ref[0, g, (t - n_band) * lanes_per_sub + i] for i in range(lanes_per_sub)]
        return jnp.concatenate(tiles, axis=1)

    def next_scores(slot, idx, nxt, m8n):
        q_ref, g_n, win_n = nxt
        mj = scores_into(slot, idx, q_ref, g_n, win_n)
        hh, t = subs[idx]
        m8n[hh] = mj if t == 0 else jnp.maximum(m8n[hh], mj)
        if t == n_t - 1:
            m8_scr[slot, hh] = m8n[hh]

    def stage_body(g, win_cur, nxt):
        slot = g
        t0 = win_cur[0] // LANE
        n_q = len(subs) if nxt is not None else 0
        m8n = {}
        for idx in range(min(SWA_QK_LEAD, n_q)):
            next_scores(1 - slot, idx, nxt, m8n)
        for idx, (hh, t) in enumerate(subs):
            hd = g * SWA_GROUP + hh
            if t == 0:
                sk = sink_ref[hd] * LOG2E
                m = jnp.maximum(jnp.max(m8_scr[slot, hh], axis=0, keepdims=True), sk)
                mb = jnp.broadcast_to(m, (SWA_CHUNK, tq))
                acc = None
            base = idx * SWA_SUB
            for r in range(base, base + SWA_SUB, SWA_CHUNK):
                x = s_scr[slot, r:r + SWA_CHUNK, :] - mb
                p_scr[r:r + SWA_CHUNK, :] = jnp.exp2(x).astype(BF16)
            d = _dot(values_t(g, t, t0), p_scr[base:base + SWA_SUB, :])
            acc = d if acc is None else acc + d
            if t == n_t - 1:
                denom = acc[SWA_HD:SWA_HD + 1] + jnp.exp2(sk - m)
                ot_scr[hd * SWA_HD:(hd + 1) * SWA_HD, :] = acc[0:SWA_HD] / denom
            if idx + SWA_QK_LEAD < n_q:
                next_scores(1 - slot, idx + SWA_QK_LEAD, nxt, m8n)

    win = window(j)

    @pl.when(j == 0)
    def _():
        m8n = {}
        for idx in range(len(subs)):
            next_scores(0, idx, (qt_ref, 0, win), m8n)

    stage_body(0, win, (qt_ref, 1, win))

    @pl.when(j < nt - 1)
    def _():
        stage_body(1, win, (qtn_ref, 0, window(j + 1)))

    @pl.when(j == nt - 1)
    def _():
        stage_body(1, win, None)

    o_ref[0] = ot_scr[...].T.astype(BF16)


def _swa_bias(seq):
    nt = seq // TOK_TILE
    out = []
    for j in (0, 1, nt - 1):
        start = min(max(j * TOK_TILE - WINDOW, 0), seq - SWA_WIN)
        qpos = j * TOK_TILE + np.arange(TOK_TILE)[None, :]
        kpos = start + np.arange(SWA_WIN)[:, None]
        out.append(np.where(np.abs(qpos - kpos) <= WINDOW, 0.0, NEG_INF))
    return jnp.asarray(np.stack(out), F32)


def _swa_attention(sink, qst, ks_c, ks_l, vst_c, vst_l):
    bsz, _, _, s = qst.shape
    n_ctx = ks_c.shape[1]
    nt = s // TOK_TILE
    assert nt >= 3 and s >= SWA_WIN and n_ctx % SWA_SUB == 0
    stage_rows = SWA_GROUP * (SWA_WIN + n_ctx)
    return pl.pallas_call(
        functools.partial(_swa_kernel, seq=s),
        grid=(bsz, nt),
        in_specs=[
            pl.BlockSpec(memory_space=pltpu.SMEM),
            pl.BlockSpec((3, SWA_WIN, TOK_TILE), lambda b, j: (0, 0, 0)),
            pl.BlockSpec((1, SWA_HEADS, LANE, TOK_TILE), lambda b, j: (b, 0, 0, j)),
            pl.BlockSpec((1, SWA_GROUP, LANE, TOK_TILE), lambda b, j: (b, 0, 0, jnp.minimum(j + 1, nt - 1))),
            pl.BlockSpec((1, n_ctx, LANE), lambda b, j: (b, 0, 0)),
            pl.BlockSpec((1, s, LANE), lambda b, j: (b, 0, 0)),
            pl.BlockSpec((1, SWA_KV_HEADS, n_ctx // LANE, SWA_V_AUG, LANE), lambda b, j: (b, 0, 0, 0, 0)),
            pl.BlockSpec((1, SWA_KV_HEADS, s // LANE, SWA_V_AUG, LANE), lambda b, j: (b, 0, 0, 0, 0)),
        ],
        out_specs=pl.BlockSpec((1, TOK_TILE, SWA_HEADS * SWA_HD), lambda b, j: (b, j, 0)),
        out_shape=jax.ShapeDtypeStruct((bsz, s, SWA_HEADS * SWA_HD), BF16),
        scratch_shapes=[
            pltpu.VMEM((SWA_HEADS * SWA_HD, TOK_TILE), F32),
            pltpu.VMEM((2, stage_rows, TOK_TILE), F32),
            pltpu.VMEM((stage_rows, TOK_TILE), BF16),
            pltpu.VMEM((2, SWA_GROUP, 8, TOK_TILE), F32),
        ],
        compiler_params=pltpu.CompilerParams(vmem_limit_bytes=VMEM_LIMIT,
                                             dimension_semantics=("arbitrary", "arbitrary")),
        name="swa_attention",
    )(sink, _swa_bias(s), qst, qst, ks_c, ks_l, vst_c, vst_l)


def _mlp_kernel(x_ref, mm_ref, ms_ref, mod_ref, wo_ref, gmlp_ref, w1_ref, w2_ref, o_ref, acc_ref):
    half = MLA_HEADS * MLA_V
    y1 = _dot(mm_ref[0], wo_ref[0:half, :]) + _dot(ms_ref[0], wo_ref[half:, :])
    x1 = x_ref[0] + mod_ref[0, 2:3, :] * y1
    h2 = (_rms_rows(x1, D_MODEL) * gmlp_ref[...] * (1.0 + mod_ref[0, 4:5, :])
          + mod_ref[0, 3:4, :]).astype(BF16)
    for c in range(D_FF // FF_CHUNK):
        a = jnp.maximum(_dot(h2, w1_ref[:, c * FF_CHUNK:(c + 1) * FF_CHUNK]), 0.0)
        part = _dot((a * a).astype(BF16), w2_ref[c * FF_CHUNK:(c + 1) * FF_CHUNK, :])
        if c == 0:
            acc_ref[...] = part
        else:
            acc_ref[...] += part
    o_ref[0] = x1 + mod_ref[0, 5:6, :] * acc_ref[...]


def _outproj_mlp(x, mix_m, mix_s, mod3, w_out, g_mlp, w1, w2):
    bsz, s, _ = x.shape
    half = MLA_HEADS * MLA_V
    const = lambda b, j: (0, 0)
    single = pl.Buffered(1)
    return pl.pallas_call(
        _mlp_kernel,
        grid=(bsz, s // MLP_TILE),
        in_specs=[
            pl.BlockSpec((1, MLP_TILE, D_MODEL), lambda b, j: (b, j, 0)),
            pl.BlockSpec((1, MLP_TILE, half), lambda b, j: (b, j, 0)),
            pl.BlockSpec((1, MLP_TILE, half), lambda b, j: (b, j, 0)),
            pl.BlockSpec((1, 6, D_MODEL), lambda b, j: (b, 0, 0)),
            pl.BlockSpec((D_MODEL, D_MODEL), const, pipeline_mode=single),
            pl.BlockSpec((1, D_MODEL), const),
            pl.BlockSpec((D_MODEL, D_FF), const, pipeline_mode=single),
            pl.BlockSpec((D_FF, D_MODEL), const, pipeline_mode=single),
        ],
        out_specs=pl.BlockSpec((1, MLP_TILE, D_MODEL), lambda b, j: (b, j, 0)),
        out_shape=jax.ShapeDtypeStruct((bsz, s, D_MODEL), F32),
        scratch_shapes=[pltpu.VMEM((MLP_TILE, D_MODEL), F32)],
        compiler_params=pltpu.CompilerParams(vmem_limit_bytes=VMEM_LIMIT),
        name="outproj_mlp",
    )(x, mix_m, mix_s, mod3, w_out, g_mlp, w1, w2)


def _rope_tables(seq):
    f32 = np.float32

    def cos_sin(rot_dim):
        n_freq = rot_dim // 4
        inv = f32(ROPE_THETA) ** (-np.arange(n_freq, dtype=f32) / f32(n_freq))
        rows = seq // GRID_W
        row = np.repeat(np.arange(rows, dtype=f32), GRID_W)
        col = np.tile(np.arange(GRID_W, dtype=f32), rows)
        ang = np.concatenate([row[:, None] * inv, col[:, None] * inv], axis=-1).astype(f32)
        return np.cos(ang).astype(f32), np.sin(ang).astype(f32)

    mcos, msin = cos_sin(MLA_ROPE)
    scos, ssin = cos_sin(SWA_HD)
    z16 = np.zeros((seq, 16), f32)
    z32 = np.zeros((seq, 32), f32)
    z64 = np.zeros((seq, 64), f32)
    kc = np.concatenate([np.ones((seq, 64), f32), mcos, mcos, np.ones((seq, 32), f32)], axis=1)
    ks1 = np.concatenate([z64, -msin, z16, z32], axis=1)
    ks2 = np.concatenate([z64, z16, msin, z32], axis=1)
    sc = np.concatenate([scos, scos, scos, scos], axis=1)
    ss1 = np.concatenate([-ssin, z32, -ssin, z32], axis=1)
    ss2 = np.concatenate([z32, ssin, z32, ssin], axis=1)
    tabs = dict(mcos=mcos.T, msin=msin.T, scos=scos.T, ssin=ssin.T,
                kc=kc, ks1=ks1, ks2=ks2, sc=sc, ss1=ss1, ss2=ss2)
    return {k: jnp.asarray(np.ascontiguousarray(v)) for k, v in tabs.items()}


def _prep_weights(g_attn, w_in, g_q_a, w_uq, g_kv_a, w_ukv, g_mla_q, g_mla_k, g_swa_q, g_swa_k):
    o_ckv = Q_LORA
    o_kr = o_ckv + KV_LORA
    o_qs = o_kr + MLA_ROPE
    o_ks = o_qs + SWA_HEADS * SWA_HD
    o_vs = o_ks + SWA_KV_HEADS * SWA_HD
    w_cq = w_in[:, :o_ckv]
    w_ckv = w_in[:, o_ckv:o_kr]
    w_kr = w_in[:, o_kr:o_qs]
    w_qs = w_in[:, o_qs:o_ks]
    w_ks = w_in[:, o_ks:o_vs]
    w_vs = w_in[:, o_vs:]
    w_kr_p = jnp.pad(w_kr, ((0, 0), (MLA_NOPE, LANE - MLA_QK)))
    wtok = jnp.concatenate([w_cq, w_ckv, w_kr_p, w_ks], axis=1).astype(BF16)
    w_ukv_h = w_ukv.reshape(KV_LORA, MLA_HEADS, MLA_NOPE + MLA_V)
    wuk = jnp.pad(w_ukv_h[:, :, :MLA_NOPE], ((0, 0), (0, 0), (0, LANE - MLA_NOPE)))
    wuk = wuk.reshape(KV_LORA, MLA_HEADS * LANE).astype(BF16)
    wuv = w_ukv_h[:, :, MLA_NOPE:].reshape(KV_LORA, MLA_HEADS * MLA_V).T.astype(BF16)
    w_uq_h = jnp.pad(w_uq.reshape(Q_LORA, MLA_HEADS, MLA_QK), ((0, 0), (0, 0), (0, LANE - MLA_QK)))
    wuq = w_uq_h.reshape(Q_LORA, MLA_HEADS * LANE).T.astype(BF16)
    return dict(
        gattn=g_attn[None, :], wtok=wtok, wvs=w_vs.T.astype(BF16), gkva=g_kv_a[None, :],
        wuk=wuk, wuv=wuv, gmk=jnp.pad(g_mla_k, (0, LANE - MLA_QK))[None, :],
        gsk=jnp.tile(g_swa_k, SWA_KV_HEADS)[None, :],
        wqs=w_qs.T.astype(BF16), gqa=g_q_a[None, :], wuq=wuq,
        gmq=jnp.pad(g_mla_q, (0, LANE - MLA_QK))[:, None], gsq=g_swa_q[:, None],
    )


def kernel(x, c, ctx, c_ctx, w_mod, b_mod, g_attn, w_in, g_q_a, w_uq, g_kv_a, w_ukv, g_mla_q, g_mla_k,
           g_swa_q, g_swa_k, swa_sink, w_out, g_mlp, w_mlp1, w_mlp2):
    bsz, seq, _ = x.shape
    assert w_mod.shape[0] == 1, "single-layer block"
    assert bsz < 8 and seq % MLP_TILE == 0 and ctx.shape[1] % TOK_TILE == 0

    cond = jnp.zeros((8, D_MODEL), F32).at[:bsz].set(c).at[bsz].set(c_ctx)
    mod = _modulation(cond, w_mod[0], b_mod[0][None, :])
    mod3 = mod.reshape(8, 6, D_MODEL)

    weights = _prep_weights(g_attn[0], w_in[0], g_q_a[0], w_uq[0], g_kv_a[0], w_ukv[0],
                            g_mla_q[0], g_mla_k[0], g_swa_q[0], g_swa_k[0])
    tables = _rope_tables(seq)

    km_l, vmt_l, ks_l, vst_l, qmt, qst = _project(x, mod3, None, weights, tables, latent=True, tile=PROJ_TILE)
    km_c, vmt_c, ks_c, vst_c = _project(ctx, mod3, bsz, weights, None, latent=False, tile=TOK_TILE)

    mix_m = _mla_attention(qmt, km_c, km_l, vmt_c, vmt_l)
    mix_s = _swa_attention(swa_sink[0], qst, ks_c, ks_l, vst_c, vst_l)

    return _outproj_mlp(x, mix_m, mix_s, mod3, w_out[0].astype(BF16), g_mlp[0][None, :],
                        w_mlp1[0].astype(BF16), w_mlp2[0].astype(BF16))
```

```python
import functools
import math

import jax
import jax.numpy as jnp
import numpy as np
from jax import lax
from jax.experimental import pallas as pl
from jax.experimental.pallas import tpu as pltpu

D_MODEL = 1024
GRID_W = 64
MLA_HEADS = 8
MLA_NOPE = 64
MLA_ROPE = 32
MLA_QK = MLA_NOPE + MLA_ROPE
MLA_V = 64
MLA_V_AUG = MLA_V + 16
Q_LORA = 256
KV_LORA = 128
SWA_HEADS = 8
SWA_KV_HEADS = 2
SWA_GROUP = SWA_HEADS // SWA_KV_HEADS
SWA_HD = 64
SWA_V_AUG = SWA_HD + 16
WINDOW = 128
D_FF = 4 * D_MODEL
ROPE_THETA = 10000.0
EPS = 1e-6
NEG_INF = -1e30
LOG2E = 1.4426950408889634
MLA_QSCALE = LOG2E / math.sqrt(MLA_QK)
SWA_QSCALE = LOG2E / math.sqrt(SWA_HD)

LANE = 128
TOK_TILE = 256
PROJ_TILE = 1024
PROJ_PART = 256
MLP_TILE = 512
FF_CHUNK = 1024
SWA_WIN = 512
MLA_SUB = 256
SWA_SUB = 256
SWA_CHUNK = 32
SWA_QK_LEAD = 1
MLA_CHUNK = 32
MLA_QK_LEAD = 3
VMEM_LIMIT = 56 * 1024 * 1024

F32 = jnp.float32
BF16 = jnp.bfloat16
NT_DIMS = (((1,), (1,)), ((), ()))


def _dot(a, b):
    return jnp.dot(a, b, preferred_element_type=F32)


def _dot_nt(a, b):
    return lax.dot_general(a, b, NT_DIMS, preferred_element_type=F32)


def _mod_kernel(cond_ref, w_ref, b_ref, o_ref):
    cnd = cond_ref[...]
    act = cnd * jax.nn.sigmoid(cnd)
    o_ref[...] = _dot(act.astype(BF16), w_ref[...].astype(BF16)) + b_ref[...]


def _modulation(cond, w_mod, b_mod):
    n = w_mod.shape[1]
    tn = 1024
    return pl.pallas_call(
        _mod_kernel,
        grid=(n // tn,),
        in_specs=[
            pl.BlockSpec((8, D_MODEL), lambda i: (0, 0)),
            pl.BlockSpec((D_MODEL, tn), lambda i: (0, i)),
            pl.BlockSpec((1, tn), lambda i: (0, i)),
        ],
        out_specs=pl.BlockSpec((8, tn), lambda i: (0, i)),
        out_shape=jax.ShapeDtypeStruct((8, n), F32),
        name="modulation",
    )(cond, w_mod, b_mod)


def _rms_rows(x, n):
    return x * lax.rsqrt(jnp.sum(x * x, axis=-1, keepdims=True) * (1.0 / n) + EPS)


def _rms_cols(x, n):
    return x * lax.rsqrt(jnp.sum(x * x, axis=0, keepdims=True) * (1.0 / n) + EPS)


def _proj_kernel(*refs, latent):
    if latent:
        (x_ref, mod_ref, gattn_ref, wtok_ref, wvs_ref, gkva_ref, wuk_ref, wuv_ref, gmk_ref, gsk_ref,
         wqs_ref, gqa_ref, wuq_ref, gmq_ref, gsq_ref,
         mcos_ref, msin_ref, scos_ref, ssin_ref, kc_ref, ks1_ref, ks2_ref, sc_ref, ss1_ref, ss2_ref,
         km_ref, vmt_ref, ks_ref, vst_ref, qmt_ref, qst_ref) = refs
    else:
        (x_ref, mod_ref, gattn_ref, wtok_ref, wvs_ref, gkva_ref, wuk_ref, wuv_ref, gmk_ref, gsk_ref,
         km_ref, vmt_ref, ks_ref, vst_ref) = refs

    tile = x_ref.shape[1]
    n_part = max(1, tile // PROJ_PART)
    part = tile // n_part

    gain = gattn_ref[...] * (1.0 + mod_ref[0, 1:2, :])
    shift = mod_ref[0, 0:1, :]
    gmk = gmk_ref[...]
    ones_row = (lax.broadcasted_iota(jnp.int32, (MLA_V_AUG - MLA_V, part), 0) == 0).astype(BF16)
    if latent:
        gmq = gmq_ref[...] * MLA_QSCALE
        gsq = gsq_ref[...] * SWA_QSCALE

    def part_phases(lo):
        rows = slice(lo, lo + part)

        h = (_rms_rows(x_ref[0, rows, :], D_MODEL) * gain + shift).astype(BF16)
        yield

        zt = _dot(h, wtok_ref[...])
        vst = _dot_nt(wvs_ref[...], h)
        if latent:
            qst = _dot_nt(wqs_ref[...], h)
        yield

        cq = zt[:, 0:256]
        ckv = zt[:, 256:384]
        krp = zt[:, 384:512]
        ksw = zt[:, 512:640]
        ckvn = (_rms_rows(ckv, KV_LORA) * gkva_ref[...]).astype(BF16)
        if latent:
            cqn = (_rms_rows(cq, Q_LORA) * gqa_ref[...]).astype(BF16)
        kr = krp * gmk
        if latent:
            kr = (kr * kc_ref[rows, :] + pltpu.roll(kr, 112, 1) * ks1_ref[rows, :]
                  + pltpu.roll(kr, 16, 1) * ks2_ref[rows, :])
        ss_kr = jnp.sum(krp * krp, axis=-1, keepdims=True)
        first = lax.broadcasted_iota(jnp.int32, ksw.shape, 1) < SWA_HD
        sq = ksw * ksw
        ss0 = jnp.sum(jnp.where(first, sq, 0.0), axis=-1, keepdims=True)
        ss1 = jnp.sum(jnp.where(first, 0.0, sq), axis=-1, keepdims=True)
        rr = jnp.where(first, lax.rsqrt(ss0 * (1.0 / SWA_HD) + EPS), lax.rsqrt(ss1 * (1.0 / SWA_HD) + EPS))
        ksn = ksw * rr * gsk_ref[...]
        if latent:
            ksn = (ksn * sc_ref[rows, :] + pltpu.roll(ksn, 96, 1) * ss1_ref[rows, :]
                   + pltpu.roll(ksn, 32, 1) * ss2_ref[rows, :])
        ks_ref[0, rows, :] = ksn.astype(BF16)
        for g in range(SWA_KV_HEADS):
            for t in range(part // LANE):
                dst = lo // LANE + t
                vst_ref[0, g, dst, 0:SWA_HD, :] = vst[g * SWA_HD:(g + 1) * SWA_HD, t * LANE:(t + 1) * LANE].astype(BF16)
                vst_ref[0, g, dst, SWA_HD:SWA_V_AUG, :] = ones_row[:, 0:LANE]
        yield

        kpre = _dot(ckvn, wuk_ref[...])
        vt = _dot_nt(wuv_ref[...], ckvn)
        if latent:
            qt = _dot_nt(wuq_ref[...], cqn)
        yield

        for hd in range(MLA_HEADS):
            kp = kpre[:, hd * LANE:(hd + 1) * LANE]
            ss = jnp.sum(kp * kp, axis=-1, keepdims=True) + ss_kr
            kn = (kp * gmk + kr) * lax.rsqrt(ss * (1.0 / MLA_QK) + EPS)
            km_ref[0, hd, rows, :] = kn.astype(BF16)
            vmt_ref[0, hd, 0:MLA_V, rows] = vt[hd * MLA_V:(hd + 1) * MLA_V].astype(BF16)
            vmt_ref[0, hd, MLA_V:MLA_V_AUG, rows] = ones_row
        if latent:
            mcos = mcos_ref[:, rows]
            msin = msin_ref[:, rows]
            for hd in range(MLA_HEADS):
                qn = _rms_cols(qt[hd * LANE:(hd + 1) * LANE], MLA_QK) * gmq
                x1 = qn[64:80]
                x2 = qn[80:96]
                qmt_ref[0, hd, 0:64, rows] = qn[0:64].astype(BF16)
                qmt_ref[0, hd, 64:80, rows] = (x1 * mcos - x2 * msin).astype(BF16)
                qmt_ref[0, hd, 80:96, rows] = (x2 * mcos + x1 * msin).astype(BF16)
                qmt_ref[0, hd, 96:128, rows] = jnp.zeros((32, part), BF16)
            scos = scos_ref[:, rows]
            ssin = ssin_ref[:, rows]
            for hd in range(SWA_HEADS):
                qn = _rms_cols(qst[hd * SWA_HD:(hd + 1) * SWA_HD], SWA_HD) * gsq
                x1 = qn[0:32]
                x2 = qn[32:64]
                g = hd // SWA_GROUP
                base = g * SWA_HD
                other = (1 - g) * SWA_HD
                qst_ref[0, hd, base:base + 32, rows] = (x1 * scos - x2 * ssin).astype(BF16)
                qst_ref[0, hd, base + 32:base + 64, rows] = (x2 * scos + x1 * ssin).astype(BF16)
                qst_ref[0, hd, other:other + 64, rows] = jnp.zeros((64, part), BF16)
        yield

    n_phase = 5
    parts = [part_phases(i * part) for i in range(n_part)]
    for step in range(n_phase + n_part - 1):
        for i, gen in enumerate(parts):
            if 0 <= step - i < n_phase:
                next(gen)


def _full(shape):
    nd = len(shape)
    return pl.BlockSpec(shape, lambda b, j: (0,) * nd)


def _project(x, mod3, mod_row, weights, tables, latent, tile):
    bsz, n, _ = x.shape
    nt = n // tile
    if mod_row is None:
        mod_map = lambda b, j: (b, 0, 0)
    else:
        mod_map = lambda b, j: (mod_row, 0, 0)
    common = [weights[k] for k in ("gattn", "wtok", "wvs", "gkva", "wuk", "wuv", "gmk", "gsk")]
    ins = [x, mod3] + common
    in_specs = [
        pl.BlockSpec((1, tile, D_MODEL), lambda b, j: (b, j, 0)),
        pl.BlockSpec((1, 6, D_MODEL), mod_map),
    ] + [_full(w.shape) for w in common]
    out_shape = [
        jax.ShapeDtypeStruct((bsz, MLA_HEADS, n, LANE), BF16),
        jax.ShapeDtypeStruct((bsz, MLA_HEADS, MLA_V_AUG, n), BF16),
        jax.ShapeDtypeStruct((bsz, n, LANE), BF16),
        jax.ShapeDtypeStruct((bsz, SWA_KV_HEADS, n // LANE, SWA_V_AUG, LANE), BF16),
    ]
    out_specs = [
        pl.BlockSpec((1, MLA_HEADS, tile, LANE), lambda b, j: (b, 0, j, 0)),
        pl.BlockSpec((1, MLA_HEADS, MLA_V_AUG, tile), lambda b, j: (b, 0, 0, j)),
        pl.BlockSpec((1, tile, LANE), lambda b, j: (b, j, 0)),
        pl.BlockSpec((1, SWA_KV_HEADS, tile // LANE, SWA_V_AUG, LANE), lambda b, j: (b, 0, j, 0, 0)),
    ]
    if latent:
        extra = [weights[k] for k in ("wqs", "gqa", "wuq", "gmq", "gsq")]
        ins += extra
        in_specs += [_full(w.shape) for w in extra]
        feat = [tables[k] for k in ("mcos", "msin", "scos", "ssin")]
        ins += feat
        in_specs += [pl.BlockSpec((t.shape[0], tile), lambda b, j: (0, j)) for t in feat]
        tok = [tables[k] for k in ("kc", "ks1", "ks2", "sc", "ss1", "ss2")]
        ins += tok
        in_specs += [pl.BlockSpec((tile, LANE), lambda b, j: (j, 0)) for _ in tok]
        out_shape += [
            jax.ShapeDtypeStruct((bsz, MLA_HEADS, LANE, n), BF16),
            jax.ShapeDtypeStruct((bsz, SWA_HEADS, LANE, n), BF16),
        ]
        out_specs += [
            pl.BlockSpec((1, MLA_HEADS, LANE, tile), lambda b, j: (b, 0, 0, j)),
            pl.BlockSpec((1, SWA_HEADS, LANE, tile), lambda b, j: (b, 0, 0, j)),
        ]
    return pl.pallas_call(
        functools.partial(_proj_kernel, latent=latent),
        grid=(bsz, nt),
        in_specs=in_specs,
        out_specs=out_specs,
        out_shape=out_shape,
        compiler_params=pltpu.CompilerParams(vmem_limit_bytes=VMEM_LIMIT),
        name="project_latent" if latent else "project_context",
    )(*ins)


def _mla_kernel(qt_ref, kc_ref, kl_ref, vct_ref, vlt_ref, o_ref, ot_scr, s_scr, *, seq):
    n_ctx = kc_ref.shape[2]
    tq = qt_ref.shape[3]
    subs = ([(kc_ref, vct_ref, o) for o in range(0, n_ctx, MLA_SUB)]
            + [(kl_ref, vlt_ref, o) for o in range(0, seq, MLA_SUB)])
    n_sub = len(subs)

    def scores_into(slot, j, nxt, m8):
        q_ref, hd = nxt
        k_ref, _, off = subs[j]
        sj = _dot(k_ref[0, hd, off:off + MLA_SUB, :], q_ref[0, hd])
        s_scr[slot, j * MLA_SUB:(j + 1) * MLA_SUB, :] = sj
        mj = jnp.max(sj.reshape(MLA_SUB // 8, 8, tq), axis=0)
        return mj if m8 is None else jnp.maximum(m8, mj)

    def head_body(hd, m8_cur, slot, nxt):
        mb = jnp.broadcast_to(jnp.max(m8_cur, axis=0, keepdims=True), (MLA_CHUNK, tq))
        n_q = n_sub if nxt is not None else 0
        m8_next = None
        acc = None
        for j in range(min(MLA_QK_LEAD, n_q)):
            m8_next = scores_into(1 - slot, j, nxt, m8_next)
        for j in range(n_sub):
            base = j * MLA_SUB
            p = jnp.concatenate(
                [jnp.exp2(s_scr[slot, r:r + MLA_CHUNK, :] - mb).astype(BF16)
                 for r in range(base, base + MLA_SUB, MLA_CHUNK)], axis=0)
            _, v_ref, off = subs[j]
            d = _dot(v_ref[0, hd, :, off:off + MLA_SUB], p)
            acc = d if acc is None else acc + d
            if j + MLA_QK_LEAD < n_q:
                m8_next = scores_into(1 - slot, j + MLA_QK_LEAD, nxt, m8_next)
        row = pl.multiple_of(hd * MLA_V, MLA_V)
        ot_scr[pl.ds(row, MLA_V), :] = acc[0:MLA_V] / acc[MLA_V:MLA_V + 1]
        return m8_next

    def head_step(hd, m8):
        return lax.cond(lax.rem(hd, 2) == 0,
                        lambda m: head_body(hd, m, 0, (qt_ref, hd + 1)),
                        lambda m: head_body(hd, m, 1, (qt_ref, hd + 1)), m8)

    last = MLA_HEADS - 1
    m8_first = None
    for j in range(n_sub):
        m8_first = scores_into(0, j, (qt_ref, 0), m8_first)
    m8_last = lax.fori_loop(0, last, head_step, m8_first)
    head_body(last, m8_last, last % 2, None)
    o_ref[0] = ot_scr[...].T.astype(BF16)


def _mla_attention(qmt, km_c, km_l, vmt_c, vmt_l):
    bsz, _, _, s = qmt.shape
    n_ctx = km_c.shape[2]
    nt = s // TOK_TILE
    n_keys = n_ctx + s
    return pl.pallas_call(
        functools.partial(_mla_kernel, seq=s),
        grid=(bsz, nt),
        in_specs=[
            pl.BlockSpec((1, MLA_HEADS, LANE, TOK_TILE), lambda b, j: (b, 0, 0, j)),
            pl.BlockSpec((1, MLA_HEADS, n_ctx, LANE), lambda b, j: (b, 0, 0, 0)),
            pl.BlockSpec((1, MLA_HEADS, s, LANE), lambda b, j: (b, 0, 0, 0)),
            pl.BlockSpec((1, MLA_HEADS, MLA_V_AUG, n_ctx), lambda b, j: (b, 0, 0, 0)),
            pl.BlockSpec((1, MLA_HEADS, MLA_V_AUG, s), lambda b, j: (b, 0, 0, 0)),
        ],
        out_specs=pl.BlockSpec((1, TOK_TILE, MLA_HEADS * MLA_V), lambda b, j: (b, j, 0)),
        out_shape=jax.ShapeDtypeStruct((bsz, s, MLA_HEADS * MLA_V), BF16),
        scratch_shapes=[
            pltpu.VMEM((MLA_HEADS * MLA_V, TOK_TILE), F32),
            pltpu.VMEM((2, n_keys, TOK_TILE), F32),
        ],
        compiler_params=pltpu.CompilerParams(vmem_limit_bytes=VMEM_LIMIT,
                                             dimension_semantics=("arbitrary", "arbitrary")),
        name="mla_attention",
    )(qmt, km_c, km_l, vmt_c, vmt_l)


def _swa_kernel(sink_ref, bias_ref, qt_ref, qtn_ref, kc_ref, kl_ref, vct_ref, vlt_ref, o_ref,
                ot_scr, s_scr, m8_scr, *, seq):
    j = pl.program_id(1)
    nt = pl.num_programs(1)
    n_ctx = kc_ref.shape[1]
    tq = qt_ref.shape[3]
    n_band = SWA_WIN // SWA_SUB
    n_t = n_band + n_ctx // SWA_SUB
    lanes_per_sub = SWA_SUB // LANE
    subs = [(hh, t) for hh in range(SWA_GROUP) for t in range(n_t)]

    def window(jj):
        start = pl.multiple_of(jnp.clip(jj * TOK_TILE - WINDOW, 0, seq - SWA_WIN), LANE)
        variant = jnp.where(jj == 0, 0, jnp.where(jj == nt - 1, 2, 1))
        return start, variant

    def rows_of(idx):
        return slice(idx * SWA_SUB, (idx + 1) * SWA_SUB)

    def scores_into(slot, idx, q_ref, g, win):
        hh, t = subs[idx]
        start, variant = win
        if t < n_band:
            k = kl_ref[0, pl.ds(start + t * SWA_SUB, SWA_SUB), :]
        else:
            k = kc_ref[0, (t - n_band) * SWA_SUB:(t - n_band + 1) * SWA_SUB, :]
        sj = _dot(k, q_ref[0, g * SWA_GROUP + hh])
        if t < n_band:
            sj = sj + bias_ref[variant, t * SWA_SUB:(t + 1) * SWA_SUB, :]
        s_scr[slot, rows_of(idx), :] = sj
        return jnp.max(sj.reshape(SWA_SUB // 8, 8, tq), axis=0)

    def values_t(g, t, t0):
        if t < n_band:
            tiles = [vlt_ref[0, g, t0 + t * lanes_per_sub + i] for i in range(lanes_per_sub)]
        else:
            tiles = [vct_ref[0, g, (t - n_band) * lanes_per_sub + i] for i in range(lanes_per_sub)]
        return jnp.concatenate(tiles, axis=1)

    def next_scores(slot, idx, nxt, m8n):
        q_ref, g_n, win_n = nxt
        mj = scores_into(slot, idx, q_ref, g_n, win_n)
        hh, t = subs[idx]
        m8n[hh] = mj if t == 0 else jnp.maximum(m8n[hh], mj)
        if t == n_t - 1:
            m8_scr[slot, hh] = m8n[hh]

    def stage_body(g, win_cur, nxt):
        slot = g
        t0 = win_cur[0] // LANE
        n_q = len(subs) if nxt is not None else 0
        m8n = {}
        for idx in range(min(SWA_QK_LEAD, n_q)):
            next_scores(1 - slot, idx, nxt, m8n)
        for idx, (hh, t) in enumerate(subs):
            hd = g * SWA_GROUP + hh
            if t == 0:
                sk = sink_ref[hd] * LOG2E
                m = jnp.maximum(jnp.max(m8_scr[slot, hh], axis=0, keepdims=True), sk)
                mb = jnp.broadcast_to(m, (SWA_CHUNK, tq))
                acc = None
            base = idx * SWA_SUB
            p = jnp.concatenate(
                [jnp.exp2(s_scr[slot, r:r + SWA_CHUNK, :] - mb).astype(BF16)
                 for r in range(base, base + SWA_SUB, SWA_CHUNK)], axis=0)
            d = _dot(values_t(g, t, t0), p)
            acc = d if acc is None else acc + d
            if t == n_t - 1:
                denom = acc[SWA_HD:SWA_HD + 1] + jnp.exp2(sk - m)
                ot_scr[hd * SWA_HD:(hd + 1) * SWA_HD, :] = acc[0:SWA_HD] / denom
            if idx + SWA_QK_LEAD < n_q:
                next_scores(1 - slot, idx + SWA_QK_LEAD, nxt, m8n)

    win = window(j)

    @pl.when(j == 0)
    def _():
        m8n = {}
        for idx in range(len(subs)):
            next_scores(0, idx, (qt_ref, 0, win), m8n)

    stage_body(0, win, (qt_ref, 1, win))

    @pl.when(j < nt - 1)
    def _():
        stage_body(1, win, (qtn_ref, 0, window(j + 1)))

    @pl.when(j == nt - 1)
    def _():
        stage_body(1, win, None)

    o_ref[0] = ot_scr[...].T.astype(BF16)


def _swa_bias(seq):
    nt = seq // TOK_TILE
    out = []
    for j in (0, 1, nt - 1):
        start = min(max(j * TOK_TILE - WINDOW, 0), seq - SWA_WIN)
        qpos = j * TOK_TILE + np.arange(TOK_TILE)[None, :]
        kpos = start + np.arange(SWA_WIN)[:, None]
        out.append(np.where(np.abs(qpos - kpos) <= WINDOW, 0.0, NEG_INF))
    return jnp.asarray(np.stack(out), F32)


def _swa_attention(sink, qst, ks_c, ks_l, vst_c, vst_l):
    bsz, _, _, s = qst.shape
    n_ctx = ks_c.shape[1]
    nt = s // TOK_TILE
    assert nt >= 3 and s >= SWA_WIN and n_ctx % SWA_SUB == 0
    stage_rows = SWA_GROUP * (SWA_WIN + n_ctx)
    return pl.pallas_call(
        functools.partial(_swa_kernel, seq=s),
        grid=(bsz, nt),
        in_specs=[
            pl.BlockSpec(memory_space=pltpu.SMEM),
            pl.BlockSpec((3, SWA_WIN, TOK_TILE), lambda b, j: (0, 0, 0)),
            pl.BlockSpec((1, SWA_HEADS, LANE, TOK_TILE), lambda b, j: (b, 0, 0, j)),
            pl.BlockSpec((1, SWA_GROUP, LANE, TOK_TILE), lambda b, j: (b, 0, 0, jnp.minimum(j + 1, nt - 1))),
            pl.BlockSpec((1, n_ctx, LANE), lambda b, j: (b, 0, 0)),
            pl.BlockSpec((1, s, LANE), lambda b, j: (b, 0, 0)),
            pl.BlockSpec((1, SWA_KV_HEADS, n_ctx // LANE, SWA_V_AUG, LANE), lambda b, j: (b, 0, 0, 0, 0)),
            pl.BlockSpec((1, SWA_KV_HEADS, s // LANE, SWA_V_AUG, LANE), lambda b, j: (b, 0, 0, 0, 0)),
        ],
        out_specs=pl.BlockSpec((1, TOK_TILE, SWA_HEADS * SWA_HD), lambda b, j: (b, j, 0)),
        out_shape=jax.ShapeDtypeStruct((bsz, s, SWA_HEADS * SWA_HD), BF16),
        scratch_shapes=[
            pltpu.VMEM((SWA_HEADS * SWA_HD, TOK_TILE), F32),
            pltpu.VMEM((2, stage_rows, TOK_TILE), F32),
            pltpu.VMEM((2, SWA_GROUP, 8, TOK_TILE), F32),
        ],
        compiler_params=pltpu.CompilerParams(vmem_limit_bytes=VMEM_LIMIT,
                                             dimension_semantics=("arbitrary", "arbitrary")),
        name="swa_attention",
    )(sink, _swa_bias(s), qst, qst, ks_c, ks_l, vst_c, vst_l)


def _mlp_kernel(x_ref, mm_ref, ms_ref, mod_ref, wo_ref, gmlp_ref, w1_ref, w2_ref, o_ref, acc_ref):
    half = MLA_HEADS * MLA_V
    y1 = _dot(mm_ref[0], wo_ref[0:half, :]) + _dot(ms_ref[0], wo_ref[half:, :])
    x1 = x_ref[0] + mod_ref[0, 2:3, :] * y1
    h2 = (_rms_rows(x1, D_MODEL) * gmlp_ref[...] * (1.0 + mod_ref[0, 4:5, :])
          + mod_ref[0, 3:4, :]).astype(BF16)
    for c in range(D_FF // FF_CHUNK):
        a = jnp.maximum(_dot(h2, w1_ref[:, c * FF_CHUNK:(c + 1) * FF_CHUNK]), 0.0)
        part = _dot((a * a).astype(BF16), w2_ref[c * FF_CHUNK:(c + 1) * FF_CHUNK, :])
        if c == 0:
            acc_ref[...] = part
        else:
            acc_ref[...] += part
    o_ref[0] = x1 + mod_ref[0, 5:6, :] * acc_ref[...]


def _outproj_mlp(x, mix_m, mix_s, mod3, w_out, g_mlp, w1, w2):
    bsz, s, _ = x.shape
    half = MLA_HEADS * MLA_V
    const = lambda b, j: (0, 0)
    single = pl.Buffered(1)
    return pl.pallas_call(
        _mlp_kernel,
        grid=(bsz, s // MLP_TILE),
        in_specs=[
            pl.BlockSpec((1, MLP_TILE, D_MODEL), lambda b, j: (b, j, 0)),
            pl.BlockSpec((1, MLP_TILE, half), lambda b, j: (b, j, 0)),
            pl.BlockSpec((1, MLP_TILE, half), lambda b, j: (b, j, 0)),
            pl.BlockSpec((1, 6, D_MODEL), lambda b, j: (b, 0, 0)),
            pl.BlockSpec((D_MODEL, D_MODEL), const, pipeline_mode=single),
            pl.BlockSpec((1, D_MODEL), const),
            pl.BlockSpec((D_MODEL, D_FF), const, pipeline_mode=single),
            pl.BlockSpec((D_FF, D_MODEL), const, pipeline_mode=single),
        ],
        out_specs=pl.BlockSpec((1, MLP_TILE, D_MODEL), lambda b, j: (b, j, 0)),
        out_shape=jax.ShapeDtypeStruct((bsz, s, D_MODEL), F32),
        scratch_shapes=[pltpu.VMEM((MLP_TILE, D_MODEL), F32)],
        compiler_params=pltpu.CompilerParams(vmem_limit_bytes=VMEM_LIMIT),
        name="outproj_mlp",
    )(x, mix_m, mix_s, mod3, w_out, g_mlp, w1, w2)


def _rope_tables(seq):
    f32 = np.float32

    def cos_sin(rot_dim):
        n_freq = rot_dim // 4
        inv = f32(ROPE_THETA) ** (-np.arange(n_freq, dtype=f32) / f32(n_freq))
        rows = seq // GRID_W
        row = np.repeat(np.arange(rows, dtype=f32), GRID_W)
        col = np.tile(np.arange(GRID_W, dtype=f32), rows)
        ang = np.concatenate([row[:, None] * inv, col[:, None] * inv], axis=-1).astype(f32)
        return np.cos(ang).astype(f32), np.sin(ang).astype(f32)

    mcos, msin = cos_sin(MLA_ROPE)
    scos, ssin = cos_sin(SWA_HD)
    z16 = np.zeros((seq, 16), f32)
    z32 = np.zeros((seq, 32), f32)
    z64 = np.zeros((seq, 64), f32)
    kc = np.concatenate([np.ones((seq, 64), f32), mcos, mcos, np.ones((seq, 32), f32)], axis=1)
    ks1 = np.concatenate([z64, -msin, z16, z32], axis=1)
    ks2 = np.concatenate([z64, z16, msin, z32], axis=1)
    sc = np.concatenate([scos, scos, scos, scos], axis=1)
    ss1 = np.concatenate([-ssin, z32, -ssin, z32], axis=1)
    ss2 = np.concatenate([z32, ssin, z32, ssin], axis=1)
    tabs = dict(mcos=mcos.T, msin=msin.T, scos=scos.T, ssin=ssin.T,
                kc=kc, ks1=ks1, ks2=ks2, sc=sc, ss1=ss1, ss2=ss2)
    return {k: jnp.asarray(np.ascontiguousarray(v)) for k, v in tabs.items()}


def _prep_weights(g_attn, w_in, g_q_a, w_uq, g_kv_a, w_ukv, g_mla_q, g_mla_k, g_swa_q, g_swa_k):
    o_ckv = Q_LORA
    o_kr = o_ckv + KV_LORA
    o_qs = o_kr + MLA_ROPE
    o_ks = o_qs + SWA_HEADS * SWA_HD
    o_vs = o_ks + SWA_KV_HEADS * SWA_HD
    w_cq = w_in[:, :o_ckv]
    w_ckv = w_in[:, o_ckv:o_kr]
    w_kr = w_in[:, o_kr:o_qs]
    w_qs = w_in[:, o_qs:o_ks]
    w_ks = w_in[:, o_ks:o_vs]
    w_vs = w_in[:, o_vs:]
    w_kr_p = jnp.pad(w_kr, ((0, 0), (MLA_NOPE, LANE - MLA_QK)))
    wtok = jnp.concatenate([w_cq, w_ckv, w_kr_p, w_ks], axis=1).astype(BF16)
    w_ukv_h = w_ukv.reshape(KV_LORA, MLA_HEADS, MLA_NOPE + MLA_V)
    wuk = jnp.pad(w_ukv_h[:, :, :MLA_NOPE], ((0, 0), (0, 0), (0, LANE - MLA_NOPE)))
    wuk = wuk.reshape(KV_LORA, MLA_HEADS * LANE).astype(BF16)
    wuv = w_ukv_h[:, :, MLA_NOPE:].reshape(KV_LORA, MLA_HEADS * MLA_V).T.astype(BF16)
    w_uq_h = jnp.pad(w_uq.reshape(Q_LORA, MLA_HEADS, MLA_QK), ((0, 0), (0, 0), (0, LANE - MLA_QK)))
    wuq = w_uq_h.reshape(Q_LORA, MLA_HEADS * LANE).T.astype(BF16)
    return dict(
        gattn=g_attn[None, :], wtok=wtok, wvs=w_vs.T.astype(BF16), gkva=g_kv_a[None, :],
        wuk=wuk, wuv=wuv, gmk=jnp.pad(g_mla_k, (0, LANE - MLA_QK))[None, :],
        gsk=jnp.tile(g_swa_k, SWA_KV_HEADS)[None, :],
        wqs=w_qs.T.astype(BF16), gqa=g_q_a[None, :], wuq=wuq,
        gmq=jnp.pad(g_mla_q, (0, LANE - MLA_QK))[:, None], gsq=g_swa_q[:, None],
    )


def kernel(x, c, ctx, c_ctx, w_mod, b_mod, g_attn, w_in, g_q_a, w_uq, g_kv_a, w_ukv, g_mla_q, g_mla_k,
           g_swa_q, g_swa_k, swa_sink, w_out, g_mlp, w_mlp1, w_mlp2):
    bsz, seq, _ = x.shape
    assert w_mod.shape[0] == 1, "single-layer block"
    assert bsz < 8 and seq % MLP_TILE == 0 and ctx.shape[1] % TOK_TILE == 0

    cond = jnp.zeros((8, D_MODEL), F32).at[:bsz].set(c).at[bsz].set(c_ctx)
    mod = _modulation(cond, w_mod[0], b_mod[0][None, :])
    mod3 = mod.reshape(8, 6, D_MODEL)

    weights = _prep_weights(g_attn[0], w_in[0], g_q_a[0], w_uq[0], g_kv_a[0], w_ukv[0],
                            g_mla_q[0], g_mla_k[0], g_swa_q[0], g_swa_k[0])
    tables = _rope_tables(seq)

    km_l, vmt_l, ks_l, vst_l, qmt, qst = _project(x, mod3, None, weights, tables, latent=True, tile=PROJ_TILE)
    km_c, vmt_c, ks_c, vst_c = _project(ctx, mod3, bsz, weights, None, latent=False, tile=TOK_TILE)

    mix_m = _mla_attention(qmt, km_c, km_l, vmt_c, vmt_l)
    mix_s = _swa_attention(swa_sink[0], qst, ks_c, ks_l, vst_c, vst_l)

    return _outproj_mlp(x, mix_m, mix_s, mod3, w_out[0].astype(BF16), g_mlp[0][None, :],
                        w_mlp1[0].astype(BF16), w_mlp2[0].astype(BF16))
```

```python
import functools
import math

import jax
import jax.numpy as jnp
import numpy as np
from jax import lax
from jax.experimental import pallas as pl
from jax.experimental.pallas import tpu as pltpu

D_MODEL = 1024
GRID_W = 64
MLA_HEADS = 8
MLA_NOPE = 64
MLA_ROPE = 32
MLA_QK = MLA_NOPE + MLA_ROPE
MLA_V = 64
MLA_V_AUG = MLA_V + 16
Q_LORA = 256
KV_LORA = 128
SWA_HEADS = 8
SWA_KV_HEADS = 2
SWA_GROUP = SWA_HEADS // SWA_KV_HEADS
SWA_HD = 64
SWA_V_AUG = SWA_HD + 16
WINDOW = 128
D_FF = 4 * D_MODEL
ROPE_THETA = 10000.0
EPS = 1e-6
NEG_INF = -1e30
LOG2E = 1.4426950408889634
MLA_QSCALE = LOG2E / math.sqrt(MLA_QK)
SWA_QSCALE = LOG2E / math.sqrt(SWA_HD)

LANE = 128
TOK_TILE = 256
PROJ_TILE = 1024
PROJ_PART = 256
MLP_TILE = 512
FF_CHUNK = 1024
WSTAGE_WIDE_ROWS = 128
WSTAGE_TALL_ROWS = 512
SWA_WIN = 512
MLA_SUB = 256
SWA_SUB = 256
SWA_CHUNK = 32
SWA_QK_LEAD = 1
MLA_CHUNK = 32
MLA_QK_LEAD = 3
VMEM_LIMIT = 56 * 1024 * 1024

F32 = jnp.float32
BF16 = jnp.bfloat16
NT_DIMS = (((1,), (1,)), ((), ()))


def _dot(a, b):
    return jnp.dot(a, b, preferred_element_type=F32)


def _dot_nt(a, b):
    return lax.dot_general(a, b, NT_DIMS, preferred_element_type=F32)


def _mod_kernel(cond_ref, w_ref, b_ref, o_ref):
    cnd = cond_ref[...]
    act = cnd * jax.nn.sigmoid(cnd)
    o_ref[...] = _dot(act.astype(BF16), w_ref[...].astype(BF16)) + b_ref[...]


def _modulation(cond, w_mod, b_mod):
    n = w_mod.shape[1]
    tn = 1024
    return pl.pallas_call(
        _mod_kernel,
        grid=(n // tn,),
        in_specs=[
            pl.BlockSpec((8, D_MODEL), lambda i: (0, 0)),
            pl.BlockSpec((D_MODEL, tn), lambda i: (0, i)),
            pl.BlockSpec((1, tn), lambda i: (0, i)),
        ],
        out_specs=pl.BlockSpec((8, tn), lambda i: (0, i)),
        out_shape=jax.ShapeDtypeStruct((8, n), F32),
        name="modulation",
    )(cond, w_mod, b_mod)


def _rms_rows(x, n):
    return x * lax.rsqrt(jnp.sum(x * x, axis=-1, keepdims=True) * (1.0 / n) + EPS)


def _rms_cols(x, n):
    return x * lax.rsqrt(jnp.sum(x * x, axis=0, keepdims=True) * (1.0 / n) + EPS)


def _proj_kernel(*refs, latent):
    if latent:
        (x_ref, mod_ref, gattn_ref, wtok_ref, wvs_ref, gkva_ref, wuk_ref, wuv_ref, gmk_ref, gsk_ref,
         wqs_ref, gqa_ref, wuq_ref, gmq_ref, gsq_ref,
         mcos_ref, msin_ref, scos_ref, ssin_ref, kc_ref, ks1_ref, ks2_ref, sc_ref, ss1_ref, ss2_ref,
         km_ref, vmt_ref, ks_ref, vst_ref, qmt_ref, qst_ref) = refs
    else:
        (x_ref, mod_ref, gattn_ref, wtok_ref, wvs_ref, gkva_ref, wuk_ref, wuv_ref, gmk_ref, gsk_ref,
         km_ref, vmt_ref, ks_ref, vst_ref) = refs

    tile = x_ref.shape[1]
    n_part = max(1, tile // PROJ_PART)
    part = tile // n_part

    gain = gattn_ref[...] * (1.0 + mod_ref[0, 1:2, :])
    shift = mod_ref[0, 0:1, :]
    gmk = gmk_ref[...]
    ones_row = (lax.broadcasted_iota(jnp.int32, (MLA_V_AUG - MLA_V, part), 0) == 0).astype(BF16)
    if latent:
        gmq = gmq_ref[...] * MLA_QSCALE
        gsq = gsq_ref[...] * SWA_QSCALE

    def part_phases(lo):
        rows = slice(lo, lo + part)

        h = (_rms_rows(x_ref[0, rows, :], D_MODEL) * gain + shift).astype(BF16)
        yield

        zt = _dot(h, wtok_ref[...])
        vst = _dot_nt(wvs_ref[...], h)
        if latent:
            qst = _dot_nt(wqs_ref[...], h)
        yield

        cq = zt[:, 0:256]
        ckv = zt[:, 256:384]
        krp = zt[:, 384:512]
        ksw = zt[:, 512:640]
        ckvn = (_rms_rows(ckv, KV_LORA) * gkva_ref[...]).astype(BF16)
        if latent:
            cqn = (_rms_rows(cq, Q_LORA) * gqa_ref[...]).astype(BF16)
        kr = krp * gmk
        if latent:
            kr = (kr * kc_ref[rows, :] + pltpu.roll(kr, 112, 1) * ks1_ref[rows, :]
                  + pltpu.roll(kr, 16, 1) * ks2_ref[rows, :])
        ss_kr = jnp.sum(krp * krp, axis=-1, keepdims=True)
        first = lax.broadcasted_iota(jnp.int32, ksw.shape, 1) < SWA_HD
        sq = ksw * ksw
        ss0 = jnp.sum(jnp.where(first, sq, 0.0), axis=-1, keepdims=True)
        ss1 = jnp.sum(jnp.where(first, 0.0, sq), axis=-1, keepdims=True)
        rr = jnp.where(first, lax.rsqrt(ss0 * (1.0 / SWA_HD) + EPS), lax.rsqrt(ss1 * (1.0 / SWA_HD) + EPS))
        ksn = ksw * rr * gsk_ref[...]
        if latent:
            ksn = (ksn * sc_ref[rows, :] + pltpu.roll(ksn, 96, 1) * ss1_ref[rows, :]
                   + pltpu.roll(ksn, 32, 1) * ss2_ref[rows, :])
        ks_ref[0, rows, :] = ksn.astype(BF16)
        for g in range(SWA_KV_HEADS):
            for t in range(part // LANE):
                dst = lo // LANE + t
                vst_ref[0, g, dst, 0:SWA_HD, :] = vst[g * SWA_HD:(g + 1) * SWA_HD, t * LANE:(t + 1) * LANE].astype(BF16)
                vst_ref[0, g, dst, SWA_HD:SWA_V_AUG, :] = ones_row[:, 0:LANE]
        yield

        kpre = _dot(ckvn, wuk_ref[...])
        vt = _dot_nt(wuv_ref[...], ckvn)
        if latent:
            qt = _dot_nt(wuq_ref[...], cqn)
        yield

        for hd in range(MLA_HEADS):
            kp = kpre[:, hd * LANE:(hd + 1) * LANE]
            ss = jnp.sum(kp * kp, axis=-1, keepdims=True) + ss_kr
            kn = (kp * gmk + kr) * lax.rsqrt(ss * (1.0 / MLA_QK) + EPS)
            km_ref[0, hd, rows, :] = kn.astype(BF16)
            vmt_ref[0, hd, 0:MLA_V, rows] = vt[hd * MLA_V:(hd + 1) * MLA_V].astype(BF16)
            vmt_ref[0, hd, MLA_V:MLA_V_AUG, rows] = ones_row
        if latent:
            mcos = mcos_ref[:, rows]
            msin = msin_ref[:, rows]
            for hd in range(MLA_HEADS):
                qn = _rms_cols(qt[hd * LANE:(hd + 1) * LANE], MLA_QK) * gmq
                x1 = qn[64:80]
                x2 = qn[80:96]
                qmt_ref[0, hd, 0:64, rows] = qn[0:64].astype(BF16)
                qmt_ref[0, hd, 64:80, rows] = (x1 * mcos - x2 * msin).astype(BF16)
                qmt_ref[0, hd, 80:96, rows] = (x2 * mcos + x1 * msin).astype(BF16)
                qmt_ref[0, hd, 96:128, rows] = jnp.zeros((32, part), BF16)
            scos = scos_ref[:, rows]
            ssin = ssin_ref[:, rows]
            for hd in range(SWA_HEADS):
                qn = _rms_cols(qst[hd * SWA_HD:(hd + 1) * SWA_HD], SWA_HD) * gsq
                x1 = qn[0:32]
                x2 = qn[32:64]
                g = hd // SWA_GROUP
                base = g * SWA_HD
                other = (1 - g) * SWA_HD
                qst_ref[0, hd, base:base + 32, rows] = (x1 * scos - x2 * ssin).astype(BF16)
                qst_ref[0, hd, base + 32:base + 64, rows] = (x2 * scos + x1 * ssin).astype(BF16)
                qst_ref[0, hd, other:other + 64, rows] = jnp.zeros((64, part), BF16)
        yield

    n_phase = 5
    parts = [part_phases(i * part) for i in range(n_part)]
    for step in range(n_phase + n_part - 1):
        for i, gen in enumerate(parts):
            if 0 <= step - i < n_phase:
                next(gen)


def _full(shape):
    nd = len(shape)
    return pl.BlockSpec(shape, lambda b, j: (0,) * nd)


def _project(x, mod3, mod_row, weights, tables, latent, tile):
    bsz, n, _ = x.shape
    nt = n // tile
    if mod_row is None:
        mod_map = lambda b, j: (b, 0, 0)
    else:
        mod_map = lambda b, j: (mod_row, 0, 0)
    common = [weights[k] for k in ("gattn", "wtok", "wvs", "gkva", "wuk", "wuv", "gmk", "gsk")]
    ins = [x, mod3] + common
    in_specs = [
        pl.BlockSpec((1, tile, D_MODEL), lambda b, j: (b, j, 0)),
        pl.BlockSpec((1, 6, D_MODEL), mod_map),
    ] + [_full(w.shape) for w in common]
    out_shape = [
        jax.ShapeDtypeStruct((bsz, MLA_HEADS, n, LANE), BF16),
        jax.ShapeDtypeStruct((bsz, MLA_HEADS, MLA_V_AUG, n), BF16),
        jax.ShapeDtypeStruct((bsz, n, LANE), BF16),
        jax.ShapeDtypeStruct((bsz, SWA_KV_HEADS, n // LANE, SWA_V_AUG, LANE), BF16),
    ]
    out_specs = [
        pl.BlockSpec((1, MLA_HEADS, tile, LANE), lambda b, j: (b, 0, j, 0)),
        pl.BlockSpec((1, MLA_HEADS, MLA_V_AUG, tile), lambda b, j: (b, 0, 0, j)),
        pl.BlockSpec((1, tile, LANE), lambda b, j: (b, j, 0)),
        pl.BlockSpec((1, SWA_KV_HEADS, tile // LANE, SWA_V_AUG, LANE), lambda b, j: (b, 0, j, 0, 0)),
    ]
    if latent:
        extra = [weights[k] for k in ("wqs", "gqa", "wuq", "gmq", "gsq")]
        ins += extra
        in_specs += [_full(w.shape) for w in extra]
        feat = [tables[k] for k in ("mcos", "msin", "scos", "ssin")]
        ins += feat
        in_specs += [pl.BlockSpec((t.shape[0], tile), lambda b, j: (0, j)) for t in feat]
        tok = [tables[k] for k in ("kc", "ks1", "ks2", "sc", "ss1", "ss2")]
        ins += tok
        in_specs += [pl.BlockSpec((tile, LANE), lambda b, j: (j, 0)) for _ in tok]
        out_shape += [
            jax.ShapeDtypeStruct((bsz, MLA_HEADS, LANE, n), BF16),
            jax.ShapeDtypeStruct((bsz, SWA_HEADS, LANE, n), BF16),
        ]
        out_specs += [
            pl.BlockSpec((1, MLA_HEADS, LANE, tile), lambda b, j: (b, 0, 0, j)),
            pl.BlockSpec((1, SWA_HEADS, LANE, tile), lambda b, j: (b, 0, 0, j)),
        ]
    return pl.pallas_call(
        functools.partial(_proj_kernel, latent=latent),
        grid=(bsz, nt),
        in_specs=in_specs,
        out_specs=out_specs,
        out_shape=out_shape,
        compiler_params=pltpu.CompilerParams(vmem_limit_bytes=VMEM_LIMIT),
        name="project_latent" if latent else "project_context",
    )(*ins)


def _mla_kernel(qt_ref, kc_ref, kl_ref, vct_ref, vlt_ref, o_ref, ot_scr, s_scr, *, seq):
    n_ctx = kc_ref.shape[2]
    tq = qt_ref.shape[3]
    subs = ([(kc_ref, vct_ref, o) for o in range(0, n_ctx, MLA_SUB)]
            + [(kl_ref, vlt_ref, o) for o in range(0, seq, MLA_SUB)])
    n_sub = len(subs)

    def scores_into(slot, j, nxt, m8):
        q_ref, hd = nxt
        k_ref, _, off = subs[j]
        sj = _dot(k_ref[0, hd, off:off + MLA_SUB, :], q_ref[0, hd])
        s_scr[slot, j * MLA_SUB:(j + 1) * MLA_SUB, :] = sj
        mj = jnp.max(sj.reshape(MLA_SUB // 8, 8, tq), axis=0)
        return mj if m8 is None else jnp.maximum(m8, mj)

    def head_body(hd, m8_cur, slot, nxt):
        mb = jnp.broadcast_to(jnp.max(m8_cur, axis=0, keepdims=True), (MLA_CHUNK, tq))
        n_q = n_sub if nxt is not None else 0
        m8_next = None
        acc = None
        for j in range(min(MLA_QK_LEAD, n_q)):
            m8_next = scores_into(1 - slot, j, nxt, m8_next)
        for j in range(n_sub):
            base = j * MLA_SUB
            p = jnp.concatenate(
                [jnp.exp2(s_scr[slot, r:r + MLA_CHUNK, :] - mb).astype(BF16)
                 for r in range(base, base + MLA_SUB, MLA_CHUNK)], axis=0)
            _, v_ref, off = subs[j]
            d = _dot(v_ref[0, hd, :, off:off + MLA_SUB], p)
            acc = d if acc is None else acc + d
            if j + MLA_QK_LEAD < n_q:
                m8_next = scores_into(1 - slot, j + MLA_QK_LEAD, nxt, m8_next)
        row = pl.multiple_of(hd * MLA_V, MLA_V)
        ot_scr[pl.ds(row, MLA_V), :] = acc[0:MLA_V] / acc[MLA_V:MLA_V + 1]
        return m8_next

    def head_step(hd, m8):
        return lax.cond(lax.rem(hd, 2) == 0,
                        lambda m: head_body(hd, m, 0, (qt_ref, hd + 1)),
                        lambda m: head_body(hd, m, 1, (qt_ref, hd + 1)), m8)

    last = MLA_HEADS - 1
    m8_first = None
    for j in range(n_sub):
        m8_first = scores_into(0, j, (qt_ref, 0), m8_first)
    m8_last = lax.fori_loop(0, last, head_step, m8_first)
    head_body(last, m8_last, last % 2, None)
    o_ref[0] = ot_scr[...].T.astype(BF16)


def _mla_attention(qmt, km_c, km_l, vmt_c, vmt_l):
    bsz, _, _, s = qmt.shape
    n_ctx = km_c.shape[2]
    nt = s // TOK_TILE
    n_keys = n_ctx + s
    return pl.pallas_call(
        functools.partial(_mla_kernel, seq=s),
        grid=(bsz, nt),
        in_specs=[
            pl.BlockSpec((1, MLA_HEADS, LANE, TOK_TILE), lambda b, j: (b, 0, 0, j)),
            pl.BlockSpec((1, MLA_HEADS, n_ctx, LANE), lambda b, j: (b, 0, 0, 0)),
            pl.BlockSpec((1, MLA_HEADS, s, LANE), lambda b, j: (b, 0, 0, 0)),
            pl.BlockSpec((1, MLA_HEADS, MLA_V_AUG, n_ctx), lambda b, j: (b, 0, 0, 0)),
            pl.BlockSpec((1, MLA_HEADS, MLA_V_AUG, s), lambda b, j: (b, 0, 0, 0)),
        ],
        out_specs=pl.BlockSpec((1, TOK_TILE, MLA_HEADS * MLA_V), lambda b, j: (b, j, 0)),
        out_shape=jax.ShapeDtypeStruct((bsz, s, MLA_HEADS * MLA_V), BF16),
        scratch_shapes=[
            pltpu.VMEM((MLA_HEADS * MLA_V, TOK_TILE), F32),
            pltpu.VMEM((2, n_keys, TOK_TILE), F32),
        ],
        compiler_params=pltpu.CompilerParams(vmem_limit_bytes=VMEM_LIMIT,
                                             dimension_semantics=("arbitrary", "arbitrary")),
        name="mla_attention",
    )(qmt, km_c, km_l, vmt_c, vmt_l)


def _swa_kernel(sink_ref, bias_ref, qt_ref, qtn_ref, kc_ref, kl_ref, vct_ref, vlt_ref, o_ref,
                ot_scr, s_scr, m8_scr, *, seq):
    j = pl.program_id(1)
    nt = pl.num_programs(1)
    n_ctx = kc_ref.shape[1]
    tq = qt_ref.shape[3]
    n_band = SWA_WIN // SWA_SUB
    n_t = n_band + n_ctx // SWA_SUB
    lanes_per_sub = SWA_SUB // LANE
    subs = [(hh, t) for hh in range(SWA_GROUP) for t in range(n_t)]

    def window(jj):
        start = pl.multiple_of(jnp.clip(jj * TOK_TILE - WINDOW, 0, seq - SWA_WIN), LANE)
        variant = jnp.where(jj == 0, 0, jnp.where(jj == nt - 1, 2, 1))
        return start, variant

    def rows_of(idx):
        return slice(idx * SWA_SUB, (idx + 1) * SWA_SUB)

    def scores_into(slot, idx, q_ref, g, win):
        hh, t = subs[idx]
        start, variant = win
        if t < n_band:
            k = kl_ref[0, pl.ds(start + t * SWA_SUB, SWA_SUB), :]
        else:
            k = kc_ref[0, (t - n_band) * SWA_SUB:(t - n_band + 1) * SWA_SUB, :]
        sj = _dot(k, q_ref[0, g * SWA_GROUP + hh])
        if t < n_band:
            sj = sj + bias_ref[variant, t * SWA_SUB:(t + 1) * SWA_SUB, :]
        s_scr[slot, rows_of(idx), :] = sj
        return jnp.max(sj.reshape(SWA_SUB // 8, 8, tq), axis=0)

    def values_t(g, t, t0):
        if t < n_band:
            tiles = [vlt_ref[0, g, t0 + t * lanes_per_sub + i] for i in range(lanes_per_sub)]
        else:
            tiles = [vct_ref[0, g, (t - n_band) * lanes_per_sub + i] for i in range(lanes_per_sub)]
        return jnp.concatenate(tiles, axis=1)

    def next_scores(slot, idx, nxt, m8n):
        q_ref, g_n, win_n = nxt
        mj = scores_into(slot, idx, q_ref, g_n, win_n)
        hh, t = subs[idx]
        m8n[hh] = mj if t == 0 else jnp.maximum(m8n[hh], mj)
        if t == n_t - 1:
            m8_scr[slot, hh] = m8n[hh]

    def stage_body(g, win_cur, nxt):
        slot = g
        t0 = win_cur[0] // LANE
        n_q = len(subs) if nxt is not None else 0
        m8n = {}
        for idx in range(min(SWA_QK_LEAD, n_q)):
            next_scores(1 - slot, idx, nxt, m8n)
        for idx, (hh, t) in enumerate(subs):
            hd = g * SWA_GROUP + hh
            if t == 0:
                sk = sink_ref[hd] * LOG2E
                m = jnp.maximum(jnp.max(m8_scr[slot, hh], axis=0, keepdims=True), sk)
                mb = jnp.broadcast_to(m, (SWA_CHUNK, tq))
                acc = None
            base = idx * SWA_SUB
            p = jnp.concatenate(
                [jnp.exp2(s_scr[slot, r:r + SWA_CHUNK, :] - mb).astype(BF16)
                 for r in range(base, base + SWA_SUB, SWA_CHUNK)], axis=0)
            d = _dot(values_t(g, t, t0), p)
            acc = d if acc is None else acc + d
            if t == n_t - 1:
                denom = acc[SWA_HD:SWA_HD + 1] + jnp.exp2(sk - m)
                ot_scr[hd * SWA_HD:(hd + 1) * SWA_HD, :] = acc[0:SWA_HD] / denom
            if idx + SWA_QK_LEAD < n_q:
                next_scores(1 - slot, idx + SWA_QK_LEAD, nxt, m8n)

    win = window(j)

    @pl.when(j == 0)
    def _():
        m8n = {}
        for idx in range(len(subs)):
            next_scores(0, idx, (qt_ref, 0, win), m8n)

    stage_body(0, win, (qt_ref, 1, win))

    @pl.when(j < nt - 1)
    def _():
        stage_body(1, win, (qtn_ref, 0, window(j + 1)))

    @pl.when(j == nt - 1)
    def _():
        stage_body(1, win, None)

    o_ref[0] = ot_scr[...].T.astype(BF16)


def _swa_bias(seq):
    nt = seq // TOK_TILE
    out = []
    for j in (0, 1, nt - 1):
        start = min(max(j * TOK_TILE - WINDOW, 0), seq - SWA_WIN)
        qpos = j * TOK_TILE + np.arange(TOK_TILE)[None, :]
        kpos = start + np.arange(SWA_WIN)[:, None]
        out.append(np.where(np.abs(qpos - kpos) <= WINDOW, 0.0, NEG_INF))
    return jnp.asarray(np.stack(out), F32)


def _swa_attention(sink, qst, ks_c, ks_l, vst_c, vst_l):
    bsz, _, _, s = qst.shape
    n_ctx = ks_c.shape[1]
    nt = s // TOK_TILE
    assert nt >= 3 and s >= SWA_WIN and n_ctx % SWA_SUB == 0
    stage_rows = SWA_GROUP * (SWA_WIN + n_ctx)
    return pl.pallas_call(
        functools.partial(_swa_kernel, seq=s),
        grid=(bsz, nt),
        in_specs=[
            pl.BlockSpec(memory_space=pltpu.SMEM),
            pl.BlockSpec((3, SWA_WIN, TOK_TILE), lambda b, j: (0, 0, 0)),
            pl.BlockSpec((1, SWA_HEADS, LANE, TOK_TILE), lambda b, j: (b, 0, 0, j)),
            pl.BlockSpec((1, SWA_GROUP, LANE, TOK_TILE), lambda b, j: (b, 0, 0, jnp.minimum(j + 1, nt - 1))),
            pl.BlockSpec((1, n_ctx, LANE), lambda b, j: (b, 0, 0)),
            pl.BlockSpec((1, s, LANE), lambda b, j: (b, 0, 0)),
            pl.BlockSpec((1, SWA_KV_HEADS, n_ctx // LANE, SWA_V_AUG, LANE), lambda b, j: (b, 0, 0, 0, 0)),
            pl.BlockSpec((1, SWA_KV_HEADS, s // LANE, SWA_V_AUG, LANE), lambda b, j: (b, 0, 0, 0, 0)),
        ],
        out_specs=pl.BlockSpec((1, TOK_TILE, SWA_HEADS * SWA_HD), lambda b, j: (b, j, 0)),
        out_shape=jax.ShapeDtypeStruct((bsz, s, SWA_HEADS * SWA_HD), BF16),
        scratch_shapes=[
            pltpu.VMEM((SWA_HEADS * SWA_HD, TOK_TILE), F32),
            pltpu.VMEM((2, stage_rows, TOK_TILE), F32),
            pltpu.VMEM((2, SWA_GROUP, 8, TOK_TILE), F32),
        ],
        compiler_params=pltpu.CompilerParams(vmem_limit_bytes=VMEM_LIMIT,
                                             dimension_semantics=("arbitrary", "arbitrary")),
        name="swa_attention",
    )(sink, _swa_bias(s), qst, qst, ks_c, ks_l, vst_c, vst_l)


def _stream_cast(jobs, stage, sem):
    rows = stage.shape[1]

    def copy(i):
        src, _, r = jobs[i]
        return pltpu.make_async_copy(src.at[pl.ds(r, rows), :], stage.at[i % 2], sem.at[i % 2])

    copy(0).start()
    for i, (_, dst, r) in enumerate(jobs):
        if i + 1 < len(jobs):
            copy(i + 1).start()
        copy(i).wait()
        dst[r:r + rows, :] = stage[i % 2].astype(BF16)


def _mlp_kernel(x_ref, mm_ref, ms_ref, mod_ref, wo_hbm, gmlp_ref, w1_hbm, w2_hbm, o_ref,
                wo_ref, w1_ref, w2_ref, stage_wide, stage_tall, sem_wide, sem_tall, acc_ref):
    @pl.when((pl.program_id(0) == 0) & (pl.program_id(1) == 0))
    def _():
        tall = stage_tall.shape[1]
        _stream_cast([(wo_hbm, wo_ref, r) for r in range(0, D_MODEL, tall)]
                     + [(w2_hbm, w2_ref, r) for r in range(0, D_FF, tall)], stage_tall, sem_tall)
        wide = stage_wide.shape[1]
        _stream_cast([(w1_hbm, w1_ref, r) for r in range(0, D_MODEL, wide)], stage_wide, sem_wide)

    half = MLA_HEADS * MLA_V
    y1 = _dot(mm_ref[0], wo_ref[0:half, :]) + _dot(ms_ref[0], wo_ref[half:, :])
    x1 = x_ref[0] + mod_ref[0, 2:3, :] * y1
    h2 = (_rms_rows(x1, D_MODEL) * gmlp_ref[...] * (1.0 + mod_ref[0, 4:5, :])
          + mod_ref[0, 3:4, :]).astype(BF16)
    for c in range(D_FF // FF_CHUNK):
        a = jnp.maximum(_dot(h2, w1_ref[:, c * FF_CHUNK:(c + 1) * FF_CHUNK]), 0.0)
        part = _dot((a * a).astype(BF16), w2_ref[c * FF_CHUNK:(c + 1) * FF_CHUNK, :])
        if c == 0:
            acc_ref[...] = part
        else:
            acc_ref[...] += part
    o_ref[0] = x1 + mod_ref[0, 5:6, :] * acc_ref[...]


def _outproj_mlp(x, mix_m, mix_s, mod3, w_out, g_mlp, w1, w2):
    bsz, s, _ = x.shape
    half = MLA_HEADS * MLA_V
    const = lambda b, j: (0, 0)
    hbm = pl.BlockSpec(memory_space=pl.ANY)
    return pl.pallas_call(
        _mlp_kernel,
        grid=(bsz, s // MLP_TILE),
        in_specs=[
            pl.BlockSpec((1, MLP_TILE, D_MODEL), lambda b, j: (b, j, 0)),
            pl.BlockSpec((1, MLP_TILE, half), lambda b, j: (b, j, 0)),
            pl.BlockSpec((1, MLP_TILE, half), lambda b, j: (b, j, 0)),
            pl.BlockSpec((1, 6, D_MODEL), lambda b, j: (b, 0, 0)),
            hbm,
            pl.BlockSpec((1, D_MODEL), const),
            hbm,
            hbm,
        ],
        out_specs=pl.BlockSpec((1, MLP_TILE, D_MODEL), lambda b, j: (b, j, 0)),
        out_shape=jax.ShapeDtypeStruct((bsz, s, D_MODEL), F32),
        scratch_shapes=[
            pltpu.VMEM((D_MODEL, D_MODEL), BF16),
            pltpu.VMEM((D_MODEL, D_FF), BF16),
            pltpu.VMEM((D_FF, D_MODEL), BF16),
            pltpu.VMEM((2, WSTAGE_WIDE_ROWS, D_FF), F32),
            pltpu.VMEM((2, WSTAGE_TALL_ROWS, D_MODEL), F32),
            pltpu.SemaphoreType.DMA((2,)),
            pltpu.SemaphoreType.DMA((2,)),
            pltpu.VMEM((MLP_TILE, D_MODEL), F32),
        ],
        compiler_params=pltpu.CompilerParams(vmem_limit_bytes=VMEM_LIMIT,
                                             dimension_semantics=("arbitrary", "arbitrary")),
        name="outproj_mlp",
    )(x, mix_m, mix_s, mod3, w_out, g_mlp, w1, w2)


def _rope_tables(seq):
    f32 = np.float32

    def cos_sin(rot_dim):
        n_freq = rot_dim // 4
        inv = f32(ROPE_THETA) ** (-np.arange(n_freq, dtype=f32) / f32(n_freq))
        rows = seq // GRID_W
        row = np.repeat(np.arange(rows, dtype=f32), GRID_W)
        col = np.tile(np.arange(GRID_W, dtype=f32), rows)
        ang = np.concatenate([row[:, None] * inv, col[:, None] * inv], axis=-1).astype(f32)
        return np.cos(ang).astype(f32), np.sin(ang).astype(f32)

    mcos, msin = cos_sin(MLA_ROPE)
    scos, ssin = cos_sin(SWA_HD)
    z16 = np.zeros((seq, 16), f32)
    z32 = np.zeros((seq, 32), f32)
    z64 = np.zeros((seq, 64), f32)
    kc = np.concatenate([np.ones((seq, 64), f32), mcos, mcos, np.ones((seq, 32), f32)], axis=1)
    ks1 = np.concatenate([z64, -msin, z16, z32], axis=1)
    ks2 = np.concatenate([z64, z16, msin, z32], axis=1)
    sc = np.concatenate([scos, scos, scos, scos], axis=1)
    ss1 = np.concatenate([-ssin, z32, -ssin, z32], axis=1)
    ss2 = np.concatenate([z32, ssin, z32, ssin], axis=1)
    tabs = dict(mcos=mcos.T, msin=msin.T, scos=scos.T, ssin=ssin.T,
                kc=kc, ks1=ks1, ks2=ks2, sc=sc, ss1=ss1, ss2=ss2)
    return {k: jnp.asarray(np.ascontiguousarray(v)) for k, v in tabs.items()}


def _prep_weights(g_attn, w_in, g_q_a, w_uq, g_kv_a, w_ukv, g_mla_q, g_mla_k, g_swa_q, g_swa_k):
    o_ckv = Q_LORA
    o_kr = o_ckv + KV_LORA
    o_qs = o_kr + MLA_ROPE
    o_ks = o_qs + SWA_HEADS * SWA_HD
    o_vs = o_ks + SWA_KV_HEADS * SWA_HD
    w_cq = w_in[:, :o_ckv]
    w_ckv = w_in[:, o_ckv:o_kr]
    w_kr = w_in[:, o_kr:o_qs]
    w_qs = w_in[:, o_qs:o_ks]
    w_ks = w_in[:, o_ks:o_vs]
    w_vs = w_in[:, o_vs:]
    w_kr_p = jnp.pad(w_kr, ((0, 0), (MLA_NOPE, LANE - MLA_QK)))
    wtok = jnp.concatenate([w_cq, w_ckv, w_kr_p, w_ks], axis=1).astype(BF16)
    w_ukv_h = w_ukv.reshape(KV_LORA, MLA_HEADS, MLA_NOPE + MLA_V)
    wuk = jnp.pad(w_ukv_h[:, :, :MLA_NOPE], ((0, 0), (0, 0), (0, LANE - MLA_NOPE)))
    wuk = wuk.reshape(KV_LORA, MLA_HEADS * LANE).astype(BF16)
    wuv = w_ukv_h[:, :, MLA_NOPE:].reshape(KV_LORA, MLA_HEADS * MLA_V).T.astype(BF16)
    w_uq_h = jnp.pad(w_uq.reshape(Q_LORA, MLA_HEADS, MLA_QK), ((0, 0), (0, 0), (0, LANE - MLA_QK)))
    wuq = w_uq_h.reshape(Q_LORA, MLA_HEADS * LANE).T.astype(BF16)
    return dict(
        gattn=g_attn[None, :], wtok=wtok, wvs=w_vs.T.astype(BF16), gkva=g_kv_a[None, :],
        wuk=wuk, wuv=wuv, gmk=jnp.pad(g_mla_k, (0, LANE - MLA_QK))[None, :],
        gsk=jnp.tile(g_swa_k, SWA_KV_HEADS)[None, :],
        wqs=w_qs.T.astype(BF16), gqa=g_q_a[None, :], wuq=wuq,
        gmq=jnp.pad(g_mla_q, (0, LANE - MLA_QK))[:, None], gsq=g_swa_q[:, None],
    )


def kernel(x, c, ctx, c_ctx, w_mod, b_mod, g_attn, w_in, g_q_a, w_uq, g_kv_a, w_ukv, g_mla_q, g_mla_k,
           g_swa_q, g_swa_k, swa_sink, w_out, g_mlp, w_mlp1, w_mlp2):
    bsz, seq, _ = x.shape
    assert w_mod.shape[0] == 1, "single-layer block"
    assert bsz < 8 and seq % MLP_TILE == 0 and ctx.shape[1] % TOK_TILE == 0

    cond = jnp.zeros((8, D_MODEL), F32).at[:bsz].set(c).at[bsz].set(c_ctx)
    mod = _modulation(cond, w_mod[0], b_mod[0][None, :])
    mod3 = mod.reshape(8, 6, D_MODEL)

    weights = _prep_weights(g_attn[0], w_in[0], g_q_a[0], w_uq[0], g_kv_a[0], w_ukv[0],
                            g_mla_q[0], g_mla_k[0], g_swa_q[0], g_swa_k[0])
    tables = _rope_tables(seq)

    km_l, vmt_l, ks_l, vst_l, qmt, qst = _project(x, mod3, None, weights, tables, latent=True, tile=PROJ_TILE)
    km_c, vmt_c, ks_c, vst_c = _project(ctx, mod3, bsz, weights, None, latent=False, tile=TOK_TILE)

    mix_m = _mla_attention(qmt, km_c, km_l, vmt_c, vmt_l)
    mix_s = _swa_attention(swa_sink[0], qst, ks_c, ks_l, vst_c, vst_l)

    return _outproj_mlp(x, mix_m, mix_s, mod3, w_out[0], g_mlp[0][None, :], w_mlp1[0], w_mlp2[0])
```

```python
import functools
import math

import jax
import jax.numpy as jnp
import numpy as np
from jax import lax
from jax.experimental import pallas as pl
from jax.experimental.pallas import tpu as pltpu

D_MODEL = 1024
GRID_W = 64
MLA_HEADS = 8
MLA_NOPE = 64
MLA_ROPE = 32
MLA_QK = MLA_NOPE + MLA_ROPE
MLA_V = 64
MLA_V_AUG = MLA_V + 16
Q_LORA = 256
KV_LORA = 128
SWA_HEADS = 8
SWA_KV_HEADS = 2
SWA_GROUP = SWA_HEADS // SWA_KV_HEADS
SWA_HD = 64
SWA_V_AUG = SWA_HD + 16
WINDOW = 128
D_FF = 4 * D_MODEL
ROPE_THETA = 10000.0
EPS = 1e-6
NEG_INF = -1e30
LOG2E = 1.4426950408889634
MLA_QSCALE = LOG2E / math.sqrt(MLA_QK)
SWA_QSCALE = LOG2E / math.sqrt(SWA_HD)

LANE = 128
SUBLANE = 8
MXU_DIM = 256
TOK_TILE = MXU_DIM
PROJ_TILE = 1024
PROJ_PART = 256
MOD_TILE = 1024
MLP_TILE = 512
FF_CHUNK = 1024
SWA_WIN = 2 * TOK_TILE
MLA_SUB = MXU_DIM
SWA_SUB = MXU_DIM
MLA_CHUNK = 32
SWA_CHUNK = 32
MLA_QK_LEAD = 3
SWA_QK_LEAD = 1
VMEM_LIMIT = 56 * 1024 * 1024

F32 = jnp.float32
BF16 = jnp.bfloat16
NT_DIMS = (((1,), (1,)), ((), ()))


def _dot(a, b):
    return jnp.dot(a, b, preferred_element_type=F32)


def _dot_nt(a, b):
    return lax.dot_general(a, b, NT_DIMS, preferred_element_type=F32)


def _mod_kernel(cond_ref, w_ref, b_ref, o_ref):
    cnd = cond_ref[...]
    act = cnd * jax.nn.sigmoid(cnd)
    o_ref[...] = _dot(act.astype(BF16), w_ref[...].astype(BF16)) + b_ref[...]


def _modulation(cond, w_mod, b_mod):
    n = w_mod.shape[1]
    tn = MOD_TILE
    return pl.pallas_call(
        _mod_kernel,
        grid=(n // tn,),
        in_specs=[
            pl.BlockSpec((SUBLANE, D_MODEL), lambda i: (0, 0)),
            pl.BlockSpec((D_MODEL, tn), lambda i: (0, i)),
            pl.BlockSpec((1, tn), lambda i: (0, i)),
        ],
        out_specs=pl.BlockSpec((SUBLANE, tn), lambda i: (0, i)),
        out_shape=jax.ShapeDtypeStruct((SUBLANE, n), F32),
        name="modulation",
    )(cond, w_mod, b_mod)


def _rms_rows(x, n):
    return x * lax.rsqrt(jnp.sum(x * x, axis=-1, keepdims=True) * (1.0 / n) + EPS)


def _rms_cols(x, n):
    return x * lax.rsqrt(jnp.sum(x * x, axis=0, keepdims=True) * (1.0 / n) + EPS)


def _proj_kernel(*refs, latent):
    if latent:
        (x_ref, mod_ref, gattn_ref, wtok_ref, wvs_ref, gkva_ref, wuk_ref, wuv_ref, gmk_ref, gsk_ref,
         wqs_ref, gqa_ref, wuq_ref, gmq_ref, gsq_ref,
         mcos_ref, msin_ref, scos_ref, ssin_ref, kc_ref, ks1_ref, ks2_ref, sc_ref, ss1_ref, ss2_ref,
         wo_f32_ref, w1_f32_ref, w2_f32_ref,
         km_ref, vmt_ref, ks_ref, vst_ref, qmt_ref, qst_ref,
         wo_bf16_ref, w1_bf16_ref, w2_bf16_ref) = refs
        for src, dst in ((wo_f32_ref, wo_bf16_ref), (w1_f32_ref, w1_bf16_ref), (w2_f32_ref, w2_bf16_ref)):
            dst[...] = src[...].astype(BF16)
    else:
        (x_ref, mod_ref, gattn_ref, wtok_ref, wvs_ref, gkva_ref, wuk_ref, wuv_ref, gmk_ref, gsk_ref,
         km_ref, vmt_ref, ks_ref, vst_ref) = refs

    tile = x_ref.shape[1]
    n_part = max(1, tile // PROJ_PART)
    part = tile // n_part

    gain = gattn_ref[...] * (1.0 + mod_ref[0, 1:2, :])
    shift = mod_ref[0, 0:1, :]
    gmk = gmk_ref[...]
    ones_row = (lax.broadcasted_iota(jnp.int32, (MLA_V_AUG - MLA_V, part), 0) == 0).astype(BF16)
    if latent:
        gmq = gmq_ref[...] * MLA_QSCALE
        gsq = gsq_ref[...] * SWA_QSCALE

    def part_phases(lo):
        rows = slice(lo, lo + part)

        h = (_rms_rows(x_ref[0, rows, :], D_MODEL) * gain + shift).astype(BF16)
        yield

        zt = _dot(h, wtok_ref[...])
        vst = _dot_nt(wvs_ref[...], h)
        if latent:
            qst = _dot_nt(wqs_ref[...], h)
        yield

        cq = zt[:, 0:256]
        ckv = zt[:, 256:384]
        krp = zt[:, 384:512]
        ksw = zt[:, 512:640]
        ckvn = (_rms_rows(ckv, KV_LORA) * gkva_ref[...]).astype(BF16)
        if latent:
            cqn = (_rms_rows(cq, Q_LORA) * gqa_ref[...]).astype(BF16)
        kr = krp * gmk
        if latent:
            kr = (kr * kc_ref[rows, :] + pltpu.roll(kr, 112, 1) * ks1_ref[rows, :]
                  + pltpu.roll(kr, 16, 1) * ks2_ref[rows, :])
        ss_kr = jnp.sum(krp * krp, axis=-1, keepdims=True)
        first = lax.broadcasted_iota(jnp.int32, ksw.shape, 1) < SWA_HD
        sq = ksw * ksw
        ss0 = jnp.sum(jnp.where(first, sq, 0.0), axis=-1, keepdims=True)
        ss1 = jnp.sum(jnp.where(first, 0.0, sq), axis=-1, keepdims=True)
        rr = jnp.where(first, lax.rsqrt(ss0 * (1.0 / SWA_HD) + EPS), lax.rsqrt(ss1 * (1.0 / SWA_HD) + EPS))
        ksn = ksw * rr * gsk_ref[...]
        if latent:
            ksn = (ksn * sc_ref[rows, :] + pltpu.roll(ksn, 96, 1) * ss1_ref[rows, :]
                   + pltpu.roll(ksn, 32, 1) * ss2_ref[rows, :])
        ks_ref[0, rows, :] = ksn.astype(BF16)
        for g in range(SWA_KV_HEADS):
            for t in range(part // LANE):
                dst = lo // LANE + t
                vst_ref[0, g, dst, 0:SWA_HD, :] = vst[g * SWA_HD:(g + 1) * SWA_HD, t * LANE:(t + 1) * LANE].astype(BF16)
                vst_ref[0, g, dst, SWA_HD:SWA_V_AUG, :] = ones_row[:, 0:LANE]
        yield

        kpre = _dot(ckvn, wuk_ref[...])
        vt = _dot_nt(wuv_ref[...], ckvn)
        if latent:
            qt = _dot_nt(wuq_ref[...], cqn)
        yield

        for hd in range(MLA_HEADS):
            kp = kpre[:, hd * LANE:(hd + 1) * LANE]
            ss = jnp.sum(kp * kp, axis=-1, keepdims=True) + ss_kr
            kn = (kp * gmk + kr) * lax.rsqrt(ss * (1.0 / MLA_QK) + EPS)
            km_ref[0, hd, rows, :] = kn.astype(BF16)
            vmt_ref[0, hd, 0:MLA_V, rows] = vt[hd * MLA_V:(hd + 1) * MLA_V].astype(BF16)
            vmt_ref[0, hd, MLA_V:MLA_V_AUG, rows] = ones_row
        if latent:
            mcos = mcos_ref[:, rows]
            msin = msin_ref[:, rows]
            for hd in range(MLA_HEADS):
                qn = _rms_cols(qt[hd * LANE:(hd + 1) * LANE], MLA_QK) * gmq
                x1 = qn[64:80]
                x2 = qn[80:96]
                qmt_ref[0, hd, 0:64, rows] = qn[0:64].astype(BF16)
                qmt_ref[0, hd, 64:80, rows] = (x1 * mcos - x2 * msin).astype(BF16)
                qmt_ref[0, hd, 80:96, rows] = (x2 * mcos + x1 * msin).astype(BF16)
                qmt_ref[0, hd, 96:128, rows] = jnp.zeros((32, part), BF16)
            scos = scos_ref[:, rows]
            ssin = ssin_ref[:, rows]
            for hd in range(SWA_HEADS):
                qn = _rms_cols(qst[hd * SWA_HD:(hd + 1) * SWA_HD], SWA_HD) * gsq
                x1 = qn[0:32]
                x2 = qn[32:64]
                g = hd // SWA_GROUP
                base = g * SWA_HD
                other = (1 - g) * SWA_HD
                qst_ref[0, hd, base:base + 32, rows] = (x1 * scos - x2 * ssin).astype(BF16)
                qst_ref[0, hd, base + 32:base + 64, rows] = (x2 * scos + x1 * ssin).astype(BF16)
                qst_ref[0, hd, other:other + 64, rows] = jnp.zeros((64, part), BF16)
        yield

    n_phase = 5
    parts = [part_phases(i * part) for i in range(n_part)]
    for step in range(n_phase + n_part - 1):
        for i, gen in enumerate(parts):
            if 0 <= step - i < n_phase:
                next(gen)


def _full(shape):
    nd = len(shape)
    return pl.BlockSpec(shape, lambda b, j: (0,) * nd)


def _project(x, mod3, mod_row, weights, tables, latent, tile):
    bsz, n, _ = x.shape
    nt = n // tile
    if mod_row is None:
        mod_map = lambda b, j: (b, 0, 0)
    else:
        mod_map = lambda b, j: (mod_row, 0, 0)
    common = [weights[k] for k in ("gattn", "wtok", "wvs", "gkva", "wuk", "wuv", "gmk", "gsk")]
    ins = [x, mod3] + common
    in_specs = [
        pl.BlockSpec((1, tile, D_MODEL), lambda b, j: (b, j, 0)),
        pl.BlockSpec((1, 6, D_MODEL), mod_map),
    ] + [_full(w.shape) for w in common]
    out_shape = [
        jax.ShapeDtypeStruct((bsz, MLA_HEADS, n, LANE), BF16),
        jax.ShapeDtypeStruct((bsz, MLA_HEADS, MLA_V_AUG, n), BF16),
        jax.ShapeDtypeStruct((bsz, n, LANE), BF16),
        jax.ShapeDtypeStruct((bsz, SWA_KV_HEADS, n // LANE, SWA_V_AUG, LANE), BF16),
    ]
    out_specs = [
        pl.BlockSpec((1, MLA_HEADS, tile, LANE), lambda b, j: (b, 0, j, 0)),
        pl.BlockSpec((1, MLA_HEADS, MLA_V_AUG, tile), lambda b, j: (b, 0, 0, j)),
        pl.BlockSpec((1, tile, LANE), lambda b, j: (b, j, 0)),
        pl.BlockSpec((1, SWA_KV_HEADS, tile // LANE, SWA_V_AUG, LANE), lambda b, j: (b, 0, j, 0, 0)),
    ]
    if latent:
        extra = [weights[k] for k in ("wqs", "gqa", "wuq", "gmq", "gsq")]
        ins += extra
        in_specs += [_full(w.shape) for w in extra]
        feat = [tables[k] for k in ("mcos", "msin", "scos", "ssin")]
        ins += feat
        in_specs += [pl.BlockSpec((t.shape[0], tile), lambda b, j: (0, j)) for t in feat]
        tok = [tables[k] for k in ("kc", "ks1", "ks2", "sc", "ss1", "ss2")]
        ins += tok
        in_specs += [pl.BlockSpec((tile, LANE), lambda b, j: (j, 0)) for _ in tok]
        out_shape += [
            jax.ShapeDtypeStruct((bsz, MLA_HEADS, LANE, n), BF16),
            jax.ShapeDtypeStruct((bsz, SWA_HEADS, LANE, n), BF16),
        ]
        out_specs += [
            pl.BlockSpec((1, MLA_HEADS, LANE, tile), lambda b, j: (b, 0, 0, j)),
            pl.BlockSpec((1, SWA_HEADS, LANE, tile), lambda b, j: (b, 0, 0, j)),
        ]
        steps = bsz * nt
        for w in weights["to_cast"]:
            rows = w.shape[0] // steps
            assert rows * steps == w.shape[0] and rows % 16 == 0
            ins.append(w)
            in_specs.append(pl.BlockSpec((rows, w.shape[1]), lambda b, j: (b * nt + j, 0)))
            out_shape.append(jax.ShapeDtypeStruct(w.shape, BF16))
            out_specs.append(pl.BlockSpec((rows, w.shape[1]), lambda b, j: (b * nt + j, 0)))
    return pl.pallas_call(
        functools.partial(_proj_kernel, latent=latent),
        grid=(bsz, nt),
        in_specs=in_specs,
        out_specs=out_specs,
        out_shape=out_shape,
        compiler_params=pltpu.CompilerParams(vmem_limit_bytes=VMEM_LIMIT),
        name="project_latent" if latent else "project_context",
    )(*ins)


def _mla_kernel(qt_ref, kc_ref, kl_ref, vct_ref, vlt_ref, o_ref, ot_scr, s_scr, *, seq):
    n_ctx = kc_ref.shape[2]
    tq = qt_ref.shape[3]
    subs = ([(kc_ref, vct_ref, o) for o in range(0, n_ctx, MLA_SUB)]
            + [(kl_ref, vlt_ref, o) for o in range(0, seq, MLA_SUB)])
    n_sub = len(subs)

    def scores_into(slot, j, nxt, m8):
        q_ref, hd = nxt
        k_ref, _, off = subs[j]
        sj = _dot(k_ref[0, hd, off:off + MLA_SUB, :], q_ref[0, hd])
        s_scr[slot, j * MLA_SUB:(j + 1) * MLA_SUB, :] = sj
        mj = jnp.max(sj.reshape(MLA_SUB // SUBLANE, SUBLANE, tq), axis=0)
        return mj if m8 is None else jnp.maximum(m8, mj)

    def head_body(hd, m8_cur, slot, nxt):
        mb = jnp.broadcast_to(jnp.max(m8_cur, axis=0, keepdims=True), (MLA_CHUNK, tq))
        n_q = n_sub if nxt is not None else 0
        m8_next = None
        acc = None
        for j in range(min(MLA_QK_LEAD, n_q)):
            m8_next = scores_into(1 - slot, j, nxt, m8_next)
        for j in range(n_sub):
            base = j * MLA_SUB
            p = jnp.concatenate(
                [jnp.exp2(s_scr[slot, r:r + MLA_CHUNK, :] - mb).astype(BF16)
                 for r in range(base, base + MLA_SUB, MLA_CHUNK)], axis=0)
            _, v_ref, off = subs[j]
            d = _dot(v_ref[0, hd, :, off:off + MLA_SUB], p)
            acc = d if acc is None else acc + d
            if j + MLA_QK_LEAD < n_q:
                m8_next = scores_into(1 - slot, j + MLA_QK_LEAD, nxt, m8_next)
        row = pl.multiple_of(hd * MLA_V, MLA_V)
        ot_scr[pl.ds(row, MLA_V), :] = acc[0:MLA_V] / acc[MLA_V:MLA_V + 1]
        return m8_next

    def head_step(hd, m8):
        return lax.cond(lax.rem(hd, 2) == 0,
                        lambda m: head_body(hd, m, 0, (qt_ref, hd + 1)),
                        lambda m: head_body(hd, m, 1, (qt_ref, hd + 1)), m8)

    last = MLA_HEADS - 1
    m8_first = None
    for j in range(n_sub):
        m8_first = scores_into(0, j, (qt_ref, 0), m8_first)
    m8_last = lax.fori_loop(0, last, head_step, m8_first)
    head_body(last, m8_last, last % 2, None)
    o_ref[0] = ot_scr[...].T.astype(BF16)


def _mla_attention(qmt, km_c, km_l, vmt_c, vmt_l):
    bsz, _, _, s = qmt.shape
    n_ctx = km_c.shape[2]
    nt = s // TOK_TILE
    n_keys = n_ctx + s
    return pl.pallas_call(
        functools.partial(_mla_kernel, seq=s),
        grid=(bsz, nt),
        in_specs=[
            pl.BlockSpec((1, MLA_HEADS, LANE, TOK_TILE), lambda b, j: (b, 0, 0, j)),
            pl.BlockSpec((1, MLA_HEADS, n_ctx, LANE), lambda b, j: (b, 0, 0, 0)),
            pl.BlockSpec((1, MLA_HEADS, s, LANE), lambda b, j: (b, 0, 0, 0)),
            pl.BlockSpec((1, MLA_HEADS, MLA_V_AUG, n_ctx), lambda b, j: (b, 0, 0, 0)),
            pl.BlockSpec((1, MLA_HEADS, MLA_V_AUG, s), lambda b, j: (b, 0, 0, 0)),
        ],
        out_specs=pl.BlockSpec((1, TOK_TILE, MLA_HEADS * MLA_V), lambda b, j: (b, j, 0)),
        out_shape=jax.ShapeDtypeStruct((bsz, s, MLA_HEADS * MLA_V), BF16),
        scratch_shapes=[
            pltpu.VMEM((MLA_HEADS * MLA_V, TOK_TILE), F32),
            pltpu.VMEM((2, n_keys, TOK_TILE), F32),
        ],
        compiler_params=pltpu.CompilerParams(vmem_limit_bytes=VMEM_LIMIT,
                                             dimension_semantics=("arbitrary", "arbitrary")),
        name="mla_attention",
    )(qmt, km_c, km_l, vmt_c, vmt_l)


def _swa_kernel(sink_ref, bias_ref, qt_ref, qtn_ref, kc_ref, kl_ref, vct_ref, vlt_ref, o_ref,
                ot_scr, s_scr, m8_scr, *, seq):
    j = pl.program_id(1)
    nt = pl.num_programs(1)
    n_ctx = kc_ref.shape[1]
    tq = qt_ref.shape[3]
    n_band = SWA_WIN // SWA_SUB
    n_t = n_band + n_ctx // SWA_SUB
    lanes_per_sub = SWA_SUB // LANE
    subs = [(hh, t) for hh in range(SWA_GROUP) for t in range(n_t)]

    def window(jj):
        start = pl.multiple_of(jnp.clip(jj * TOK_TILE - WINDOW, 0, seq - SWA_WIN), LANE)
        variant = jnp.where(jj == 0, 0, jnp.where(jj == nt - 1, 2, 1))
        return start, variant

    def rows_of(idx):
        return slice(idx * SWA_SUB, (idx + 1) * SWA_SUB)

    def scores_into(slot, idx, q_ref, g, win):
        hh, t = subs[idx]
        start, variant = win
        if t < n_band:
            k = kl_ref[0, pl.ds(start + t * SWA_SUB, SWA_SUB), :]
        else:
            k = kc_ref[0, (t - n_band) * SWA_SUB:(t - n_band + 1) * SWA_SUB, :]
        sj = _dot(k, q_ref[0, g * SWA_GROUP + hh])
        if t < n_band:
            sj = sj + bias_ref[variant, t * SWA_SUB:(t + 1) * SWA_SUB, :]
        s_scr[slot, rows_of(idx), :] = sj
        return jnp.max(sj.reshape(SWA_SUB // SUBLANE, SUBLANE, tq), axis=0)

    def values_t(g, t, t0):
        if t < n_band:
            tiles = [vlt_ref[0, g, t0 + t * lanes_per_sub + i] for i in range(lanes_per_sub)]
        else:
            tiles = [vct_ref[0, g, (t - n_band) * lanes_per_sub + i] for i in range(lanes_per_sub)]
        return jnp.concatenate(tiles, axis=1)

    def next_scores(slot, idx, nxt, m8n):
        q_ref, g_n, win_n = nxt
        mj = scores_into(slot, idx, q_ref, g_n, win_n)
        hh, t = subs[idx]
        m8n[hh] = mj if t == 0 else jnp.maximum(m8n[hh], mj)
        if t == n_t - 1:
            m8_scr[slot, hh] = m8n[hh]

    def stage_body(g, win_cur, nxt):
        slot = g
        t0 = win_cur[0] // LANE
        n_q = len(subs) if nxt is not None else 0
        m8n = {}
        for idx in range(min(SWA_QK_LEAD, n_q)):
            next_scores(1 - slot, idx, nxt, m8n)
        for idx, (hh, t) in enumerate(subs):
            hd = g * SWA_GROUP + hh
            if t == 0:
                sk = sink_ref[hd] * LOG2E
                m = jnp.maximum(jnp.max(m8_scr[slot, hh], axis=0, keepdims=True), sk)
                mb = jnp.broadcast_to(m, (SWA_CHUNK, tq))
                acc = None
            base = idx * SWA_SUB
            p = jnp.concatenate(
                [jnp.exp2(s_scr[slot, r:r + SWA_CHUNK, :] - mb).astype(BF16)
                 for r in range(base, base + SWA_SUB, SWA_CHUNK)], axis=0)
            d = _dot(values_t(g, t, t0), p)
            acc = d if acc is None else acc + d
            if t == n_t - 1:
                denom = acc[SWA_HD:SWA_HD + 1] + jnp.exp2(sk - m)
                ot_scr[hd * SWA_HD:(hd + 1) * SWA_HD, :] = acc[0:SWA_HD] / denom
            if idx + SWA_QK_LEAD < n_q:
                next_scores(1 - slot, idx + SWA_QK_LEAD, nxt, m8n)

    win = window(j)

    @pl.when(j == 0)
    def _():
        m8n = {}
        for idx in range(len(subs)):
            next_scores(0, idx, (qt_ref, 0, win), m8n)

    stage_body(0, win, (qt_ref, 1, win))

    @pl.when(j < nt - 1)
    def _():
        stage_body(1, win, (qtn_ref, 0, window(j + 1)))

    @pl.when(j == nt - 1)
    def _():
        stage_body(1, win, None)

    o_ref[0] = ot_scr[...].T.astype(BF16)


def _swa_bias(seq):
    nt = seq // TOK_TILE
    out = []
    for j in (0, 1, nt - 1):
        start = min(max(j * TOK_TILE - WINDOW, 0), seq - SWA_WIN)
        qpos = j * TOK_TILE + np.arange(TOK_TILE)[None, :]
        kpos = start + np.arange(SWA_WIN)[:, None]
        out.append(np.where(np.abs(qpos - kpos) <= WINDOW, 0.0, NEG_INF))
    return jnp.asarray(np.stack(out), F32)


def _swa_attention(sink, qst, ks_c, ks_l, vst_c, vst_l):
    bsz, _, _, s = qst.shape
    n_ctx = ks_c.shape[1]
    nt = s // TOK_TILE
    assert nt >= 3 and s >= SWA_WIN and n_ctx % SWA_SUB == 0
    stage_rows = SWA_GROUP * (SWA_WIN + n_ctx)
    return pl.pallas_call(
        functools.partial(_swa_kernel, seq=s),
        grid=(bsz, nt),
        in_specs=[
            pl.BlockSpec(memory_space=pltpu.SMEM),
            pl.BlockSpec((3, SWA_WIN, TOK_TILE), lambda b, j: (0, 0, 0)),
            pl.BlockSpec((1, SWA_HEADS, LANE, TOK_TILE), lambda b, j: (b, 0, 0, j)),
            pl.BlockSpec((1, SWA_GROUP, LANE, TOK_TILE), lambda b, j: (b, 0, 0, jnp.minimum(j + 1, nt - 1))),
            pl.BlockSpec((1, n_ctx, LANE), lambda b, j: (b, 0, 0)),
            pl.BlockSpec((1, s, LANE), lambda b, j: (b, 0, 0)),
            pl.BlockSpec((1, SWA_KV_HEADS, n_ctx // LANE, SWA_V_AUG, LANE), lambda b, j: (b, 0, 0, 0, 0)),
            pl.BlockSpec((1, SWA_KV_HEADS, s // LANE, SWA_V_AUG, LANE), lambda b, j: (b, 0, 0, 0, 0)),
        ],
        out_specs=pl.BlockSpec((1, TOK_TILE, SWA_HEADS * SWA_HD), lambda b, j: (b, j, 0)),
        out_shape=jax.ShapeDtypeStruct((bsz, s, SWA_HEADS * SWA_HD), BF16),
        scratch_shapes=[
            pltpu.VMEM((SWA_HEADS * SWA_HD, TOK_TILE), F32),
            pltpu.VMEM((2, stage_rows, TOK_TILE), F32),
            pltpu.VMEM((2, SWA_GROUP, SUBLANE, TOK_TILE), F32),
        ],
        compiler_params=pltpu.CompilerParams(vmem_limit_bytes=VMEM_LIMIT,
                                             dimension_semantics=("arbitrary", "arbitrary")),
        name="swa_attention",
    )(sink, _swa_bias(s), qst, qst, ks_c, ks_l, vst_c, vst_l)


def _mlp_kernel(x_ref, mm_ref, ms_ref, mod_ref, wo_ref, gmlp_ref, w1_ref, w2_ref, o_ref, acc_ref):
    half = MLA_HEADS * MLA_V
    y1 = _dot(mm_ref[0], wo_ref[0:half, :]) + _dot(ms_ref[0], wo_ref[half:, :])
    x1 = x_ref[0] + mod_ref[0, 2:3, :] * y1
    h2 = (_rms_rows(x1, D_MODEL) * gmlp_ref[...] * (1.0 + mod_ref[0, 4:5, :])
          + mod_ref[0, 3:4, :]).astype(BF16)
    for c in range(D_FF // FF_CHUNK):
        a = jnp.maximum(_dot(h2, w1_ref[:, c * FF_CHUNK:(c + 1) * FF_CHUNK]), 0.0)
        part = _dot((a * a).astype(BF16), w2_ref[c * FF_CHUNK:(c + 1) * FF_CHUNK, :])
        if c == 0:
            acc_ref[...] = part
        else:
            acc_ref[...] += part
    o_ref[0] = x1 + mod_ref[0, 5:6, :] * acc_ref[...]


def _outproj_mlp(x, mix_m, mix_s, mod3, w_out, g_mlp, w1, w2):
    bsz, s, _ = x.shape
    half = MLA_HEADS * MLA_V
    const = lambda b, j: (0, 0)
    single = pl.Buffered(1)
    return pl.pallas_call(
        _mlp_kernel,
        grid=(bsz, s // MLP_TILE),
        in_specs=[
            pl.BlockSpec((1, MLP_TILE, D_MODEL), lambda b, j: (b, j, 0)),
            pl.BlockSpec((1, MLP_TILE, half), lambda b, j: (b, j, 0)),
            pl.BlockSpec((1, MLP_TILE, half), lambda b, j: (b, j, 0)),
            pl.BlockSpec((1, 6, D_MODEL), lambda b, j: (b, 0, 0)),
            pl.BlockSpec((D_MODEL, D_MODEL), const, pipeline_mode=single),
            pl.BlockSpec((1, D_MODEL), const),
            pl.BlockSpec((D_MODEL, D_FF), const, pipeline_mode=single),
            pl.BlockSpec((D_FF, D_MODEL), const, pipeline_mode=single),
        ],
        out_specs=pl.BlockSpec((1, MLP_TILE, D_MODEL), lambda b, j: (b, j, 0)),
        out_shape=jax.ShapeDtypeStruct((bsz, s, D_MODEL), F32),
        scratch_shapes=[pltpu.VMEM((MLP_TILE, D_MODEL), F32)],
        compiler_params=pltpu.CompilerParams(vmem_limit_bytes=VMEM_LIMIT),
        name="outproj_mlp",
    )(x, mix_m, mix_s, mod3, w_out, g_mlp, w1, w2)


def _rope_tables(seq):
    f32 = np.float32

    def cos_sin(rot_dim):
        n_freq = rot_dim // 4
        inv = f32(ROPE_THETA) ** (-np.arange(n_freq, dtype=f32) / f32(n_freq))
        rows = seq // GRID_W
        row = np.repeat(np.arange(rows, dtype=f32), GRID_W)
        col = np.tile(np.arange(GRID_W, dtype=f32), rows)
        ang = np.concatenate([row[:, None] * inv, col[:, None] * inv], axis=-1).astype(f32)
        return np.cos(ang).astype(f32), np.sin(ang).astype(f32)

    mcos, msin = cos_sin(MLA_ROPE)
    scos, ssin = cos_sin(SWA_HD)
    z16 = np.zeros((seq, 16), f32)
    z32 = np.zeros((seq, 32), f32)
    z64 = np.zeros((seq, 64), f32)
    kc = np.concatenate([np.ones((seq, 64), f32), mcos, mcos, np.ones((seq, 32), f32)], axis=1)
    ks1 = np.concatenate([z64, -msin, z16, z32], axis=1)
    ks2 = np.concatenate([z64, z16, msin, z32], axis=1)
    sc = np.concatenate([scos, scos, scos, scos], axis=1)
    ss1 = np.concatenate([-ssin, z32, -ssin, z32], axis=1)
    ss2 = np.concatenate([z32, ssin, z32, ssin], axis=1)
    tabs = dict(mcos=mcos.T, msin=msin.T, scos=scos.T, ssin=ssin.T,
                kc=kc, ks1=ks1, ks2=ks2, sc=sc, ss1=ss1, ss2=ss2)
    return {k: jnp.asarray(np.ascontiguousarray(v)) for k, v in tabs.items()}


def _prep_weights(g_attn, w_in, g_q_a, w_uq, g_kv_a, w_ukv, g_mla_q, g_mla_k, g_swa_q, g_swa_k):
    o_ckv = Q_LORA
    o_kr = o_ckv + KV_LORA
    o_qs = o_kr + MLA_ROPE
    o_ks = o_qs + SWA_HEADS * SWA_HD
    o_vs = o_ks + SWA_KV_HEADS * SWA_HD
    w_cq = w_in[:, :o_ckv]
    w_ckv = w_in[:, o_ckv:o_kr]
    w_kr = w_in[:, o_kr:o_qs]
    w_qs = w_in[:, o_qs:o_ks]
    w_ks = w_in[:, o_ks:o_vs]
    w_vs = w_in[:, o_vs:]
    w_kr_p = jnp.pad(w_kr, ((0, 0), (MLA_NOPE, LANE - MLA_QK)))
    wtok = jnp.concatenate([w_cq, w_ckv, w_kr_p, w_ks], axis=1).astype(BF16)
    w_ukv_h = w_ukv.reshape(KV_LORA, MLA_HEADS, MLA_NOPE + MLA_V)
    wuk = jnp.pad(w_ukv_h[:, :, :MLA_NOPE], ((0, 0), (0, 0), (0, LANE - MLA_NOPE)))
    wuk = wuk.reshape(KV_LORA, MLA_HEADS * LANE).astype(BF16)
    wuv = w_ukv_h[:, :, MLA_NOPE:].reshape(KV_LORA, MLA_HEADS * MLA_V).T.astype(BF16)
    w_uq_h = jnp.pad(w_uq.reshape(Q_LORA, MLA_HEADS, MLA_QK), ((0, 0), (0, 0), (0, LANE - MLA_QK)))
    wuq = w_uq_h.reshape(Q_LORA, MLA_HEADS * LANE).T.astype(BF16)
    return dict(
        gattn=g_attn[None, :], wtok=wtok, wvs=w_vs.T.astype(BF16), gkva=g_kv_a[None, :],
        wuk=wuk, wuv=wuv, gmk=jnp.pad(g_mla_k, (0, LANE - MLA_QK))[None, :],
        gsk=jnp.tile(g_swa_k, SWA_KV_HEADS)[None, :],
        wqs=w_qs.T.astype(BF16), gqa=g_q_a[None, :], wuq=wuq,
        gmq=jnp.pad(g_mla_q, (0, LANE - MLA_QK))[:, None], gsq=g_swa_q[:, None],
    )


def kernel(x, c, ctx, c_ctx, w_mod, b_mod, g_attn, w_in, g_q_a, w_uq, g_kv_a, w_ukv, g_mla_q, g_mla_k,
           g_swa_q, g_swa_k, swa_sink, w_out, g_mlp, w_mlp1, w_mlp2):
    bsz, seq, _ = x.shape
    assert w_mod.shape[0] == 1, "single-layer block"
    assert bsz < SUBLANE and seq % PROJ_TILE == 0 and seq % MLP_TILE == 0 and ctx.shape[1] % TOK_TILE == 0

    cond = jnp.zeros((SUBLANE, D_MODEL), F32).at[:bsz].set(c).at[bsz].set(c_ctx)
    mod = _modulation(cond, w_mod[0], b_mod[0][None, :])
    mod3 = mod.reshape(SUBLANE, 6, D_MODEL)

    weights = _prep_weights(g_attn[0], w_in[0], g_q_a[0], w_uq[0], g_kv_a[0], w_ukv[0],
                            g_mla_q[0], g_mla_k[0], g_swa_q[0], g_swa_k[0])
    tables = _rope_tables(seq)

    weights["to_cast"] = [w_out[0], w_mlp1[0], w_mlp2[0]]
    (km_l, vmt_l, ks_l, vst_l, qmt, qst,
     w_out_bf16, w_mlp1_bf16, w_mlp2_bf16) = _project(x, mod3, None, weights, tables, latent=True, tile=PROJ_TILE)
    km_c, vmt_c, ks_c, vst_c = _project(ctx, mod3, bsz, weights, None, latent=False, tile=TOK_TILE)

    mix_m = _mla_attention(qmt, km_c, km_l, vmt_c, vmt_l)
    mix_s = _swa_attention(swa_sink[0], qst, ks_c, ks_l, vst_c, vst_l)

    return _outproj_mlp(x, mix_m, mix_s, mod3, w_out_bf16, g_mlp[0][None, :], w_mlp1_bf16, w_mlp2_bf16)
```

```python
import functools
import math

import jax
import jax.numpy as jnp
import numpy as np
from jax import lax
from jax.experimental import pallas as pl
from jax.experimental.pallas import tpu as pltpu

D_MODEL = 1024
GRID_W = 64
MLA_HEADS = 8
MLA_NOPE = 64
MLA_ROPE = 32
MLA_QK = MLA_NOPE + MLA_ROPE
MLA_V = 64
MLA_V_AUG = MLA_V + 16
Q_LORA = 256
KV_LORA = 128
SWA_HEADS = 8
SWA_KV_HEADS = 2
SWA_GROUP = SWA_HEADS // SWA_KV_HEADS
SWA_HD = 64
SWA_V_AUG = SWA_HD + 16
WINDOW = 128
D_FF = 4 * D_MODEL
ROPE_THETA = 10000.0
EPS = 1e-6
NEG_INF = -1e30
LOG2E = 1.4426950408889634
MLA_QSCALE = LOG2E / math.sqrt(MLA_QK)
SWA_QSCALE = LOG2E / math.sqrt(SWA_HD)

LANE = 128
SUBLANE = 8
MXU_DIM = 256
TOK_TILE = MXU_DIM
PROJ_TILE = 1024
PROJ_PART = 256
MOD_TILE = 1024
MLP_TILE = 512
FF_CHUNK = 1024
SWA_WIN = 2 * TOK_TILE
MLA_SUB = MXU_DIM
MLA_Q_TILES = 2
SWA_SUB = MXU_DIM
MLA_CHUNK = 32
SWA_CHUNK = 32
MLA_QK_LEAD = 3
SWA_QK_LEAD = 1
VMEM_LIMIT = 56 * 1024 * 1024

F32 = jnp.float32
BF16 = jnp.bfloat16
NT_DIMS = (((1,), (1,)), ((), ()))


def _dot(a, b):
    return jnp.dot(a, b, preferred_element_type=F32)


def _dot_nt(a, b):
    return lax.dot_general(a, b, NT_DIMS, preferred_element_type=F32)


def _mod_kernel(cond_ref, w_ref, b_ref, o_ref):
    cnd = cond_ref[...]
    act = cnd * jax.nn.sigmoid(cnd)
    o_ref[...] = _dot(act.astype(BF16), w_ref[...].astype(BF16)) + b_ref[...]


def _modulation(cond, w_mod, b_mod):
    n = w_mod.shape[1]
    tn = MOD_TILE
    return pl.pallas_call(
        _mod_kernel,
        grid=(n // tn,),
        in_specs=[
            pl.BlockSpec((SUBLANE, D_MODEL), lambda i: (0, 0)),
            pl.BlockSpec((D_MODEL, tn), lambda i: (0, i)),
            pl.BlockSpec((1, tn), lambda i: (0, i)),
        ],
        out_specs=pl.BlockSpec((SUBLANE, tn), lambda i: (0, i)),
        out_shape=jax.ShapeDtypeStruct((SUBLANE, n), F32),
        name="modulation",
    )(cond, w_mod, b_mod)


def _rms_rows(x, n):
    return x * lax.rsqrt(jnp.sum(x * x, axis=-1, keepdims=True) * (1.0 / n) + EPS)


def _rms_cols(x, n):
    return x * lax.rsqrt(jnp.sum(x * x, axis=0, keepdims=True) * (1.0 / n) + EPS)


def _proj_kernel(*refs, latent):
    if latent:
        (x_ref, mod_ref, gattn_ref, wtok_ref, wvs_ref, gkva_ref, wuk_ref, wuv_ref, gmk_ref, gsk_ref,
         wqs_ref, gqa_ref, wuq_ref, gmq_ref, gsq_ref,
         mcos_ref, msin_ref, scos_ref, ssin_ref, kc_ref, ks1_ref, ks2_ref, sc_ref, ss1_ref, ss2_ref,
         wo_f32_ref, w1_f32_ref, w2_f32_ref,
         km_ref, vmt_ref, ks_ref, vst_ref, qmt_ref, qst_ref,
         wo_bf16_ref, w1_bf16_ref, w2_bf16_ref) = refs
        for src, dst in ((wo_f32_ref, wo_bf16_ref), (w1_f32_ref, w1_bf16_ref), (w2_f32_ref, w2_bf16_ref)):
            dst[...] = src[...].astype(BF16)
    else:
        (x_ref, mod_ref, gattn_ref, wtok_ref, wvs_ref, gkva_ref, wuk_ref, wuv_ref, gmk_ref, gsk_ref,
         km_ref, vmt_ref, ks_ref, vst_ref) = refs

    tile = x_ref.shape[1]
    n_part = max(1, tile // PROJ_PART)
    part = tile // n_part

    gain = gattn_ref[...] * (1.0 + mod_ref[0, 1:2, :])
    shift = mod_ref[0, 0:1, :]
    gmk = gmk_ref[...]
    ones_row = (lax.broadcasted_iota(jnp.int32, (MLA_V_AUG - MLA_V, part), 0) == 0).astype(BF16)
    if latent:
        gmq = gmq_ref[...] * MLA_QSCALE
        gsq = gsq_ref[...] * SWA_QSCALE

    def part_phases(lo):
        rows = slice(lo, lo + part)

        h = (_rms_rows(x_ref[0, rows, :], D_MODEL) * gain + shift).astype(BF16)
        yield

        zt = _dot(h, wtok_ref[...])
        vst = _dot_nt(wvs_ref[...], h)
        if latent:
            qst = _dot_nt(wqs_ref[...], h)
        yield

        cq = zt[:, 0:256]
        ckv = zt[:, 256:384]
        krp = zt[:, 384:512]
        ksw = zt[:, 512:640]
        ckvn = (_rms_rows(ckv, KV_LORA) * gkva_ref[...]).astype(BF16)
        if latent:
            cqn = (_rms_rows(cq, Q_LORA) * gqa_ref[...]).astype(BF16)
        kr = krp * gmk
        if latent:
            kr = (kr * kc_ref[rows, :] + pltpu.roll(kr, 112, 1) * ks1_ref[rows, :]
                  + pltpu.roll(kr, 16, 1) * ks2_ref[rows, :])
        ss_kr = jnp.sum(krp * krp, axis=-1, keepdims=True)
        first = lax.broadcasted_iota(jnp.int32, ksw.shape, 1) < SWA_HD
        sq = ksw * ksw
        ss0 = jnp.sum(jnp.where(first, sq, 0.0), axis=-1, keepdims=True)
        ss1 = jnp.sum(jnp.where(first, 0.0, sq), axis=-1, keepdims=True)
        rr = jnp.where(first, lax.rsqrt(ss0 * (1.0 / SWA_HD) + EPS), lax.rsqrt(ss1 * (1.0 / SWA_HD) + EPS))
        ksn = ksw * rr * gsk_ref[...]
        if latent:
            ksn = (ksn * sc_ref[rows, :] + pltpu.roll(ksn, 96, 1) * ss1_ref[rows, :]
                   + pltpu.roll(ksn, 32, 1) * ss2_ref[rows, :])
        ks_ref[0, rows, :] = ksn.astype(BF16)
        for g in range(SWA_KV_HEADS):
            for t in range(part // LANE):
                dst = lo // LANE + t
                vst_ref[0, g, dst, 0:SWA_HD, :] = vst[g * SWA_HD:(g + 1) * SWA_HD, t * LANE:(t + 1) * LANE].astype(BF16)
                vst_ref[0, g, dst, SWA_HD:SWA_V_AUG, :] = ones_row[:, 0:LANE]
        yield

        kpre = _dot(ckvn, wuk_ref[...])
        vt = _dot_nt(wuv_ref[...], ckvn)
        if latent:
            qt = _dot_nt(wuq_ref[...], cqn)
        yield

        for hd in range(MLA_HEADS):
            kp = kpre[:, hd * LANE:(hd + 1) * LANE]
            ss = jnp.sum(kp * kp, axis=-1, keepdims=True) + ss_kr
            kn = (kp * gmk + kr) * lax.rsqrt(ss * (1.0 / MLA_QK) + EPS)
            km_ref[0, hd, rows, :] = kn.astype(BF16)
            vmt_ref[0, hd, 0:MLA_V, rows] = vt[hd * MLA_V:(hd + 1) * MLA_V].astype(BF16)
            vmt_ref[0, hd, MLA_V:MLA_V_AUG, rows] = ones_row
        if latent:
            mcos = mcos_ref[:, rows]
            msin = msin_ref[:, rows]
            for hd in range(MLA_HEADS):
                qn = _rms_cols(qt[hd * LANE:(hd + 1) * LANE], MLA_QK) * gmq
                x1 = qn[64:80]
                x2 = qn[80:96]
                qmt_ref[0, hd, 0:64, rows] = qn[0:64].astype(BF16)
                qmt_ref[0, hd, 64:80, rows] = (x1 * mcos - x2 * msin).astype(BF16)
                qmt_ref[0, hd, 80:96, rows] = (x2 * mcos + x1 * msin).astype(BF16)
                qmt_ref[0, hd, 96:128, rows] = jnp.zeros((32, part), BF16)
            scos = scos_ref[:, rows]
            ssin = ssin_ref[:, rows]
            for hd in range(SWA_HEADS):
                qn = _rms_cols(qst[hd * SWA_HD:(hd + 1) * SWA_HD], SWA_HD) * gsq
                x1 = qn[0:32]
                x2 = qn[32:64]
                g = hd // SWA_GROUP
                base = g * SWA_HD
                other = (1 - g) * SWA_HD
                qst_ref[0, hd, base:base + 32, rows] = (x1 * scos - x2 * ssin).astype(BF16)
                qst_ref[0, hd, base + 32:base + 64, rows] = (x2 * scos + x1 * ssin).astype(BF16)
                qst_ref[0, hd, other:other + 64, rows] = jnp.zeros((64, part), BF16)
        yield

    n_phase = 5
    parts = [part_phases(i * part) for i in range(n_part)]
    for step in range(n_phase + n_part - 1):
        for i, gen in enumerate(parts):
            if 0 <= step - i < n_phase:
                next(gen)


def _full(shape):
    nd = len(shape)
    return pl.BlockSpec(shape, lambda b, j: (0,) * nd)


def _project(x, mod3, mod_row, weights, tables, latent, tile):
    bsz, n, _ = x.shape
    nt = n // tile
    if mod_row is None:
        mod_map = lambda b, j: (b, 0, 0)
    else:
        mod_map = lambda b, j: (mod_row, 0, 0)
    common = [weights[k] for k in ("gattn", "wtok", "wvs", "gkva", "wuk", "wuv", "gmk", "gsk")]
    ins = [x, mod3] + common
    in_specs = [
        pl.BlockSpec((1, tile, D_MODEL), lambda b, j: (b, j, 0)),
        pl.BlockSpec((1, 6, D_MODEL), mod_map),
    ] + [_full(w.shape) for w in common]
    out_shape = [
        jax.ShapeDtypeStruct((bsz, MLA_HEADS, n, LANE), BF16),
        jax.ShapeDtypeStruct((bsz, MLA_HEADS, MLA_V_AUG, n), BF16),
        jax.ShapeDtypeStruct((bsz, n, LANE), BF16),
        jax.ShapeDtypeStruct((bsz, SWA_KV_HEADS, n // LANE, SWA_V_AUG, LANE), BF16),
    ]
    out_specs = [
        pl.BlockSpec((1, MLA_HEADS, tile, LANE), lambda b, j: (b, 0, j, 0)),
        pl.BlockSpec((1, MLA_HEADS, MLA_V_AUG, tile), lambda b, j: (b, 0, 0, j)),
        pl.BlockSpec((1, tile, LANE), lambda b, j: (b, j, 0)),
        pl.BlockSpec((1, SWA_KV_HEADS, tile // LANE, SWA_V_AUG, LANE), lambda b, j: (b, 0, j, 0, 0)),
    ]
    if latent:
        extra = [weights[k] for k in ("wqs", "gqa", "wuq", "gmq", "gsq")]
        ins += extra
        in_specs += [_full(w.shape) for w in extra]
        feat = [tables[k] for k in ("mcos", "msin", "scos", "ssin")]
        ins += feat
        in_specs += [pl.BlockSpec((t.shape[0], tile), lambda b, j: (0, j)) for t in feat]
        tok = [tables[k] for k in ("kc", "ks1", "ks2", "sc", "ss1", "ss2")]
        ins += tok
        in_specs += [pl.BlockSpec((tile, LANE), lambda b, j: (j, 0)) for _ in tok]
        out_shape += [
            jax.ShapeDtypeStruct((bsz, MLA_HEADS, LANE, n), BF16),
            jax.ShapeDtypeStruct((bsz, SWA_HEADS, LANE, n), BF16),
        ]
        out_specs += [
            pl.BlockSpec((1, MLA_HEADS, LANE, tile), lambda b, j: (b, 0, 0, j)),
            pl.BlockSpec((1, SWA_HEADS, LANE, tile), lambda b, j: (b, 0, 0, j)),
        ]
        steps = bsz * nt
        for w in weights["to_cast"]:
            rows = w.shape[0] // steps
            assert rows * steps == w.shape[0] and rows % 16 == 0
            ins.append(w)
            in_specs.append(pl.BlockSpec((rows, w.shape[1]), lambda b, j: (b * nt + j, 0)))
            out_shape.append(jax.ShapeDtypeStruct(w.shape, BF16))
            out_specs.append(pl.BlockSpec((rows, w.shape[1]), lambda b, j: (b * nt + j, 0)))
    return pl.pallas_call(
        functools.partial(_proj_kernel, latent=latent),
        grid=(bsz, nt),
        in_specs=in_specs,
        out_specs=out_specs,
        out_shape=out_shape,
        compiler_params=pltpu.CompilerParams(vmem_limit_bytes=VMEM_LIMIT),
        name="project_latent" if latent else "project_context",
    )(*ins)


def _mla_kernel(qt_ref, kc_ref, kl_ref, vct_ref, vlt_ref, o_ref, ot_scr, s_scr, *, seq):
    n_ctx = kc_ref.shape[2]
    tq = TOK_TILE
    subs = ([(kc_ref, vct_ref, o) for o in range(0, n_ctx, MLA_SUB)]
            + [(kl_ref, vlt_ref, o) for o in range(0, seq, MLA_SUB)])
    n_sub = len(subs)

    def scores_into(slot, j, stage, m8):
        hd, t = stage
        k_ref, _, off = subs[j]
        sj = _dot(k_ref[0, hd, off:off + MLA_SUB, :], qt_ref[0, hd, :, t * tq:(t + 1) * tq])
        s_scr[slot, j * MLA_SUB:(j + 1) * MLA_SUB, :] = sj
        mj = jnp.max(sj.reshape(MLA_SUB // SUBLANE, SUBLANE, tq), axis=0)
        return mj if m8 is None else jnp.maximum(m8, mj)

    def stage_body(stage, m8_cur, nxt):
        hd, slot = stage
        mb = jnp.broadcast_to(jnp.max(m8_cur, axis=0, keepdims=True), (MLA_CHUNK, tq))
        n_q = n_sub if nxt is not None else 0
        m8_next = None
        acc = None
        for j in range(min(MLA_QK_LEAD, n_q)):
            m8_next = scores_into(1 - slot, j, nxt, m8_next)
        for j in range(n_sub):
            base = j * MLA_SUB
            p = jnp.concatenate(
                [jnp.exp2(s_scr[slot, r:r + MLA_CHUNK, :] - mb).astype(BF16)
                 for r in range(base, base + MLA_SUB, MLA_CHUNK)], axis=0)
            _, v_ref, off = subs[j]
            d = _dot(v_ref[0, hd, :, off:off + MLA_SUB], p)
            acc = d if acc is None else acc + d
            if j + MLA_QK_LEAD < n_q:
                m8_next = scores_into(1 - slot, j + MLA_QK_LEAD, nxt, m8_next)
        row = pl.multiple_of(hd * MLA_V, MLA_V)
        ot_scr[pl.ds(row, MLA_V), slot * tq:(slot + 1) * tq] = acc[0:MLA_V] / acc[MLA_V:MLA_V + 1]
        return m8_next

    def stage_step(n, m8):
        hd = n // MLA_Q_TILES
        return lax.cond(lax.rem(n, MLA_Q_TILES) == 0,
                        lambda m: stage_body((hd, 0), m, (hd, 1)),
                        lambda m: stage_body((hd, 1), m, (hd + 1, 0)), m8)

    assert MLA_Q_TILES == 2
    n_stage = MLA_HEADS * MLA_Q_TILES
    m8_first = None
    for j in range(n_sub):
        m8_first = scores_into(0, j, (0, 0), m8_first)
    m8_last = lax.fori_loop(0, n_stage - 1, stage_step, m8_first)
    stage_body((MLA_HEADS - 1, 1), m8_last, None)
    o_ref[0] = ot_scr[...].T.astype(BF16)


def _mla_attention(qmt, km_c, km_l, vmt_c, vmt_l):
    bsz, _, _, s = qmt.shape
    n_ctx = km_c.shape[2]
    step_q = MLA_Q_TILES * TOK_TILE
    assert s % step_q == 0
    n_keys = n_ctx + s
    return pl.pallas_call(
        functools.partial(_mla_kernel, seq=s),
        grid=(bsz, s // step_q),
        in_specs=[
            pl.BlockSpec((1, MLA_HEADS, LANE, step_q), lambda b, j: (b, 0, 0, j)),
            pl.BlockSpec((1, MLA_HEADS, n_ctx, LANE), lambda b, j: (b, 0, 0, 0)),
            pl.BlockSpec((1, MLA_HEADS, s, LANE), lambda b, j: (b, 0, 0, 0)),
            pl.BlockSpec((1, MLA_HEADS, MLA_V_AUG, n_ctx), lambda b, j: (b, 0, 0, 0)),
            pl.BlockSpec((1, MLA_HEADS, MLA_V_AUG, s), lambda b, j: (b, 0, 0, 0)),
        ],
        out_specs=pl.BlockSpec((1, step_q, MLA_HEADS * MLA_V), lambda b, j: (b, j, 0)),
        out_shape=jax.ShapeDtypeStruct((bsz, s, MLA_HEADS * MLA_V), BF16),
        scratch_shapes=[
            pltpu.VMEM((MLA_HEADS * MLA_V, step_q), F32),
            pltpu.VMEM((2, n_keys, TOK_TILE), F32),
        ],
        compiler_params=pltpu.CompilerParams(vmem_limit_bytes=VMEM_LIMIT,
                                             dimension_semantics=("arbitrary", "arbitrary")),
        name="mla_attention",
    )(qmt, km_c, km_l, vmt_c, vmt_l)


def _swa_kernel(sink_ref, bias_ref, qt_ref, qtn_ref, kc_ref, kl_ref, vct_ref, vlt_ref, o_ref,
                ot_scr, s_scr, m8_scr, *, seq):
    j = pl.program_id(1)
    nt = pl.num_programs(1)
    n_ctx = kc_ref.shape[1]
    tq = qt_ref.shape[3]
    n_band = SWA_WIN // SWA_SUB
    n_t = n_band + n_ctx // SWA_SUB
    lanes_per_sub = SWA_SUB // LANE
    subs = [(hh, t) for hh in range(SWA_GROUP) for t in range(n_t)]

    def window(jj):
        start = pl.multiple_of(jnp.clip(jj * TOK_TILE - WINDOW, 0, seq - SWA_WIN), LANE)
        variant = jnp.where(jj == 0, 0, jnp.where(jj == nt - 1, 2, 1))
        return start, variant

    def rows_of(idx):
        return slice(idx * SWA_SUB, (idx + 1) * SWA_SUB)

    def scores_into(slot, idx, q_ref, g, win):
        hh, t = subs[idx]
        start, variant = win
        if t < n_band:
            k = kl_ref[0, pl.ds(start + t * SWA_SUB, SWA_SUB), :]
        else:
            k = kc_ref[0, (t - n_band) * SWA_SUB:(t - n_band + 1) * SWA_SUB, :]
        sj = _dot(k, q_ref[0, g * SWA_GROUP + hh])
        if t < n_band:
            sj = sj + bias_ref[variant, t * SWA_SUB:(t + 1) * SWA_SUB, :]
        s_scr[slot, rows_of(idx), :] = sj
        return jnp.max(sj.reshape(SWA_SUB // SUBLANE, SUBLANE, tq), axis=0)

    def values_t(g, t, t0):
        if t < n_band:
            tiles = [vlt_ref[0, g, t0 + t * lanes_per_sub + i] for i in range(lanes_per_sub)]
        else:
            tiles = [vct_ref[0, g, (t - n_band) * lanes_per_sub + i] for i in range(lanes_per_sub)]
        return jnp.concatenate(tiles, axis=1)

    def next_scores(slot, idx, nxt, m8n):
        q_ref, g_n, win_n = nxt
        mj = scores_into(slot, idx, q_ref, g_n, win_n)
        hh, t = subs[idx]
        m8n[hh] = mj if t == 0 else jnp.maximum(m8n[hh], mj)
        if t == n_t - 1:
            m8_scr[slot, hh] = m8n[hh]

    def stage_body(g, win_cur, nxt):
        slot = g
        t0 = win_cur[0] // LANE
        n_q = len(subs) if nxt is not None else 0
        m8n = {}
        for idx in range(min(SWA_QK_LEAD, n_q)):
            next_scores(1 - slot, idx, nxt, m8n)
        for idx, (hh, t) in enumerate(subs):
            hd = g * SWA_GROUP + hh
            if t == 0:
                sk = sink_ref[hd] * LOG2E
                m = jnp.maximum(jnp.max(m8_scr[slot, hh], axis=0, keepdims=True), sk)
                mb = jnp.broadcast_to(m, (SWA_CHUNK, tq))
                acc = None
            base = idx * SWA_SUB
            p = jnp.concatenate(
                [jnp.exp2(s_scr[slot, r:r + SWA_CHUNK, :] - mb).astype(BF16)
                 for r in range(base, base + SWA_SUB, SWA_CHUNK)], axis=0)
            d = _dot(values_t(g, t, t0), p)
            acc = d if acc is None else acc + d
            if t == n_t - 1:
                denom = acc[SWA_HD:SWA_HD + 1] + jnp.exp2(sk - m)
                ot_scr[hd * SWA_HD:(hd + 1) * SWA_HD, :] = acc[0:SWA_HD] / denom
            if idx + SWA_QK_LEAD < n_q:
                next_scores(1 - slot, idx + SWA_QK_LEAD, nxt, m8n)

    win = window(j)

    @pl.when(j == 0)
    def _():
        m8n = {}
        for idx in range(len(subs)):
            next_scores(0, idx, (qt_ref, 0, win), m8n)

    stage_body(0, win, (qt_ref, 1, win))

    @pl.when(j < nt - 1)
    def _():
        stage_body(1, win, (qtn_ref, 0, window(j + 1)))

    @pl.when(j == nt - 1)
    def _():
        stage_body(1, win, None)

    o_ref[0] = ot_scr[...].T.astype(BF16)


def _swa_bias(seq):
    nt = seq // TOK_TILE
    out = []
    for j in (0, 1, nt - 1):
        start = min(max(j * TOK_TILE - WINDOW, 0), seq - SWA_WIN)
        qpos = j * TOK_TILE + np.arange(TOK_TILE)[None, :]
        kpos = start + np.arange(SWA_WIN)[:, None]
        out.append(np.where(np.abs(qpos - kpos) <= WINDOW, 0.0, NEG_INF))
    return jnp.asarray(np.stack(out), F32)


def _swa_attention(sink, qst, ks_c, ks_l, vst_c, vst_l):
    bsz, _, _, s = qst.shape
    n_ctx = ks_c.shape[1]
    nt = s // TOK_TILE
    assert nt >= 3 and s >= SWA_WIN and n_ctx % SWA_SUB == 0
    stage_rows = SWA_GROUP * (SWA_WIN + n_ctx)
    return pl.pallas_call(
        functools.partial(_swa_kernel, seq=s),
        grid=(bsz, nt),
        in_specs=[
            pl.BlockSpec(memory_space=pltpu.SMEM),
            pl.BlockSpec((3, SWA_WIN, TOK_TILE), lambda b, j: (0, 0, 0)),
            pl.BlockSpec((1, SWA_HEADS, LANE, TOK_TILE), lambda b, j: (b, 0, 0, j)),
            pl.BlockSpec((1, SWA_GROUP, LANE, TOK_TILE), lambda b, j: (b, 0, 0, jnp.minimum(j + 1, nt - 1))),
            pl.BlockSpec((1, n_ctx, LANE), lambda b, j: (b, 0, 0)),
            pl.BlockSpec((1, s, LANE), lambda b, j: (b, 0, 0)),
            pl.BlockSpec((1, SWA_KV_HEADS, n_ctx // LANE, SWA_V_AUG, LANE), lambda b, j: (b, 0, 0, 0, 0)),
            pl.BlockSpec((1, SWA_KV_HEADS, s // LANE, SWA_V_AUG, LANE), lambda b, j: (b, 0, 0, 0, 0)),
        ],
        out_specs=pl.BlockSpec((1, TOK_TILE, SWA_HEADS * SWA_HD), lambda b, j: (b, j, 0)),
        out_shape=jax.ShapeDtypeStruct((bsz, s, SWA_HEADS * SWA_HD), BF16),
        scratch_shapes=[
            pltpu.VMEM((SWA_HEADS * SWA_HD, TOK_TILE), F32),
            pltpu.VMEM((2, stage_rows, TOK_TILE), F32),
            pltpu.VMEM((2, SWA_GROUP, SUBLANE, TOK_TILE), F32),
        ],
        compiler_params=pltpu.CompilerParams(vmem_limit_bytes=VMEM_LIMIT,
                                             dimension_semantics=("arbitrary", "arbitrary")),
        name="swa_attention",
    )(sink, _swa_bias(s), qst, qst, ks_c, ks_l, vst_c, vst_l)


def _mlp_kernel(x_ref, mm_ref, ms_ref, mod_ref, wo_ref, gmlp_ref, w1_ref, w2_ref, o_ref, acc_ref):
    half = MLA_HEADS * MLA_V
    y1 = _dot(mm_ref[0], wo_ref[0:half, :]) + _dot(ms_ref[0], wo_ref[half:, :])
    x1 = x_ref[0] + mod_ref[0, 2:3, :] * y1
    h2 = (_rms_rows(x1, D_MODEL) * gmlp_ref[...] * (1.0 + mod_ref[0, 4:5, :])
          + mod_ref[0, 3:4, :]).astype(BF16)
    for c in range(D_FF // FF_CHUNK):
        a = jnp.maximum(_dot(h2, w1_ref[:, c * FF_CHUNK:(c + 1) * FF_CHUNK]), 0.0)
        part = _dot((a * a).astype(BF16), w2_ref[c * FF_CHUNK:(c + 1) * FF_CHUNK, :])
        if c == 0:
            acc_ref[...] = part
        else:
            acc_ref[...] += part
    o_ref[0] = x1 + mod_ref[0, 5:6, :] * acc_ref[...]


def _outproj_mlp(x, mix_m, mix_s, mod3, w_out, g_mlp, w1, w2):
    bsz, s, _ = x.shape
    half = MLA_HEADS * MLA_V
    const = lambda b, j: (0, 0)
    single = pl.Buffered(1)
    return pl.pallas_call(
        _mlp_kernel,
        grid=(bsz, s // MLP_TILE),
        in_specs=[
            pl.BlockSpec((1, MLP_TILE, D_MODEL), lambda b, j: (b, j, 0)),
            pl.BlockSpec((1, MLP_TILE, half), lambda b, j: (b, j, 0)),
            pl.BlockSpec((1, MLP_TILE, half), lambda b, j: (b, j, 0)),
            pl.BlockSpec((1, 6, D_MODEL), lambda b, j: (b, 0, 0)),
            pl.BlockSpec((D_MODEL, D_MODEL), const, pipeline_mode=single),
            pl.BlockSpec((1, D_MODEL), const),
            pl.BlockSpec((D_MODEL, D_FF), const, pipeline_mode=single),
            pl.BlockSpec((D_FF, D_MODEL), const, pipeline_mode=single),
        ],
        out_specs=pl.BlockSpec((1, MLP_TILE, D_MODEL), lambda b, j: (b, j, 0)),
        out_shape=jax.ShapeDtypeStruct((bsz, s, D_MODEL), F32),
        scratch_shapes=[pltpu.VMEM((MLP_TILE, D_MODEL), F32)],
        compiler_params=pltpu.CompilerParams(vmem_limit_bytes=VMEM_LIMIT),
        name="outproj_mlp",
    )(x, mix_m, mix_s, mod3, w_out, g_mlp, w1, w2)


def _rope_tables(seq):
    f32 = np.float32

    def cos_sin(rot_dim):
        n_freq = rot_dim // 4
        inv = f32(ROPE_THETA) ** (-np.arange(n_freq, dtype=f32) / f32(n_freq))
        rows = seq // GRID_W
        row = np.repeat(np.arange(rows, dtype=f32), GRID_W)
        col = np.tile(np.arange(GRID_W, dtype=f32), rows)
        ang = np.concatenate([row[:, None] * inv, col[:, None] * inv], axis=-1).astype(f32)
        return np.cos(ang).astype(f32), np.sin(ang).astype(f32)

    mcos, msin = cos_sin(MLA_ROPE)
    scos, ssin = cos_sin(SWA_HD)
    z16 = np.zeros((seq, 16), f32)
    z32 = np.zeros((seq, 32), f32)
    z64 = np.zeros((seq, 64), f32)
    kc = np.concatenate([np.ones((seq, 64), f32), mcos, mcos, np.ones((seq, 32), f32)], axis=1)
    ks1 = np.concatenate([z64, -msin, z16, z32], axis=1)
    ks2 = np.concatenate([z64, z16, msin, z32], axis=1)
    sc = np.concatenate([scos, scos, scos, scos], axis=1)
    ss1 = np.concatenate([-ssin, z32, -ssin, z32], axis=1)
    ss2 = np.concatenate([z32, ssin, z32, ssin], axis=1)
    tabs = dict(mcos=mcos.T, msin=msin.T, scos=scos.T, ssin=ssin.T,
                kc=kc, ks1=ks1, ks2=ks2, sc=sc, ss1=ss1, ss2=ss2)
    return {k: jnp.asarray(np.ascontiguousarray(v)) for k, v in tabs.items()}


def _prep_weights(g_attn, w_in, g_q_a, w_uq, g_kv_a, w_ukv, g_mla_q, g_mla_k, g_swa_q, g_swa_k):
    o_ckv = Q_LORA
    o_kr = o_ckv + KV_LORA
    o_qs = o_kr + MLA_ROPE
    o_ks = o_qs + SWA_HEADS * SWA_HD
    o_vs = o_ks + SWA_KV_HEADS * SWA_HD
    w_cq = w_in[:, :o_ckv]
    w_ckv = w_in[:, o_ckv:o_kr]
    w_kr = w_in[:, o_kr:o_qs]
    w_qs = w_in[:, o_qs:o_ks]
    w_ks = w_in[:, o_ks:o_vs]
    w_vs = w_in[:, o_vs:]
    w_kr_p = jnp.pad(w_kr, ((0, 0), (MLA_NOPE, LANE - MLA_QK)))
    wtok = jnp.concatenate([w_cq, w_ckv, w_kr_p, w_ks], axis=1).astype(BF16)
    w_ukv_h = w_ukv.reshape(KV_LORA, MLA_HEADS, MLA_NOPE + MLA_V)
    wuk = jnp.pad(w_ukv_h[:, :, :MLA_NOPE], ((0, 0), (0, 0), (0, LANE - MLA_NOPE)))
    wuk = wuk.reshape(KV_LORA, MLA_HEADS * LANE).astype(BF16)
    wuv = w_ukv_h[:, :, MLA_NOPE:].reshape(KV_LORA, MLA_HEADS * MLA_V).T.astype(BF16)
    w_uq_h = jnp.pad(w_uq.reshape(Q_LORA, MLA_HEADS, MLA_QK), ((0, 0), (0, 0), (0, LANE - MLA_QK)))
    wuq = w_uq_h.reshape(Q_LORA, MLA_HEADS * LANE).T.astype(BF16)
    return dict(
        gattn=g_attn[None, :], wtok=wtok, wvs=w_vs.T.astype(BF16), gkva=g_kv_a[None, :],
        wuk=wuk, wuv=wuv, gmk=jnp.pad(g_mla_k, (0, LANE - MLA_QK))[None, :],
        gsk=jnp.tile(g_swa_k, SWA_KV_HEADS)[None, :],
        wqs=w_qs.T.astype(BF16), gqa=g_q_a[None, :], wuq=wuq,
        gmq=jnp.pad(g_mla_q, (0, LANE - MLA_QK))[:, None], gsq=g_swa_q[:, None],
    )


def kernel(x, c, ctx, c_ctx, w_mod, b_mod, g_attn, w_in, g_q_a, w_uq, g_kv_a, w_ukv, g_mla_q, g_mla_k,
           g_swa_q, g_swa_k, swa_sink, w_out, g_mlp, w_mlp1, w_mlp2):
    bsz, seq, _ = x.shape
    assert w_mod.shape[0] == 1, "single-layer block"
    assert bsz < SUBLANE and seq % PROJ_TILE == 0 and seq % MLP_TILE == 0 and ctx.shape[1] % TOK_TILE == 0

    cond = jnp.zeros((SUBLANE, D_MODEL), F32).at[:bsz].set(c).at[bsz].set(c_ctx)
    mod = _modulation(cond, w_mod[0], b_mod[0][None, :])
    mod3 = mod.reshape(SUBLANE, 6, D_MODEL)

    weights = _prep_weights(g_attn[0], w_in[0], g_q_a[0], w_uq[0], g_kv_a[0], w_ukv[0],
                            g_mla_q[0], g_mla_k[0], g_swa_q[0], g_swa_k[0])
    tables = _rope_tables(seq)

    weights["to_cast"] = [w_out[0], w_mlp1[0], w_mlp2[0]]
    (km_l, vmt_l, ks_l, vst_l, qmt, qst,
     w_out_bf16, w_mlp1_bf16, w_mlp2_bf16) = _project(x, mod3, None, weights, tables, latent=True, tile=PROJ_TILE)
    km_c, vmt_c, ks_c, vst_c = _project(ctx, mod3, bsz, weights, None, latent=False, tile=TOK_TILE)

    mix_m = _mla_attention(qmt, km_c, km_l, vmt_c, vmt_l)
    mix_s = _swa_attention(swa_sink[0], qst, ks_c, ks_l, vst_c, vst_l)

    return _outproj_mlp(x, mix_m, mix_s, mod3, w_out_bf16, g_mlp[0][None, :], w_mlp1_bf16, w_mlp2_bf16)
```

```python
import functools
import math

import jax
import jax.numpy as jnp
import numpy as np
from jax import lax
from jax.experimental import pallas as pl
from jax.experimental.pallas import tpu as pltpu

D_MODEL = 1024
GRID_W = 64
MLA_HEADS = 8
MLA_NOPE = 64
MLA_ROPE = 32
MLA_QK = MLA_NOPE + MLA_ROPE
MLA_V = 64
MLA_V_AUG = MLA_V + 16
Q_LORA = 256
KV_LORA = 128
SWA_HEADS = 8
SWA_KV_HEADS = 2
SWA_GROUP = SWA_HEADS // SWA_KV_HEADS
SWA_HD = 64
SWA_V_AUG = SWA_HD + 16
WINDOW = 128
D_FF = 4 * D_MODEL
ROPE_THETA = 10000.0
EPS = 1e-6
NEG_INF = -1e30
LOG2E = 1.4426950408889634
MLA_QSCALE = LOG2E / math.sqrt(MLA_QK)
SWA_QSCALE = LOG2E / math.sqrt(SWA_HD)

LANE = 128
SUBLANE = 8
MXU_DIM = 256
TOK_TILE = MXU_DIM
PROJ_TILE = 1024
PROJ_PART = 256
MOD_TILE = 1024
MLP_TILE = 512
FF_CHUNK = 1024
SWA_WIN = 2 * TOK_TILE
MLA_SUB = MXU_DIM
MLA_Q_TILES = 4
SWA_SUB = MXU_DIM
MLA_CHUNK = 32
SWA_CHUNK = 32
MLA_QK_LEAD = 3
SWA_QK_LEAD = 1
VMEM_LIMIT = 56 * 1024 * 1024

F32 = jnp.float32
BF16 = jnp.bfloat16
NT_DIMS = (((1,), (1,)), ((), ()))


def _dot(a, b):
    return jnp.dot(a, b, preferred_element_type=F32)


def _dot_nt(a, b):
    return lax.dot_general(a, b, NT_DIMS, preferred_element_type=F32)


def _mod_kernel(cond_ref, w_ref, b_ref, o_ref):
    cnd = cond_ref[...]
    act = cnd * jax.nn.sigmoid(cnd)
    o_ref[...] = _dot(act.astype(BF16), w_ref[...].astype(BF16)) + b_ref[...]


def _modulation(cond, w_mod, b_mod):
    n = w_mod.shape[1]
    tn = MOD_TILE
    return pl.pallas_call(
        _mod_kernel,
        grid=(n // tn,),
        in_specs=[
            pl.BlockSpec((SUBLANE, D_MODEL), lambda i: (0, 0)),
            pl.BlockSpec((D_MODEL, tn), lambda i: (0, i)),
            pl.BlockSpec((1, tn), lambda i: (0, i)),
        ],
        out_specs=pl.BlockSpec((SUBLANE, tn), lambda i: (0, i)),
        out_shape=jax.ShapeDtypeStruct((SUBLANE, n), F32),
        name="modulation",
    )(cond, w_mod, b_mod)


def _rms_rows(x, n):
    return x * lax.rsqrt(jnp.sum(x * x, axis=-1, keepdims=True) * (1.0 / n) + EPS)


def _rms_cols(x, n):
    return x * lax.rsqrt(jnp.sum(x * x, axis=0, keepdims=True) * (1.0 / n) + EPS)


def _proj_kernel(*refs, latent):
    if latent:
        (x_ref, mod_ref, gattn_ref, wtok_ref, wvs_ref, gkva_ref, wuk_ref, wuv_ref, gmk_ref, gsk_ref,
         wqs_ref, gqa_ref, wuq_ref, gmq_ref, gsq_ref,
         mcos_ref, msin_ref, scos_ref, ssin_ref, kc_ref, ks1_ref, ks2_ref, sc_ref, ss1_ref, ss2_ref,
         wo_f32_ref, w1_f32_ref, w2_f32_ref,
         km_ref, vmt_ref, ks_ref, vst_ref, qmt_ref, qst_ref,
         wo_bf16_ref, w1_bf16_ref, w2_bf16_ref) = refs
        for src, dst in ((wo_f32_ref, wo_bf16_ref), (w1_f32_ref, w1_bf16_ref), (w2_f32_ref, w2_bf16_ref)):
            dst[...] = src[...].astype(BF16)
    else:
        (x_ref, mod_ref, gattn_ref, wtok_ref, wvs_ref, gkva_ref, wuk_ref, wuv_ref, gmk_ref, gsk_ref,
         km_ref, vmt_ref, ks_ref, vst_ref) = refs

    tile = x_ref.shape[1]
    n_part = max(1, tile // PROJ_PART)
    part = tile // n_part

    gain = gattn_ref[...] * (1.0 + mod_ref[0, 1:2, :])
    shift = mod_ref[0, 0:1, :]
    gmk = gmk_ref[...]
    ones_row = (lax.broadcasted_iota(jnp.int32, (MLA_V_AUG - MLA_V, part), 0) == 0).astype(BF16)
    if latent:
        gmq = gmq_ref[...] * MLA_QSCALE
        gsq = gsq_ref[...] * SWA_QSCALE

    def part_phases(lo):
        rows = slice(lo, lo + part)

        h = (_rms_rows(x_ref[0, rows, :], D_MODEL) * gain + shift).astype(BF16)
        yield

        zt = _dot(h, wtok_ref[...])
        vst = _dot_nt(wvs_ref[...], h)
        if latent:
            qst = _dot_nt(wqs_ref[...], h)
        yield

        cq = zt[:, 0:256]
        ckv = zt[:, 256:384]
        krp = zt[:, 384:512]
        ksw = zt[:, 512:640]
        ckvn = (_rms_rows(ckv, KV_LORA) * gkva_ref[...]).astype(BF16)
        if latent:
            cqn = (_rms_rows(cq, Q_LORA) * gqa_ref[...]).astype(BF16)
        kr = krp * gmk
        if latent:
            kr = (kr * kc_ref[rows, :] + pltpu.roll(kr, 112, 1) * ks1_ref[rows, :]
                  + pltpu.roll(kr, 16, 1) * ks2_ref[rows, :])
        ss_kr = jnp.sum(krp * krp, axis=-1, keepdims=True)
        first = lax.broadcasted_iota(jnp.int32, ksw.shape, 1) < SWA_HD
        sq = ksw * ksw
        ss0 = jnp.sum(jnp.where(first, sq, 0.0), axis=-1, keepdims=True)
        ss1 = jnp.sum(jnp.where(first, 0.0, sq), axis=-1, keepdims=True)
        rr = jnp.where(first, lax.rsqrt(ss0 * (1.0 / SWA_HD) + EPS), lax.rsqrt(ss1 * (1.0 / SWA_HD) + EPS))
        ksn = ksw * rr * gsk_ref[...]
        if latent:
            ksn = (ksn * sc_ref[rows, :] + pltpu.roll(ksn, 96, 1) * ss1_ref[rows, :]
                   + pltpu.roll(ksn, 32, 1) * ss2_ref[rows, :])
        ks_ref[0, rows, :] = ksn.astype(BF16)
        for g in range(SWA_KV_HEADS):
            for t in range(part // LANE):
                dst = lo // LANE + t
                vst_ref[0, g, dst, 0:SWA_HD, :] = vst[g * SWA_HD:(g + 1) * SWA_HD, t * LANE:(t + 1) * LANE].astype(BF16)
                vst_ref[0, g, dst, SWA_HD:SWA_V_AUG, :] = ones_row[:, 0:LANE]
        yield

        kpre = _dot(ckvn, wuk_ref[...])
        vt = _dot_nt(wuv_ref[...], ckvn)
        if latent:
            qt = _dot_nt(wuq_ref[...], cqn)
        yield

        for hd in range(MLA_HEADS):
            kp = kpre[:, hd * LANE:(hd + 1) * LANE]
            ss = jnp.sum(kp * kp, axis=-1, keepdims=True) + ss_kr
            kn = (kp * gmk + kr) * lax.rsqrt(ss * (1.0 / MLA_QK) + EPS)
            km_ref[0, hd, rows, :] = kn.astype(BF16)
            vmt_ref[0, hd, 0:MLA_V, rows] = vt[hd * MLA_V:(hd + 1) * MLA_V].astype(BF16)
            vmt_ref[0, hd, MLA_V:MLA_V_AUG, rows] = ones_row
        if latent:
            mcos = mcos_ref[:, rows]
            msin = msin_ref[:, rows]
            for hd in range(MLA_HEADS):
                qn = _rms_cols(qt[hd * LANE:(hd + 1) * LANE], MLA_QK) * gmq
                x1 = qn[64:80]
                x2 = qn[80:96]
                qt_idx = lo // TOK_TILE
                qmt_ref[0, hd, qt_idx, 0:64, :] = qn[0:64].astype(BF16)
                qmt_ref[0, hd, qt_idx, 64:80, :] = (x1 * mcos - x2 * msin).astype(BF16)
                qmt_ref[0, hd, qt_idx, 80:96, :] = (x2 * mcos + x1 * msin).astype(BF16)
                qmt_ref[0, hd, qt_idx, 96:128, :] = jnp.zeros((32, part), BF16)
            scos = scos_ref[:, rows]
            ssin = ssin_ref[:, rows]
            for hd in range(SWA_HEADS):
                qn = _rms_cols(qst[hd * SWA_HD:(hd + 1) * SWA_HD], SWA_HD) * gsq
                x1 = qn[0:32]
                x2 = qn[32:64]
                g = hd // SWA_GROUP
                base = g * SWA_HD
                other = (1 - g) * SWA_HD
                qst_ref[0, hd, base:base + 32, rows] = (x1 * scos - x2 * ssin).astype(BF16)
                qst_ref[0, hd, base + 32:base + 64, rows] = (x2 * scos + x1 * ssin).astype(BF16)
                qst_ref[0, hd, other:other + 64, rows] = jnp.zeros((64, part), BF16)
        yield

    n_phase = 5
    parts = [part_phases(i * part) for i in range(n_part)]
    for step in range(n_phase + n_part - 1):
        for i, gen in enumerate(parts):
            if 0 <= step - i < n_phase:
                next(gen)


def _full(shape):
    nd = len(shape)
    return pl.BlockSpec(shape, lambda b, j: (0,) * nd)


def _project(x, mod3, mod_row, weights, tables, latent, tile):
    bsz, n, _ = x.shape
    nt = n // tile
    if mod_row is None:
        mod_map = lambda b, j: (b, 0, 0)
    else:
        mod_map = lambda b, j: (mod_row, 0, 0)
    common = [weights[k] for k in ("gattn", "wtok", "wvs", "gkva", "wuk", "wuv", "gmk", "gsk")]
    ins = [x, mod3] + common
    in_specs = [
        pl.BlockSpec((1, tile, D_MODEL), lambda b, j: (b, j, 0)),
        pl.BlockSpec((1, 6, D_MODEL), mod_map),
    ] + [_full(w.shape) for w in common]
    out_shape = [
        jax.ShapeDtypeStruct((bsz, MLA_HEADS, n, LANE), BF16),
        jax.ShapeDtypeStruct((bsz, MLA_HEADS, MLA_V_AUG, n), BF16),
        jax.ShapeDtypeStruct((bsz, n, LANE), BF16),
        jax.ShapeDtypeStruct((bsz, SWA_KV_HEADS, n // LANE, SWA_V_AUG, LANE), BF16),
    ]
    out_specs = [
        pl.BlockSpec((1, MLA_HEADS, tile, LANE), lambda b, j: (b, 0, j, 0)),
        pl.BlockSpec((1, MLA_HEADS, MLA_V_AUG, tile), lambda b, j: (b, 0, 0, j)),
        pl.BlockSpec((1, tile, LANE), lambda b, j: (b, j, 0)),
        pl.BlockSpec((1, SWA_KV_HEADS, tile // LANE, SWA_V_AUG, LANE), lambda b, j: (b, 0, j, 0, 0)),
    ]
    if latent:
        extra = [weights[k] for k in ("wqs", "gqa", "wuq", "gmq", "gsq")]
        ins += extra
        in_specs += [_full(w.shape) for w in extra]
        feat = [tables[k] for k in ("mcos", "msin", "scos", "ssin")]
        ins += feat
        in_specs += [pl.BlockSpec((t.shape[0], tile), lambda b, j: (0, j)) for t in feat]
        tok = [tables[k] for k in ("kc", "ks1", "ks2", "sc", "ss1", "ss2")]
        ins += tok
        in_specs += [pl.BlockSpec((tile, LANE), lambda b, j: (j, 0)) for _ in tok]
        assert PROJ_PART == TOK_TILE
        out_shape += [
            jax.ShapeDtypeStruct((bsz, MLA_HEADS, n // TOK_TILE, LANE, TOK_TILE), BF16),
            jax.ShapeDtypeStruct((bsz, SWA_HEADS, LANE, n), BF16),
        ]
        out_specs += [
            pl.BlockSpec((1, MLA_HEADS, tile // TOK_TILE, LANE, TOK_TILE), lambda b, j: (b, 0, j, 0, 0)),
            pl.BlockSpec((1, SWA_HEADS, LANE, tile), lambda b, j: (b, 0, 0, j)),
        ]
        steps = bsz * nt
        for w in weights["to_cast"]:
            rows = w.shape[0] // steps
            assert rows * steps == w.shape[0] and rows % 16 == 0
            ins.append(w)
            in_specs.append(pl.BlockSpec((rows, w.shape[1]), lambda b, j: (b * nt + j, 0)))
            out_shape.append(jax.ShapeDtypeStruct(w.shape, BF16))
            out_specs.append(pl.BlockSpec((rows, w.shape[1]), lambda b, j: (b * nt + j, 0)))
    return pl.pallas_call(
        functools.partial(_proj_kernel, latent=latent),
        grid=(bsz, nt),
        in_specs=in_specs,
        out_specs=out_specs,
        out_shape=out_shape,
        compiler_params=pltpu.CompilerParams(vmem_limit_bytes=VMEM_LIMIT),
        name="project_latent" if latent else "project_context",
    )(*ins)


def _mla_kernel(qt_ref, kc_ref, kl_ref, vct_ref, vlt_ref, o_ref, ot_scr, s_scr, *, seq):
    n_ctx = kc_ref.shape[2]
    tq = TOK_TILE
    subs = ([(kc_ref, vct_ref, o) for o in range(0, n_ctx, MLA_SUB)]
            + [(kl_ref, vlt_ref, o) for o in range(0, seq, MLA_SUB)])
    n_sub = len(subs)

    def scores_into(slot, j, stage, m8):
        t, hd = stage
        k_ref, _, off = subs[j]
        sj = _dot(k_ref[0, hd, off:off + MLA_SUB, :], qt_ref[0, hd, t])
        s_scr[slot, j * MLA_SUB:(j + 1) * MLA_SUB, :] = sj
        mj = jnp.max(sj.reshape(MLA_SUB // SUBLANE, SUBLANE, tq), axis=0)
        return mj if m8 is None else jnp.maximum(m8, mj)

    def stage_body(stage, m8_cur, slot, nxt):
        t, hd = stage
        mb = jnp.broadcast_to(jnp.max(m8_cur, axis=0, keepdims=True), (MLA_CHUNK, tq))
        n_q = n_sub if nxt is not None else 0
        m8_next = None
        acc = None
        for j in range(min(MLA_QK_LEAD, n_q)):
            m8_next = scores_into(1 - slot, j, nxt, m8_next)
        for j in range(n_sub):
            base = j * MLA_SUB
            p = jnp.concatenate(
                [jnp.exp2(s_scr[slot, r:r + MLA_CHUNK, :] - mb).astype(BF16)
                 for r in range(base, base + MLA_SUB, MLA_CHUNK)], axis=0)
            _, v_ref, off = subs[j]
            d = _dot(v_ref[0, hd, :, off:off + MLA_SUB], p)
            acc = d if acc is None else acc + d
            if j + MLA_QK_LEAD < n_q:
                m8_next = scores_into(1 - slot, j + MLA_QK_LEAD, nxt, m8_next)
        row = pl.multiple_of(hd * MLA_V, MLA_V)
        ot_scr[t, pl.ds(row, MLA_V), :] = acc[0:MLA_V] / acc[MLA_V:MLA_V + 1]
        return m8_next

    def stage_of(n):
        return n // MLA_HEADS, lax.rem(n, MLA_HEADS)

    def stage_step(n, m8):
        return lax.cond(lax.rem(n, 2) == 0,
                        lambda m: stage_body(stage_of(n), m, 0, stage_of(n + 1)),
                        lambda m: stage_body(stage_of(n), m, 1, stage_of(n + 1)), m8)

    n_stage = MLA_HEADS * MLA_Q_TILES
    m8_first = None
    for j in range(n_sub):
        m8_first = scores_into(0, j, (0, 0), m8_first)
    m8_last = lax.fori_loop(0, n_stage - 1, stage_step, m8_first)
    stage_body((MLA_Q_TILES - 1, MLA_HEADS - 1), m8_last, (n_stage - 1) % 2, None)
    for t in range(MLA_Q_TILES):
        o_ref[0, t * tq:(t + 1) * tq, :] = ot_scr[t].T.astype(BF16)


def _mla_attention(qmt, km_c, km_l, vmt_c, vmt_l):
    bsz = qmt.shape[0]
    s = km_l.shape[2]
    n_ctx = km_c.shape[2]
    step_q = MLA_Q_TILES * TOK_TILE
    assert s % step_q == 0
    n_keys = n_ctx + s
    return pl.pallas_call(
        functools.partial(_mla_kernel, seq=s),
        grid=(bsz, s // step_q),
        in_specs=[
            pl.BlockSpec((1, MLA_HEADS, MLA_Q_TILES, LANE, TOK_TILE), lambda b, j: (b, 0, j, 0, 0)),
            pl.BlockSpec((1, MLA_HEADS, n_ctx, LANE), lambda b, j: (b, 0, 0, 0)),
            pl.BlockSpec((1, MLA_HEADS, s, LANE), lambda b, j: (b, 0, 0, 0)),
            pl.BlockSpec((1, MLA_HEADS, MLA_V_AUG, n_ctx), lambda b, j: (b, 0, 0, 0)),
            pl.BlockSpec((1, MLA_HEADS, MLA_V_AUG, s), lambda b, j: (b, 0, 0, 0)),
        ],
        out_specs=pl.BlockSpec((1, step_q, MLA_HEADS * MLA_V), lambda b, j: (b, j, 0)),
        out_shape=jax.ShapeDtypeStruct((bsz, s, MLA_HEADS * MLA_V), BF16),
        scratch_shapes=[
            pltpu.VMEM((MLA_Q_TILES, MLA_HEADS * MLA_V, TOK_TILE), F32),
            pltpu.VMEM((2, n_keys, TOK_TILE), F32),
        ],
        compiler_params=pltpu.CompilerParams(vmem_limit_bytes=VMEM_LIMIT,
                                             dimension_semantics=("arbitrary", "arbitrary")),
        name="mla_attention",
    )(qmt, km_c, km_l, vmt_c, vmt_l)


def _swa_kernel(sink_ref, bias_ref, qt_ref, qtn_ref, kc_ref, kl_ref, vct_ref, vlt_ref, o_ref,
                ot_scr, s_scr, m8_scr, *, seq):
    j = pl.program_id(1)
    nt = pl.num_programs(1)
    n_ctx = kc_ref.shape[1]
    tq = qt_ref.shape[3]
    n_band = SWA_WIN // SWA_SUB
    n_t = n_band + n_ctx // SWA_SUB
    lanes_per_sub = SWA_SUB // LANE
    subs = [(hh, t) for hh in range(SWA_GROUP) for t in range(n_t)]

    def window(jj):
        start = pl.multiple_of(jnp.clip(jj * TOK_TILE - WINDOW, 0, seq - SWA_WIN), LANE)
        variant = jnp.where(jj == 0, 0, jnp.where(jj == nt - 1, 2, 1))
        return start, variant

    def rows_of(idx):
        return slice(idx * SWA_SUB, (idx + 1) * SWA_SUB)

    def scores_into(slot, idx, q_ref, g, win):
        hh, t = subs[idx]
        start, variant = win
        if t < n_band:
            k = kl_ref[0, pl.ds(start + t * SWA_SUB, SWA_SUB), :]
        else:
            k = kc_ref[0, (t - n_band) * SWA_SUB:(t - n_band + 1) * SWA_SUB, :]
        sj = _dot(k, q_ref[0, g * SWA_GROUP + hh])
        if t < n_band:
            sj = sj + bias_ref[variant, t * SWA_SUB:(t + 1) * SWA_SUB, :]
        s_scr[slot, rows_of(idx), :] = sj
        return jnp.max(sj.reshape(SWA_SUB // SUBLANE, SUBLANE, tq), axis=0)

    def values_t(g, t, t0):
        if t < n_band:
            tiles = [vlt_ref[0, g, t0 + t * lanes_per_sub + i] for i in range(lanes_per_sub)]
        else:
            tiles = [vct_ref[0, g, (t - n_band) * lanes_per_sub + i] for i in range(lanes_per_sub)]
        return jnp.concatenate(tiles, axis=1)

    def next_scores(slot, idx, nxt, m8n):
        q_ref, g_n, win_n = nxt
        mj = scores_into(slot, idx, q_ref, g_n, win_n)
        hh, t = subs[idx]
        m8n[hh] = mj if t == 0 else jnp.maximum(m8n[hh], mj)
        if t == n_t - 1:
            m8_scr[slot, hh] = m8n[hh]

    def stage_body(g, win_cur, nxt):
        slot = g
        t0 = win_cur[0] // LANE
        n_q = len(subs) if nxt is not None else 0
        m8n = {}
        for idx in range(min(SWA_QK_LEAD, n_q)):
            next_scores(1 - slot, idx, nxt, m8n)
        for idx, (hh, t) in enumerate(subs):
            hd = g * SWA_GROUP + hh
            if t == 0:
                sk = sink_ref[hd] * LOG2E
                m = jnp.maximum(jnp.max(m8_scr[slot, hh], axis=0, keepdims=True), sk)
                mb = jnp.broadcast_to(m, (SWA_CHUNK, tq))
                acc = None
            base = idx * SWA_SUB
            p = jnp.concatenate(
                [jnp.exp2(s_scr[slot, r:r + SWA_CHUNK, :] - mb).astype(BF16)
                 for r in range(base, base + SWA_SUB, SWA_CHUNK)], axis=0)
            d = _dot(values_t(g, t, t0), p)
            acc = d if acc is None else acc + d
            if t == n_t - 1:
                denom = acc[SWA_HD:SWA_HD + 1] + jnp.exp2(sk - m)
                ot_scr[hd * SWA_HD:(hd + 1) * SWA_HD, :] = acc[0:SWA_HD] / denom
            if idx + SWA_QK_LEAD < n_q:
                next_scores(1 - slot, idx + SWA_QK_LEAD, nxt, m8n)

    win = window(j)

    @pl.when(j == 0)
    def _():
        m8n = {}
        for idx in range(len(subs)):
            next_scores(0, idx, (qt_ref, 0, win), m8n)

    stage_body(0, win, (qt_ref, 1, win))

    @pl.when(j < nt - 1)
    def _():
        stage_body(1, win, (qtn_ref, 0, window(j + 1)))

    @pl.when(j == nt - 1)
    def _():
        stage_body(1, win, None)

    o_ref[0] = ot_scr[...].T.astype(BF16)


def _swa_bias(seq):
    nt = seq // TOK_TILE
    out = []
    for j in (0, 1, nt - 1):
        start = min(max(j * TOK_TILE - WINDOW, 0), seq - SWA_WIN)
        qpos = j * TOK_TILE + np.arange(TOK_TILE)[None, :]
        kpos = start + np.arange(SWA_WIN)[:, None]
        out.append(np.where(np.abs(qpos - kpos) <= WINDOW, 0.0, NEG_INF))
    return jnp.asarray(np.stack(out), F32)


def _swa_attention(sink, qst, ks_c, ks_l, vst_c, vst_l):
    bsz, _, _, s = qst.shape
    n_ctx = ks_c.shape[1]
    nt = s // TOK_TILE
    assert nt >= 3 and s >= SWA_WIN and n_ctx % SWA_SUB == 0
    stage_rows = SWA_GROUP * (SWA_WIN + n_ctx)
    return pl.pallas_call(
        functools.partial(_swa_kernel, seq=s),
        grid=(bsz, nt),
        in_specs=[
            pl.BlockSpec(memory_space=pltpu.SMEM),
            pl.BlockSpec((3, SWA_WIN, TOK_TILE), lambda b, j: (0, 0, 0)),
            pl.BlockSpec((1, SWA_HEADS, LANE, TOK_TILE), lambda b, j: (b, 0, 0, j)),
            pl.BlockSpec((1, SWA_GROUP, LANE, TOK_TILE), lambda b, j: (b, 0, 0, jnp.minimum(j + 1, nt - 1))),
            pl.BlockSpec((1, n_ctx, LANE), lambda b, j: (b, 0, 0)),
            pl.BlockSpec((1, s, LANE), lambda b, j: (b, 0, 0)),
            pl.BlockSpec((1, SWA_KV_HEADS, n_ctx // LANE, SWA_V_AUG, LANE), lambda b, j: (b, 0, 0, 0, 0)),
            pl.BlockSpec((1, SWA_KV_HEADS, s // LANE, SWA_V_AUG, LANE), lambda b, j: (b, 0, 0, 0, 0)),
        ],
        out_specs=pl.BlockSpec((1, TOK_TILE, SWA_HEADS * SWA_HD), lambda b, j: (b, j, 0)),
        out_shape=jax.ShapeDtypeStruct((bsz, s, SWA_HEADS * SWA_HD), BF16),
        scratch_shapes=[
            pltpu.VMEM((SWA_HEADS * SWA_HD, TOK_TILE), F32),
            pltpu.VMEM((2, stage_rows, TOK_TILE), F32),
            pltpu.VMEM((2, SWA_GROUP, SUBLANE, TOK_TILE), F32),
        ],
        compiler_params=pltpu.CompilerParams(vmem_limit_bytes=VMEM_LIMIT,
                                             dimension_semantics=("arbitrary", "arbitrary")),
        name="swa_attention",
    )(sink, _swa_bias(s), qst, qst, ks_c, ks_l, vst_c, vst_l)


def _mlp_kernel(x_ref, mm_ref, ms_ref, mod_ref, wo_ref, gmlp_ref, w1_ref, w2_ref, o_ref, acc_ref):
    half = MLA_HEADS * MLA_V
    y1 = _dot(mm_ref[0], wo_ref[0:half, :]) + _dot(ms_ref[0], wo_ref[half:, :])
    x1 = x_ref[0] + mod_ref[0, 2:3, :] * y1
    h2 = (_rms_rows(x1, D_MODEL) * gmlp_ref[...] * (1.0 + mod_ref[0, 4:5, :])
          + mod_ref[0, 3:4, :]).astype(BF16)
    for c in range(D_FF // FF_CHUNK):
        a = jnp.maximum(_dot(h2, w1_ref[:, c * FF_CHUNK:(c + 1) * FF_CHUNK]), 0.0)
        part = _dot((a * a).astype(BF16), w2_ref[c * FF_CHUNK:(c + 1) * FF_CHUNK, :])
        if c == 0:
            acc_ref[...] = part
        else:
            acc_ref[...] += part
    o_ref[0] = x1 + mod_ref[0, 5:6, :] * acc_ref[...]


def _outproj_mlp(x, mix_m, mix_s, mod3, w_out, g_mlp, w1, w2):
    bsz, s, _ = x.shape
    half = MLA_HEADS * MLA_V
    const = lambda b, j: (0, 0)
    single = pl.Buffered(1)
    return pl.pallas_call(
        _mlp_kernel,
        grid=(bsz, s // MLP_TILE),
        in_specs=[
            pl.BlockSpec((1, MLP_TILE, D_MODEL), lambda b, j: (b, j, 0)),
            pl.BlockSpec((1, MLP_TILE, half), lambda b, j: (b, j, 0)),
            pl.BlockSpec((1, MLP_TILE, half), lambda b, j: (b, j, 0)),
            pl.BlockSpec((1, 6, D_MODEL), lambda b, j: (b, 0, 0)),
            pl.BlockSpec((D_MODEL, D_MODEL), const, pipeline_mode=single),
            pl.BlockSpec((1, D_MODEL), const),
            pl.BlockSpec((D_MODEL, D_FF), const, pipeline_mode=single),
            pl.BlockSpec((D_FF, D_MODEL), const, pipeline_mode=single),
        ],
        out_specs=pl.BlockSpec((1, MLP_TILE, D_MODEL), lambda b, j: (b, j, 0)),
        out_shape=jax.ShapeDtypeStruct((bsz, s, D_MODEL), F32),
        scratch_shapes=[pltpu.VMEM((MLP_TILE, D_MODEL), F32)],
        compiler_params=pltpu.CompilerParams(vmem_limit_bytes=VMEM_LIMIT),
        name="outproj_mlp",
    )(x, mix_m, mix_s, mod3, w_out, g_mlp, w1, w2)


def _rope_tables(seq):
    f32 = np.float32

    def cos_sin(rot_dim):
        n_freq = rot_dim // 4
        inv = f32(ROPE_THETA) ** (-np.arange(n_freq, dtype=f32) / f32(n_freq))
        rows = seq // GRID_W
        row = np.repeat(np.arange(rows, dtype=f32), GRID_W)
        col = np.tile(np.arange(GRID_W, dtype=f32), rows)
        ang = np.concatenate([row[:, None] * inv, col[:, None] * inv], axis=-1).astype(f32)
        return np.cos(ang).astype(f32), np.sin(ang).astype(f32)

    mcos, msin = cos_sin(MLA_ROPE)
    scos, ssin = cos_sin(SWA_HD)
    z16 = np.zeros((seq, 16), f32)
    z32 = np.zeros((seq, 32), f32)
    z64 = np.zeros((seq, 64), f32)
    kc = np.concatenate([np.ones((seq, 64), f32), mcos, mcos, np.ones((seq, 32), f32)], axis=1)
    ks1 = np.concatenate([z64, -msin, z16, z32], axis=1)
    ks2 = np.concatenate([z64, z16, msin, z32], axis=1)
    sc = np.concatenate([scos, scos, scos, scos], axis=1)
    ss1 = np.concatenate([-ssin, z32, -ssin, z32], axis=1)
    ss2 = np.concatenate([z32, ssin, z32, ssin], axis=1)
    tabs = dict(mcos=mcos.T, msin=msin.T, scos=scos.T, ssin=ssin.T,
                kc=kc, ks1=ks1, ks2=ks2, sc=sc, ss1=ss1, ss2=ss2)
    return {k: jnp.asarray(np.ascontiguousarray(v)) for k, v in tabs.items()}


def _prep_weights(g_attn, w_in, g_q_a, w_uq, g_kv_a, w_ukv, g_mla_q, g_mla_k, g_swa_q, g_swa_k):
    o_ckv = Q_LORA
    o_kr = o_ckv + KV_LORA
    o_qs = o_kr + MLA_ROPE
    o_ks = o_qs + SWA_HEADS * SWA_HD
    o_vs = o_ks + SWA_KV_HEADS * SWA_HD
    w_cq = w_in[:, :o_ckv]
    w_ckv = w_in[:, o_ckv:o_kr]
    w_kr = w_in[:, o_kr:o_qs]
    w_qs = w_in[:, o_qs:o_ks]
    w_ks = w_in[:, o_ks:o_vs]
    w_vs = w_in[:, o_vs:]
    w_kr_p = jnp.pad(w_kr, ((0, 0), (MLA_NOPE, LANE - MLA_QK)))
    wtok = jnp.concatenate([w_cq, w_ckv, w_kr_p, w_ks], axis=1).astype(BF16)
    w_ukv_h = w_ukv.reshape(KV_LORA, MLA_HEADS, MLA_NOPE + MLA_V)
    wuk = jnp.pad(w_ukv_h[:, :, :MLA_NOPE], ((0, 0), (0, 0), (0, LANE - MLA_NOPE)))
    wuk = wuk.reshape(KV_LORA, MLA_HEADS * LANE).astype(BF16)
    wuv = w_ukv_h[:, :, MLA_NOPE:].reshape(KV_LORA, MLA_HEADS * MLA_V).T.astype(BF16)
    w_uq_h = jnp.pad(w_uq.reshape(Q_LORA, MLA_HEADS, MLA_QK), ((0, 0), (0, 0), (0, LANE - MLA_QK)))
    wuq = w_uq_h.reshape(Q_LORA, MLA_HEADS * LANE).T.astype(BF16)
    return dict(
        gattn=g_attn[None, :], wtok=wtok, wvs=w_vs.T.astype(BF16), gkva=g_kv_a[None, :],
        wuk=wuk, wuv=wuv, gmk=jnp.pad(g_mla_k, (0, LANE - MLA_QK))[None, :],
        gsk=jnp.tile(g_swa_k, SWA_KV_HEADS)[None, :],
        wqs=w_qs.T.astype(BF16), gqa=g_q_a[None, :], wuq=wuq,
        gmq=jnp.pad(g_mla_q, (0, LANE - MLA_QK))[:, None], gsq=g_swa_q[:, None],
    )


def kernel(x, c, ctx, c_ctx, w_mod, b_mod, g_attn, w_in, g_q_a, w_uq, g_kv_a, w_ukv, g_mla_q, g_mla_k,
           g_swa_q, g_swa_k, swa_sink, w_out, g_mlp, w_mlp1, w_mlp2):
    bsz, seq, _ = x.shape
    assert w_mod.shape[0] == 1, "single-layer block"
    assert bsz < SUBLANE and seq % PROJ_TILE == 0 and seq % MLP_TILE == 0 and ctx.shape[1] % TOK_TILE == 0

    cond = jnp.zeros((SUBLANE, D_MODEL), F32).at[:bsz].set(c).at[bsz].set(c_ctx)
    mod = _modulation(cond, w_mod[0], b_mod[0][None, :])
    mod3 = mod.reshape(SUBLANE, 6, D_MODEL)

    weights = _prep_weights(g_attn[0], w_in[0], g_q_a[0], w_uq[0], g_kv_a[0], w_ukv[0],
                            g_mla_q[0], g_mla_k[0], g_swa_q[0], g_swa_k[0])
    tables = _rope_tables(seq)

    weights["to_cast"] = [w_out[0], w_mlp1[0], w_mlp2[0]]
    (km_l, vmt_l, ks_l, vst_l, qmt, qst,
     w_out_bf16, w_mlp1_bf16, w_mlp2_bf16) = _project(x, mod3, None, weights, tables, latent=True, tile=PROJ_TILE)
    km_c, vmt_c, ks_c, vst_c = _project(ctx, mod3, bsz, weights, None, latent=False, tile=TOK_TILE)

    mix_m = _mla_attention(qmt, km_c, km_l, vmt_c, vmt_l)
    mix_s = _swa_attention(swa_sink[0], qst, ks_c, ks_l, vst_c, vst_l)

    return _outproj_mlp(x, mix_m, mix_s, mod3, w_out_bf16, g_mlp[0][None, :], w_mlp1_bf16, w_mlp2_bf16)
```

```python
import functools
import math

import jax
import jax.numpy as jnp
import numpy as np
from jax import lax
from jax.experimental import pallas as pl
from jax.experimental.pallas import tpu as pltpu

D_MODEL = 1024
GRID_W = 64
MLA_HEADS = 8
MLA_NOPE = 64
MLA_ROPE = 32
MLA_QK = MLA_NOPE + MLA_ROPE
MLA_V = 64
MLA_V_AUG = MLA_V + 16
Q_LORA = 256
KV_LORA = 128
SWA_HEADS = 8
SWA_KV_HEADS = 2
SWA_GROUP = SWA_HEADS // SWA_KV_HEADS
SWA_HD = 64
SWA_V_AUG = SWA_HD + 16
WINDOW = 128
D_FF = 4 * D_MODEL
ROPE_THETA = 10000.0
EPS = 1e-6
NEG_INF = -1e30
LOG2E = 1.4426950408889634
MLA_QSCALE = LOG2E / math.sqrt(MLA_QK)
SWA_QSCALE = LOG2E / math.sqrt(SWA_HD)

LANE = 128
SUBLANE = 8
MXU_DIM = 256
TOK_TILE = MXU_DIM
PROJ_TILE = 1024
PROJ_PART = 256
MOD_TILE = 1024
MLP_TILE = 512
FF_CHUNK = 1024
SWA_WIN = 2 * TOK_TILE
MLA_SUB = MXU_DIM
MLA_Q_TILES = 4
MLA_STAGE_HEADS = 2
SWA_SUB = MXU_DIM
MLA_CHUNK = 32
SWA_CHUNK = 32
MLA_QK_LEAD = 3
SWA_QK_LEAD = 1
VMEM_LIMIT = 56 * 1024 * 1024

F32 = jnp.float32
BF16 = jnp.bfloat16
NT_DIMS = (((1,), (1,)), ((), ()))


def _dot(a, b):
    return jnp.dot(a, b, preferred_element_type=F32)


def _dot_nt(a, b):
    return lax.dot_general(a, b, NT_DIMS, preferred_element_type=F32)


def _mod_kernel(cond_ref, w_ref, b_ref, o_ref):
    cnd = cond_ref[...]
    act = cnd * jax.nn.sigmoid(cnd)
    o_ref[...] = _dot(act.astype(BF16), w_ref[...].astype(BF16)) + b_ref[...]


def _modulation(cond, w_mod, b_mod):
    n = w_mod.shape[1]
    tn = MOD_TILE
    return pl.pallas_call(
        _mod_kernel,
        grid=(n // tn,),
        in_specs=[
            pl.BlockSpec((SUBLANE, D_MODEL), lambda i: (0, 0)),
            pl.BlockSpec((D_MODEL, tn), lambda i: (0, i)),
            pl.BlockSpec((1, tn), lambda i: (0, i)),
        ],
        out_specs=pl.BlockSpec((SUBLANE, tn), lambda i: (0, i)),
        out_shape=jax.ShapeDtypeStruct((SUBLANE, n), F32),
        name="modulation",
    )(cond, w_mod, b_mod)


def _rms_rows(x, n):
    return x * lax.rsqrt(jnp.sum(x * x, axis=-1, keepdims=True) * (1.0 / n) + EPS)


def _rms_cols(x, n):
    return x * lax.rsqrt(jnp.sum(x * x, axis=0, keepdims=True) * (1.0 / n) + EPS)


def _proj_kernel(*refs, latent):
    if latent:
        (x_ref, mod_ref, gattn_ref, wtok_ref, wvs_ref, gkva_ref, wuk_ref, wuv_ref, gmk_ref, gsk_ref,
         wqs_ref, gqa_ref, wuq_ref, gmq_ref, gsq_ref,
         mcos_ref, msin_ref, scos_ref, ssin_ref, kc_ref, ks1_ref, ks2_ref, sc_ref, ss1_ref, ss2_ref,
         wo_f32_ref, w1_f32_ref, w2_f32_ref,
         km_ref, vmt_ref, ks_ref, vst_ref, qmt_ref, qst_ref,
         wo_bf16_ref, w1_bf16_ref, w2_bf16_ref) = refs
        for src, dst in ((wo_f32_ref, wo_bf16_ref), (w1_f32_ref, w1_bf16_ref), (w2_f32_ref, w2_bf16_ref)):
            dst[...] = src[...].astype(BF16)
    else:
        (x_ref, mod_ref, gattn_ref, wtok_ref, wvs_ref, gkva_ref, wuk_ref, wuv_ref, gmk_ref, gsk_ref,
         km_ref, vmt_ref, ks_ref, vst_ref) = refs

    tile = x_ref.shape[1]
    n_part = max(1, tile // PROJ_PART)
    part = tile // n_part

    gain = gattn_ref[...] * (1.0 + mod_ref[0, 1:2, :])
    shift = mod_ref[0, 0:1, :]
    gmk = gmk_ref[...]
    ones_row = (lax.broadcasted_iota(jnp.int32, (MLA_V_AUG - MLA_V, part), 0) == 0).astype(BF16)
    if latent:
        gmq = gmq_ref[...] * MLA_QSCALE
        gsq = gsq_ref[...] * SWA_QSCALE

    def part_phases(lo):
        rows = slice(lo, lo + part)

        h = (_rms_rows(x_ref[0, rows, :], D_MODEL) * gain + shift).astype(BF16)
        yield

        zt = _dot(h, wtok_ref[...])
        vst = _dot_nt(wvs_ref[...], h)
        if latent:
            qst = _dot_nt(wqs_ref[...], h)
        yield

        cq = zt[:, 0:256]
        ckv = zt[:, 256:384]
        krp = zt[:, 384:512]
        ksw = zt[:, 512:640]
        ckvn = (_rms_rows(ckv, KV_LORA) * gkva_ref[...]).astype(BF16)
        if latent:
            cqn = (_rms_rows(cq, Q_LORA) * gqa_ref[...]).astype(BF16)
        kr = krp * gmk
        if latent:
            kr = (kr * kc_ref[rows, :] + pltpu.roll(kr, 112, 1) * ks1_ref[rows, :]
                  + pltpu.roll(kr, 16, 1) * ks2_ref[rows, :])
        ss_kr = jnp.sum(krp * krp, axis=-1, keepdims=True)
        first = lax.broadcasted_iota(jnp.int32, ksw.shape, 1) < SWA_HD
        sq = ksw * ksw
        ss0 = jnp.sum(jnp.where(first, sq, 0.0), axis=-1, keepdims=True)
        ss1 = jnp.sum(jnp.where(first, 0.0, sq), axis=-1, keepdims=True)
        rr = jnp.where(first, lax.rsqrt(ss0 * (1.0 / SWA_HD) + EPS), lax.rsqrt(ss1 * (1.0 / SWA_HD) + EPS))
        ksn = ksw * rr * gsk_ref[...]
        if latent:
            ksn = (ksn * sc_ref[rows, :] + pltpu.roll(ksn, 96, 1) * ss1_ref[rows, :]
                   + pltpu.roll(ksn, 32, 1) * ss2_ref[rows, :])
        ks_ref[0, rows, :] = ksn.astype(BF16)
        for g in range(SWA_KV_HEADS):
            for t in range(part // LANE):
                dst = lo // LANE + t
                vst_ref[0, g, dst, 0:SWA_HD, :] = vst[g * SWA_HD:(g + 1) * SWA_HD, t * LANE:(t + 1) * LANE].astype(BF16)
                vst_ref[0, g, dst, SWA_HD:SWA_V_AUG, :] = ones_row[:, 0:LANE]
        yield

        kpre = _dot(ckvn, wuk_ref[...])
        vt = _dot_nt(wuv_ref[...], ckvn)
        if latent:
            qt = _dot_nt(wuq_ref[...], cqn)
        yield

        for hd in range(MLA_HEADS):
            kp = kpre[:, hd * LANE:(hd + 1) * LANE]
            ss = jnp.sum(kp * kp, axis=-1, keepdims=True) + ss_kr
            kn = (kp * gmk + kr) * lax.rsqrt(ss * (1.0 / MLA_QK) + EPS)
            km_ref[0, hd, rows, :] = kn.astype(BF16)
            vmt_ref[0, hd, 0:MLA_V, rows] = vt[hd * MLA_V:(hd + 1) * MLA_V].astype(BF16)
            vmt_ref[0, hd, MLA_V:MLA_V_AUG, rows] = ones_row
        if latent:
            mcos = mcos_ref[:, rows]
            msin = msin_ref[:, rows]
            for hd in range(MLA_HEADS):
                qn = _rms_cols(qt[hd * LANE:(hd + 1) * LANE], MLA_QK) * gmq
                x1 = qn[64:80]
                x2 = qn[80:96]
                qt_idx = lo // TOK_TILE
                qmt_ref[0, hd, qt_idx, 0:64, :] = qn[0:64].astype(BF16)
                qmt_ref[0, hd, qt_idx, 64:80, :] = (x1 * mcos - x2 * msin).astype(BF16)
                qmt_ref[0, hd, qt_idx, 80:96, :] = (x2 * mcos + x1 * msin).astype(BF16)
                qmt_ref[0, hd, qt_idx, 96:128, :] = jnp.zeros((32, part), BF16)
            scos = scos_ref[:, rows]
            ssin = ssin_ref[:, rows]
            for hd in range(SWA_HEADS):
                qn = _rms_cols(qst[hd * SWA_HD:(hd + 1) * SWA_HD], SWA_HD) * gsq
                x1 = qn[0:32]
                x2 = qn[32:64]
                g = hd // SWA_GROUP
                base = g * SWA_HD
                other = (1 - g) * SWA_HD
                qst_ref[0, hd, base:base + 32, rows] = (x1 * scos - x2 * ssin).astype(BF16)
                qst_ref[0, hd, base + 32:base + 64, rows] = (x2 * scos + x1 * ssin).astype(BF16)
                qst_ref[0, hd, other:other + 64, rows] = jnp.zeros((64, part), BF16)
        yield

    n_phase = 5
    parts = [part_phases(i * part) for i in range(n_part)]
    for step in range(n_phase + n_part - 1):
        for i, gen in enumerate(parts):
            if 0 <= step - i < n_phase:
                next(gen)


def _full(shape):
    nd = len(shape)
    return pl.BlockSpec(shape, lambda b, j: (0,) * nd)


def _project(x, mod3, mod_row, weights, tables, latent, tile):
    bsz, n, _ = x.shape
    nt = n // tile
    if mod_row is None:
        mod_map = lambda b, j: (b, 0, 0)
    else:
        mod_map = lambda b, j: (mod_row, 0, 0)
    common = [weights[k] for k in ("gattn", "wtok", "wvs", "gkva", "wuk", "wuv", "gmk", "gsk")]
    ins = [x, mod3] + common
    in_specs = [
        pl.BlockSpec((1, tile, D_MODEL), lambda b, j: (b, j, 0)),
        pl.BlockSpec((1, 6, D_MODEL), mod_map),
    ] + [_full(w.shape) for w in common]
    out_shape = [
        jax.ShapeDtypeStruct((bsz, MLA_HEADS, n, LANE), BF16),
        jax.ShapeDtypeStruct((bsz, MLA_HEADS, MLA_V_AUG, n), BF16),
        jax.ShapeDtypeStruct((bsz, n, LANE), BF16),
        jax.ShapeDtypeStruct((bsz, SWA_KV_HEADS, n // LANE, SWA_V_AUG, LANE), BF16),
    ]
    out_specs = [
        pl.BlockSpec((1, MLA_HEADS, tile, LANE), lambda b, j: (b, 0, j, 0)),
        pl.BlockSpec((1, MLA_HEADS, MLA_V_AUG, tile), lambda b, j: (b, 0, 0, j)),
        pl.BlockSpec((1, tile, LANE), lambda b, j: (b, j, 0)),
        pl.BlockSpec((1, SWA_KV_HEADS, tile // LANE, SWA_V_AUG, LANE), lambda b, j: (b, 0, j, 0, 0)),
    ]
    if latent:
        extra = [weights[k] for k in ("wqs", "gqa", "wuq", "gmq", "gsq")]
        ins += extra
        in_specs += [_full(w.shape) for w in extra]
        feat = [tables[k] for k in ("mcos", "msin", "scos", "ssin")]
        ins += feat
        in_specs += [pl.BlockSpec((t.shape[0], tile), lambda b, j: (0, j)) for t in feat]
        tok = [tables[k] for k in ("kc", "ks1", "ks2", "sc", "ss1", "ss2")]
        ins += tok
        in_specs += [pl.BlockSpec((tile, LANE), lambda b, j: (j, 0)) for _ in tok]
        assert PROJ_PART == TOK_TILE
        out_shape += [
            jax.ShapeDtypeStruct((bsz, MLA_HEADS, n // TOK_TILE, LANE, TOK_TILE), BF16),
            jax.ShapeDtypeStruct((bsz, SWA_HEADS, LANE, n), BF16),
        ]
        out_specs += [
            pl.BlockSpec((1, MLA_HEADS, tile // TOK_TILE, LANE, TOK_TILE), lambda b, j: (b, 0, j, 0, 0)),
            pl.BlockSpec((1, SWA_HEADS, LANE, tile), lambda b, j: (b, 0, 0, j)),
        ]
        steps = bsz * nt
        for w in weights["to_cast"]:
            rows = w.shape[0] // steps
            assert rows * steps == w.shape[0] and rows % 16 == 0
            ins.append(w)
            in_specs.append(pl.BlockSpec((rows, w.shape[1]), lambda b, j: (b * nt + j, 0)))
            out_shape.append(jax.ShapeDtypeStruct(w.shape, BF16))
            out_specs.append(pl.BlockSpec((rows, w.shape[1]), lambda b, j: (b * nt + j, 0)))
    return pl.pallas_call(
        functools.partial(_proj_kernel, latent=latent),
        grid=(bsz, nt),
        in_specs=in_specs,
        out_specs=out_specs,
        out_shape=out_shape,
        compiler_params=pltpu.CompilerParams(vmem_limit_bytes=VMEM_LIMIT),
        name="project_latent" if latent else "project_context",
    )(*ins)


def _mla_kernel(qt_ref, kc_ref, kl_ref, vct_ref, vlt_ref, o_ref, ot_scr, s_scr, *, seq):
    n_ctx = kc_ref.shape[2]
    tq = TOK_TILE
    key_subs = ([(kc_ref, vct_ref, o) for o in range(0, n_ctx, MLA_SUB)]
                + [(kl_ref, vlt_ref, o) for o in range(0, seq, MLA_SUB)])
    n_key = len(key_subs)
    subs = [(h, j) for h in range(MLA_STAGE_HEADS) for j in range(n_key)]
    groups = MLA_HEADS // MLA_STAGE_HEADS

    def scores_into(slot, idx, stage, m8):
        t, grp = stage
        h, j = subs[idx]
        hd = grp * MLA_STAGE_HEADS + h
        k_ref, _, off = key_subs[j]
        sj = _dot(k_ref[0, hd, off:off + MLA_SUB, :], qt_ref[0, hd, t])
        s_scr[slot, idx * MLA_SUB:(idx + 1) * MLA_SUB, :] = sj
        mj = jnp.max(sj.reshape(MLA_SUB // SUBLANE, SUBLANE, tq), axis=0)
        m8 = list(m8)
        m8[h] = mj if j == 0 else jnp.maximum(m8[h], mj)
        return tuple(m8)

    def stage_body(stage, m8_cur, slot, nxt):
        t, grp = stage
        n_q = len(subs) if nxt is not None else 0
        m8_next = (None,) * MLA_STAGE_HEADS
        for idx in range(min(MLA_QK_LEAD, n_q)):
            m8_next = scores_into(1 - slot, idx, nxt, m8_next)
        for idx, (h, j) in enumerate(subs):
            hd = grp * MLA_STAGE_HEADS + h
            if j == 0:
                mb = jnp.broadcast_to(jnp.max(m8_cur[h], axis=0, keepdims=True), (MLA_CHUNK, tq))
                acc = None
            base = idx * MLA_SUB
            p = jnp.concatenate(
                [jnp.exp2(s_scr[slot, r:r + MLA_CHUNK, :] - mb).astype(BF16)
                 for r in range(base, base + MLA_SUB, MLA_CHUNK)], axis=0)
            _, v_ref, off = key_subs[j]
            d = _dot(v_ref[0, hd, :, off:off + MLA_SUB], p)
            acc = d if acc is None else acc + d
            if j == n_key - 1:
                row = pl.multiple_of(hd * MLA_V, MLA_V)
                ot_scr[t, pl.ds(row, MLA_V), :] = acc[0:MLA_V] / acc[MLA_V:MLA_V + 1]
            if idx + MLA_QK_LEAD < n_q:
                m8_next = scores_into(1 - slot, idx + MLA_QK_LEAD, nxt, m8_next)
        return m8_next

    def stage_of(n):
        return n // groups, lax.rem(n, groups)

    def stage_step(n, m8):
        return lax.cond(lax.rem(n, 2) == 0,
                        lambda m: stage_body(stage_of(n), m, 0, stage_of(n + 1)),
                        lambda m: stage_body(stage_of(n), m, 1, stage_of(n + 1)), m8)

    n_stage = groups * MLA_Q_TILES
    m8_first = (None,) * MLA_STAGE_HEADS
    for idx in range(len(subs)):
        m8_first = scores_into(0, idx, (0, 0), m8_first)
    m8_last = lax.fori_loop(0, n_stage - 1, stage_step, m8_first)
    stage_body((MLA_Q_TILES - 1, groups - 1), m8_last, (n_stage - 1) % 2, None)
    for t in range(MLA_Q_TILES):
        o_ref[0, t * tq:(t + 1) * tq, :] = ot_scr[t].T.astype(BF16)


def _mla_attention(qmt, km_c, km_l, vmt_c, vmt_l):
    bsz = qmt.shape[0]
    s = km_l.shape[2]
    n_ctx = km_c.shape[2]
    step_q = MLA_Q_TILES * TOK_TILE
    assert s % step_q == 0
    n_keys = n_ctx + s
    return pl.pallas_call(
        functools.partial(_mla_kernel, seq=s),
        grid=(bsz, s // step_q),
        in_specs=[
            pl.BlockSpec((1, MLA_HEADS, MLA_Q_TILES, LANE, TOK_TILE), lambda b, j: (b, 0, j, 0, 0)),
            pl.BlockSpec((1, MLA_HEADS, n_ctx, LANE), lambda b, j: (b, 0, 0, 0)),
            pl.BlockSpec((1, MLA_HEADS, s, LANE), lambda b, j: (b, 0, 0, 0)),
            pl.BlockSpec((1, MLA_HEADS, MLA_V_AUG, n_ctx), lambda b, j: (b, 0, 0, 0)),
            pl.BlockSpec((1, MLA_HEADS, MLA_V_AUG, s), lambda b, j: (b, 0, 0, 0)),
        ],
        out_specs=pl.BlockSpec((1, step_q, MLA_HEADS * MLA_V), lambda b, j: (b, j, 0)),
        out_shape=jax.ShapeDtypeStruct((bsz, s, MLA_HEADS * MLA_V), BF16),
        scratch_shapes=[
            pltpu.VMEM((MLA_Q_TILES, MLA_HEADS * MLA_V, TOK_TILE), F32),
            pltpu.VMEM((2, MLA_STAGE_HEADS * n_keys, TOK_TILE), F32),
        ],
        compiler_params=pltpu.CompilerParams(vmem_limit_bytes=VMEM_LIMIT,
                                             dimension_semantics=("arbitrary", "arbitrary")),
        name="mla_attention",
    )(qmt, km_c, km_l, vmt_c, vmt_l)


def _swa_kernel(sink_ref, bias_ref, qt_ref, qtn_ref, kc_ref, kl_ref, vct_ref, vlt_ref, o_ref,
                ot_scr, s_scr, m8_scr, *, seq):
    j = pl.program_id(1)
    nt = pl.num_programs(1)
    n_ctx = kc_ref.shape[1]
    tq = qt_ref.shape[3]
    n_band = SWA_WIN // SWA_SUB
    n_t = n_band + n_ctx // SWA_SUB
    lanes_per_sub = SWA_SUB // LANE
    subs = [(hh, t) for hh in range(SWA_GROUP) for t in range(n_t)]

    def window(jj):
        start = pl.multiple_of(jnp.clip(jj * TOK_TILE - WINDOW, 0, seq - SWA_WIN), LANE)
        variant = jnp.where(jj == 0, 0, jnp.where(jj == nt - 1, 2, 1))
        return start, variant

    def rows_of(idx):
        return slice(idx * SWA_SUB, (idx + 1) * SWA_SUB)

    def scores_into(slot, idx, q_ref, g, win):
        hh, t = subs[idx]
        start, variant = win
        if t < n_band:
            k = kl_ref[0, pl.ds(start + t * SWA_SUB, SWA_SUB), :]
        else:
            k = kc_ref[0, (t - n_band) * SWA_SUB:(t - n_band + 1) * SWA_SUB, :]
        sj = _dot(k, q_ref[0, g * SWA_GROUP + hh])
        if t < n_band:
            sj = sj + bias_ref[variant, t * SWA_SUB:(t + 1) * SWA_SUB, :]
        s_scr[slot, rows_of(idx), :] = sj
        return jnp.max(sj.reshape(SWA_SUB // SUBLANE, SUBLANE, tq), axis=0)

    def values_t(g, t, t0):
        if t < n_band:
            tiles = [vlt_ref[0, g, t0 + t * lanes_per_sub + i] for i in range(lanes_per_sub)]
        else:
            tiles = [vct_ref[0, g, (t - n_band) * lanes_per_sub + i] for i in range(lanes_per_sub)]
        return jnp.concatenate(tiles, axis=1)

    def next_scores(slot, idx, nxt, m8n):
        q_ref, g_n, win_n = nxt
        mj = scores_into(slot, idx, q_ref, g_n, win_n)
        hh, t = subs[idx]
        m8n[hh] = mj if t == 0 else jnp.maximum(m8n[hh], mj)
        if t == n_t - 1:
            m8_scr[slot, hh] = m8n[hh]

    def stage_body(g, win_cur, nxt):
        slot = g
        t0 = win_cur[0] // LANE
        n_q = len(subs) if nxt is not None else 0
        m8n = {}
        for idx in range(min(SWA_QK_LEAD, n_q)):
            next_scores(1 - slot, idx, nxt, m8n)
        for idx, (hh, t) in enumerate(subs):
            hd = g * SWA_GROUP + hh
            if t == 0:
                sk = sink_ref[hd] * LOG2E
                m = jnp.maximum(jnp.max(m8_scr[slot, hh], axis=0, keepdims=True), sk)
                mb = jnp.broadcast_to(m, (SWA_CHUNK, tq))
                acc = None
            base = idx * SWA_SUB
            p = jnp.concatenate(
                [jnp.exp2(s_scr[slot, r:r + SWA_CHUNK, :] - mb).astype(BF16)
                 for r in range(base, base + SWA_SUB, SWA_CHUNK)], axis=0)
            d = _dot(values_t(g, t, t0), p)
            acc = d if acc is None else acc + d
            if t == n_t - 1:
                denom = acc[SWA_HD:SWA_HD + 1] + jnp.exp2(sk - m)
                ot_scr[hd * SWA_HD:(hd + 1) * SWA_HD, :] = acc[0:SWA_HD] / denom
            if idx + SWA_QK_LEAD < n_q:
                next_scores(1 - slot, idx + SWA_QK_LEAD, nxt, m8n)

    win = window(j)

    @pl.when(j == 0)
    def _():
        m8n = {}
        for idx in range(len(subs)):
            next_scores(0, idx, (qt_ref, 0, win), m8n)

    stage_body(0, win, (qt_ref, 1, win))

    @pl.when(j < nt - 1)
    def _():
        stage_body(1, win, (qtn_ref, 0, window(j + 1)))

    @pl.when(j == nt - 1)
    def _():
        stage_body(1, win, None)

    o_ref[0] = ot_scr[...].T.astype(BF16)


def _swa_bias(seq):
    nt = seq // TOK_TILE
    out = []
    for j in (0, 1, nt - 1):
        start = min(max(j * TOK_TILE - WINDOW, 0), seq - SWA_WIN)
        qpos = j * TOK_TILE + np.arange(TOK_TILE)[None, :]
        kpos = start + np.arange(SWA_WIN)[:, None]
        out.append(np.where(np.abs(qpos - kpos) <= WINDOW, 0.0, NEG_INF))
    return jnp.asarray(np.stack(out), F32)


def _swa_attention(sink, qst, ks_c, ks_l, vst_c, vst_l):
    bsz, _, _, s = qst.shape
    n_ctx = ks_c.shape[1]
    nt = s // TOK_TILE
    assert nt >= 3 and s >= SWA_WIN and n_ctx % SWA_SUB == 0
    stage_rows = SWA_GROUP * (SWA_WIN + n_ctx)
    return pl.pallas_call(
        functools.partial(_swa_kernel, seq=s),
        grid=(bsz, nt),
        in_specs=[
            pl.BlockSpec(memory_space=pltpu.SMEM),
            pl.BlockSpec((3, SWA_WIN, TOK_TILE), lambda b, j: (0, 0, 0)),
            pl.BlockSpec((1, SWA_HEADS, LANE, TOK_TILE), lambda b, j: (b, 0, 0, j)),
            pl.BlockSpec((1, SWA_GROUP, LANE, TOK_TILE), lambda b, j: (b, 0, 0, jnp.minimum(j + 1, nt - 1))),
            pl.BlockSpec((1, n_ctx, LANE), lambda b, j: (b, 0, 0)),
            pl.BlockSpec((1, s, LANE), lambda b, j: (b, 0, 0)),
            pl.BlockSpec((1, SWA_KV_HEADS, n_ctx // LANE, SWA_V_AUG, LANE), lambda b, j: (b, 0, 0, 0, 0)),
            pl.BlockSpec((1, SWA_KV_HEADS, s // LANE, SWA_V_AUG, LANE), lambda b, j: (b, 0, 0, 0, 0)),
        ],
        out_specs=pl.BlockSpec((1, TOK_TILE, SWA_HEADS * SWA_HD), lambda b, j: (b, j, 0)),
        out_shape=jax.ShapeDtypeStruct((bsz, s, SWA_HEADS * SWA_HD), BF16),
        scratch_shapes=[
            pltpu.VMEM((SWA_HEADS * SWA_HD, TOK_TILE), F32),
            pltpu.VMEM((2, stage_rows, TOK_TILE), F32),
            pltpu.VMEM((2, SWA_GROUP, SUBLANE, TOK_TILE), F32),
        ],
        compiler_params=pltpu.CompilerParams(vmem_limit_bytes=VMEM_LIMIT,
                                             dimension_semantics=("arbitrary", "arbitrary")),
        name="swa_attention",
    )(sink, _swa_bias(s), qst, qst, ks_c, ks_l, vst_c, vst_l)


def _mlp_kernel(x_ref, mm_ref, ms_ref, mod_ref, wo_ref, gmlp_ref, w1_ref, w2_ref, o_ref, acc_ref):
    half = MLA_HEADS * MLA_V
    y1 = _dot(mm_ref[0], wo_ref[0:half, :]) + _dot(ms_ref[0], wo_ref[half:, :])
    x1 = x_ref[0] + mod_ref[0, 2:3, :] * y1
    h2 = (_rms_rows(x1, D_MODEL) * gmlp_ref[...] * (1.0 + mod_ref[0, 4:5, :])
          + mod_ref[0, 3:4, :]).astype(BF16)
    for c in range(D_FF // FF_CHUNK):
        a = jnp.maximum(_dot(h2, w1_ref[:, c * FF_CHUNK:(c + 1) * FF_CHUNK]), 0.0)
        part = _dot((a * a).astype(BF16), w2_ref[c * FF_CHUNK:(c + 1) * FF_CHUNK, :])
        if c == 0:
            acc_ref[...] = part
        else:
            acc_ref[...] += part
    o_ref[0] = x1 + mod_ref[0, 5:6, :] * acc_ref[...]


def _outproj_mlp(x, mix_m, mix_s, mod3, w_out, g_mlp, w1, w2):
    bsz, s, _ = x.shape
    half = MLA_HEADS * MLA_V
    const = lambda b, j: (0, 0)
    single = pl.Buffered(1)
    return pl.pallas_call(
        _mlp_kernel,
        grid=(bsz, s // MLP_TILE),
        in_specs=[
            pl.BlockSpec((1, MLP_TILE, D_MODEL), lambda b, j: (b, j, 0)),
            pl.BlockSpec((1, MLP_TILE, half), lambda b, j: (b, j, 0)),
            pl.BlockSpec((1, MLP_TILE, half), lambda b, j: (b, j, 0)),
            pl.BlockSpec((1, 6, D_MODEL), lambda b, j: (b, 0, 0)),
            pl.BlockSpec((D_MODEL, D_MODEL), const, pipeline_mode=single),
            pl.BlockSpec((1, D_MODEL), const),
            pl.BlockSpec((D_MODEL, D_FF), const, pipeline_mode=single),
            pl.BlockSpec((D_FF, D_MODEL), const, pipeline_mode=single),
        ],
        out_specs=pl.BlockSpec((1, MLP_TILE, D_MODEL), lambda b, j: (b, j, 0)),
        out_shape=jax.ShapeDtypeStruct((bsz, s, D_MODEL), F32),
        scratch_shapes=[pltpu.VMEM((MLP_TILE, D_MODEL), F32)],
        compiler_params=pltpu.CompilerParams(vmem_limit_bytes=VMEM_LIMIT),
        name="outproj_mlp",
    )(x, mix_m, mix_s, mod3, w_out, g_mlp, w1, w2)


def _rope_tables(seq):
    f32 = np.float32

    def cos_sin(rot_dim):
        n_freq = rot_dim // 4
        inv = f32(ROPE_THETA) ** (-np.arange(n_freq, dtype=f32) / f32(n_freq))
        rows = seq // GRID_W
        row = np.repeat(np.arange(rows, dtype=f32), GRID_W)
        col = np.tile(np.arange(GRID_W, dtype=f32), rows)
        ang = np.concatenate([row[:, None] * inv, col[:, None] * inv], axis=-1).astype(f32)
        return np.cos(ang).astype(f32), np.sin(ang).astype(f32)

    mcos, msin = cos_sin(MLA_ROPE)
    scos, ssin = cos_sin(SWA_HD)
    z16 = np.zeros((seq, 16), f32)
    z32 = np.zeros((seq, 32), f32)
    z64 = np.zeros((seq, 64), f32)
    kc = np.concatenate([np.ones((seq, 64), f32), mcos, mcos, np.ones((seq, 32), f32)], axis=1)
    ks1 = np.concatenate([z64, -msin, z16, z32], axis=1)
    ks2 = np.concatenate([z64, z16, msin, z32], axis=1)
    sc = np.concatenate([scos, scos, scos, scos], axis=1)
    ss1 = np.concatenate([-ssin, z32, -ssin, z32], axis=1)
    ss2 = np.concatenate([z32, ssin, z32, ssin], axis=1)
    tabs = dict(mcos=mcos.T, msin=msin.T, scos=scos.T, ssin=ssin.T,
                kc=kc, ks1=ks1, ks2=ks2, sc=sc, ss1=ss1, ss2=ss2)
    return {k: jnp.asarray(np.ascontiguousarray(v)) for k, v in tabs.items()}


def _prep_weights(g_attn, w_in, g_q_a, w_uq, g_kv_a, w_ukv, g_mla_q, g_mla_k, g_swa_q, g_swa_k):
    o_ckv = Q_LORA
    o_kr = o_ckv + KV_LORA
    o_qs = o_kr + MLA_ROPE
    o_ks = o_qs + SWA_HEADS * SWA_HD
    o_vs = o_ks + SWA_KV_HEADS * SWA_HD
    w_cq = w_in[:, :o_ckv]
    w_ckv = w_in[:, o_ckv:o_kr]
    w_kr = w_in[:, o_kr:o_qs]
    w_qs = w_in[:, o_qs:o_ks]
    w_ks = w_in[:, o_ks:o_vs]
    w_vs = w_in[:, o_vs:]
    w_kr_p = jnp.pad(w_kr, ((0, 0), (MLA_NOPE, LANE - MLA_QK)))
    wtok = jnp.concatenate([w_cq, w_ckv, w_kr_p, w_ks], axis=1).astype(BF16)
    w_ukv_h = w_ukv.reshape(KV_LORA, MLA_HEADS, MLA_NOPE + MLA_V)
    wuk = jnp.pad(w_ukv_h[:, :, :MLA_NOPE], ((0, 0), (0, 0), (0, LANE - MLA_NOPE)))
    wuk = wuk.reshape(KV_LORA, MLA_HEADS * LANE).astype(BF16)
    wuv = w_ukv_h[:, :, MLA_NOPE:].reshape(KV_LORA, MLA_HEADS * MLA_V).T.astype(BF16)
    w_uq_h = jnp.pad(w_uq.reshape(Q_LORA, MLA_HEADS, MLA_QK), ((0, 0), (0, 0), (0, LANE - MLA_QK)))
    wuq = w_uq_h.reshape(Q_LORA, MLA_HEADS * LANE).T.astype(BF16)
    return dict(
        gattn=g_attn[None, :], wtok=wtok, wvs=w_vs.T.astype(BF16), gkva=g_kv_a[None, :],
        wuk=wuk, wuv=wuv, gmk=jnp.pad(g_mla_k, (0, LANE - MLA_QK))[None, :],
        gsk=jnp.tile(g_swa_k, SWA_KV_HEADS)[None, :],
        wqs=w_qs.T.astype(BF16), gqa=g_q_a[None, :], wuq=wuq,
        gmq=jnp.pad(g_mla_q, (0, LANE - MLA_QK))[:, None], gsq=g_swa_q[:, None],
    )


def kernel(x, c, ctx, c_ctx, w_mod, b_mod, g_attn, w_in, g_q_a, w_uq, g_kv_a, w_ukv, g_mla_q, g_mla_k,
           g_swa_q, g_swa_k, swa_sink, w_out, g_mlp, w_mlp1, w_mlp2):
    bsz, seq, _ = x.shape
    assert w_mod.shape[0] == 1, "single-layer block"
    assert bsz < SUBLANE and seq % PROJ_TILE == 0 and seq % MLP_TILE == 0 and ctx.shape[1] % TOK_TILE == 0

    cond = jnp.zeros((SUBLANE, D_MODEL), F32).at[:bsz].set(c).at[bsz].set(c_ctx)
    mod = _modulation(cond, w_mod[0], b_mod[0][None, :])
    mod3 = mod.reshape(SUBLANE, 6, D_MODEL)

    weights = _prep_weights(g_attn[0], w_in[0], g_q_a[0], w_uq[0], g_kv_a[0], w_ukv[0],
                            g_mla_q[0], g_mla_k[0], g_swa_q[0], g_swa_k[0])
    tables = _rope_tables(seq)

    weights["to_cast"] = [w_out[0], w_mlp1[0], w_mlp2[0]]
    (km_l, vmt_l, ks_l, vst_l, qmt, qst,
     w_out_bf16, w_mlp1_bf16, w_mlp2_bf16) = _project(x, mod3, None, weights, tables, latent=True, tile=PROJ_TILE)
    km_c, vmt_c, ks_c, vst_c = _project(ctx, mod3, bsz, weights, None, latent=False, tile=TOK_TILE)

    mix_m = _mla_attention(qmt, km_c, km_l, vmt_c, vmt_l)
    mix_s = _swa_attention(swa_sink[0], qst, ks_c, ks_l, vst_c, vst_l)

    return _outproj_mlp(x, mix_m, mix_s, mod3, w_out_bf16, g_mlp[0][None, :], w_mlp1_bf16, w_mlp2_bf16)
```

```python
import functools
import math

import jax
import jax.numpy as jnp
import numpy as np
from jax import lax
from jax.experimental import pallas as pl
from jax.experimental.pallas import tpu as pltpu

D_MODEL = 1024
GRID_W = 64
MLA_HEADS = 8
MLA_NOPE = 64
MLA_ROPE = 32
MLA_QK = MLA_NOPE + MLA_ROPE
MLA_V = 64
MLA_V_AUG = MLA_V + 16
Q_LORA = 256
KV_LORA = 128
SWA_HEADS = 8
SWA_KV_HEADS = 2
SWA_GROUP = SWA_HEADS // SWA_KV_HEADS
SWA_HD = 64
SWA_V_AUG = SWA_HD + 16
WINDOW = 128
D_FF = 4 * D_MODEL
ROPE_THETA = 10000.0
EPS = 1e-6
NEG_INF = -1e30
LOG2E = 1.4426950408889634
MLA_QSCALE = LOG2E / math.sqrt(MLA_QK)
SWA_QSCALE = LOG2E / math.sqrt(SWA_HD)

LANE = 128
SUBLANE = 8
MXU_DIM = 256
TOK_TILE = MXU_DIM
PROJ_TILE = 1024
PROJ_PART = 256
MOD_TILE = 1024
MLP_TILE = 512
FF_CHUNK = 1024
SWA_WIN = 2 * TOK_TILE
MLA_SUB = MXU_DIM
MLA_Q_TILES = 4
MLA_STAGE_HEADS = 2
SWA_SUB = MXU_DIM
MLA_CHUNK = 32
SWA_CHUNK = 32
MLA_QK_LEAD = 3
SWA_QK_LEAD = 1
VMEM_LIMIT = 56 * 1024 * 1024

F32 = jnp.float32
BF16 = jnp.bfloat16
NT_DIMS = (((1,), (1,)), ((), ()))


def _dot(a, b):
    return jnp.dot(a, b, preferred_element_type=F32)


def _dot_nt(a, b):
    return lax.dot_general(a, b, NT_DIMS, preferred_element_type=F32)


def _mod_kernel(cond_ref, w_ref, b_ref, o_ref):
    cnd = cond_ref[...]
    act = cnd * jax.nn.sigmoid(cnd)
    o_ref[...] = _dot(act.astype(BF16), w_ref[...].astype(BF16)) + b_ref[...]


def _modulation(cond, w_mod, b_mod):
    n = w_mod.shape[1]
    tn = MOD_TILE
    return pl.pallas_call(
        _mod_kernel,
        grid=(n // tn,),
        in_specs=[
            pl.BlockSpec((SUBLANE, D_MODEL), lambda i: (0, 0)),
            pl.BlockSpec((D_MODEL, tn), lambda i: (0, i)),
            pl.BlockSpec((1, tn), lambda i: (0, i)),
        ],
        out_specs=pl.BlockSpec((SUBLANE, tn), lambda i: (0, i)),
        out_shape=jax.ShapeDtypeStruct((SUBLANE, n), F32),
        name="modulation",
    )(cond, w_mod, b_mod)


def _rms_rows(x, n):
    return x * lax.rsqrt(jnp.sum(x * x, axis=-1, keepdims=True) * (1.0 / n) + EPS)


def _rms_cols(x, n):
    return x * lax.rsqrt(jnp.sum(x * x, axis=0, keepdims=True) * (1.0 / n) + EPS)


def _proj_kernel(*refs, latent):
    if latent:
        (x_ref, mod_ref, gattn_ref, wtok_ref, wvs_ref, gkva_ref, wuk_ref, wuv_ref, gmk_ref, gsk_ref,
         wqs_ref, gqa_ref, wuq_ref, gmq_ref, gsq_ref,
         mcos_ref, msin_ref, scos_ref, ssin_ref, kc_ref, ks1_ref, ks2_ref, sc_ref, ss1_ref, ss2_ref,
         wo_f32_ref, w1_f32_ref, w2_f32_ref,
         km_ref, vmt_ref, ks_ref, vst_ref, qmt_ref, qst_ref,
         wo_bf16_ref, w1_bf16_ref, w2_bf16_ref) = refs
        for src, dst in ((wo_f32_ref, wo_bf16_ref), (w1_f32_ref, w1_bf16_ref), (w2_f32_ref, w2_bf16_ref)):
            dst[...] = src[...].astype(BF16)
    else:
        (x_ref, mod_ref, gattn_ref, wtok_ref, wvs_ref, gkva_ref, wuk_ref, wuv_ref, gmk_ref, gsk_ref,
         km_ref, vmt_ref, ks_ref, vst_ref) = refs

    tile = x_ref.shape[1]
    n_part = max(1, tile // PROJ_PART)
    part = tile // n_part

    gain = gattn_ref[...] * (1.0 + mod_ref[0, 1:2, :])
    shift = mod_ref[0, 0:1, :]
    gmk = gmk_ref[...]
    ones_row = (lax.broadcasted_iota(jnp.int32, (MLA_V_AUG - MLA_V, part), 0) == 0).astype(BF16)
    if latent:
        gmq = gmq_ref[...] * MLA_QSCALE
        gsq = gsq_ref[...] * SWA_QSCALE

    def part_phases(lo):
        rows = slice(lo, lo + part)

        h = (_rms_rows(x_ref[0, rows, :], D_MODEL) * gain + shift).astype(BF16)
        yield

        zt = _dot(h, wtok_ref[...])
        vst = _dot_nt(wvs_ref[...], h)
        if latent:
            qst = _dot_nt(wqs_ref[...], h)
        yield

        cq = zt[:, 0:256]
        ckv = zt[:, 256:384]
        krp = zt[:, 384:512]
        ksw = zt[:, 512:640]
        ckvn = (_rms_rows(ckv, KV_LORA) * gkva_ref[...]).astype(BF16)
        if latent:
            cqn = (_rms_rows(cq, Q_LORA) * gqa_ref[...]).astype(BF16)
        kr = krp * gmk
        if latent:
            kr = (kr * kc_ref[rows, :] + pltpu.roll(kr, 112, 1) * ks1_ref[rows, :]
                  + pltpu.roll(kr, 16, 1) * ks2_ref[rows, :])
        ss_kr = jnp.sum(krp * krp, axis=-1, keepdims=True)
        first = lax.broadcasted_iota(jnp.int32, ksw.shape, 1) < SWA_HD
        sq = ksw * ksw
        ss0 = jnp.sum(jnp.where(first, sq, 0.0), axis=-1, keepdims=True)
        ss1 = jnp.sum(jnp.where(first, 0.0, sq), axis=-1, keepdims=True)
        rr = jnp.where(first, lax.rsqrt(ss0 * (1.0 / SWA_HD) + EPS), lax.rsqrt(ss1 * (1.0 / SWA_HD) + EPS))
        ksn = ksw * rr * gsk_ref[...]
        if latent:
            ksn = (ksn * sc_ref[rows, :] + pltpu.roll(ksn, 96, 1) * ss1_ref[rows, :]
                   + pltpu.roll(ksn, 32, 1) * ss2_ref[rows, :])
        ks_ref[0, rows, :] = ksn.astype(BF16)
        for g in range(SWA_KV_HEADS):
            for t in range(part // LANE):
                dst = lo // LANE + t
                vst_ref[0, g, dst, 0:SWA_HD, :] = vst[g * SWA_HD:(g + 1) * SWA_HD, t * LANE:(t + 1) * LANE].astype(BF16)
                vst_ref[0, g, dst, SWA_HD:SWA_V_AUG, :] = ones_row[:, 0:LANE]
        yield

        kpre = _dot(ckvn, wuk_ref[...])
        vt = _dot_nt(wuv_ref[...], ckvn)
        if latent:
            qt = _dot_nt(wuq_ref[...], cqn)
        yield

        for hd in range(MLA_HEADS):
            kp = kpre[:, hd * LANE:(hd + 1) * LANE]
            ss = jnp.sum(kp * kp, axis=-1, keepdims=True) + ss_kr
            kn = (kp * gmk + kr) * lax.rsqrt(ss * (1.0 / MLA_QK) + EPS)
            km_ref[0, hd, rows, :] = kn.astype(BF16)
            vmt_ref[0, hd, 0:MLA_V, rows] = vt[hd * MLA_V:(hd + 1) * MLA_V].astype(BF16)
            vmt_ref[0, hd, MLA_V:MLA_V_AUG, rows] = ones_row
        if latent:
            mcos = mcos_ref[:, rows]
            msin = msin_ref[:, rows]
            for hd in range(MLA_HEADS):
                qn = _rms_cols(qt[hd * LANE:(hd + 1) * LANE], MLA_QK) * gmq
                x1 = qn[64:80]
                x2 = qn[80:96]
                qt_idx = lo // TOK_TILE
                qmt_ref[0, hd, qt_idx, 0:64, :] = qn[0:64].astype(BF16)
                qmt_ref[0, hd, qt_idx, 64:80, :] = (x1 * mcos - x2 * msin).astype(BF16)
                qmt_ref[0, hd, qt_idx, 80:96, :] = (x2 * mcos + x1 * msin).astype(BF16)
                qmt_ref[0, hd, qt_idx, 96:128, :] = jnp.zeros((32, part), BF16)
            scos = scos_ref[:, rows]
            ssin = ssin_ref[:, rows]
            for hd in range(SWA_HEADS):
                qn = _rms_cols(qst[hd * SWA_HD:(hd + 1) * SWA_HD], SWA_HD) * gsq
                x1 = qn[0:32]
                x2 = qn[32:64]
                g = hd // SWA_GROUP
                base = g * SWA_HD
                other = (1 - g) * SWA_HD
                qst_ref[0, hd, base:base + 32, rows] = (x1 * scos - x2 * ssin).astype(BF16)
                qst_ref[0, hd, base + 32:base + 64, rows] = (x2 * scos + x1 * ssin).astype(BF16)
                qst_ref[0, hd, other:other + 64, rows] = jnp.zeros((64, part), BF16)
        yield

    n_phase = 5
    parts = [part_phases(i * part) for i in range(n_part)]
    for step in range(n_phase + n_part - 1):
        for i, gen in enumerate(parts):
            if 0 <= step - i < n_phase:
                next(gen)


def _full(shape):
    nd = len(shape)
    return pl.BlockSpec(shape, lambda b, j: (0,) * nd)


def _project(x, mod3, mod_row, weights, tables, latent, tile):
    bsz, n, _ = x.shape
    nt = n // tile
    if mod_row is None:
        mod_map = lambda b, j: (b, 0, 0)
    else:
        mod_map = lambda b, j: (mod_row, 0, 0)
    common = [weights[k] for k in ("gattn", "wtok", "wvs", "gkva", "wuk", "wuv", "gmk", "gsk")]
    ins = [x, mod3] + common
    in_specs = [
        pl.BlockSpec((1, tile, D_MODEL), lambda b, j: (b, j, 0)),
        pl.BlockSpec((1, 6, D_MODEL), mod_map),
    ] + [_full(w.shape) for w in common]
    out_shape = [
        jax.ShapeDtypeStruct((bsz, MLA_HEADS, n, LANE), BF16),
        jax.ShapeDtypeStruct((bsz, MLA_HEADS, MLA_V_AUG, n), BF16),
        jax.ShapeDtypeStruct((bsz, n, LANE), BF16),
        jax.ShapeDtypeStruct((bsz, SWA_KV_HEADS, n // LANE, SWA_V_AUG, LANE), BF16),
    ]
    out_specs = [
        pl.BlockSpec((1, MLA_HEADS, tile, LANE), lambda b, j: (b, 0, j, 0)),
        pl.BlockSpec((1, MLA_HEADS, MLA_V_AUG, tile), lambda b, j: (b, 0, 0, j)),
        pl.BlockSpec((1, tile, LANE), lambda b, j: (b, j, 0)),
        pl.BlockSpec((1, SWA_KV_HEADS, tile // LANE, SWA_V_AUG, LANE), lambda b, j: (b, 0, j, 0, 0)),
    ]
    if latent:
        extra = [weights[k] for k in ("wqs", "gqa", "wuq", "gmq", "gsq")]
        ins += extra
        in_specs += [_full(w.shape) for w in extra]
        feat = [tables[k] for k in ("mcos", "msin", "scos", "ssin")]
        ins += feat
        in_specs += [pl.BlockSpec((t.shape[0], tile), lambda b, j: (0, j)) for t in feat]
        tok = [tables[k] for k in ("kc", "ks1", "ks2", "sc", "ss1", "ss2")]
        ins += tok
        in_specs += [pl.BlockSpec((tile, LANE), lambda b, j: (j, 0)) for _ in tok]
        assert PROJ_PART == TOK_TILE
        out_shape += [
            jax.ShapeDtypeStruct((bsz, MLA_HEADS, n // TOK_TILE, LANE, TOK_TILE), BF16),
            jax.ShapeDtypeStruct((bsz, SWA_HEADS, LANE, n), BF16),
        ]
        out_specs += [
            pl.BlockSpec((1, MLA_HEADS, tile // TOK_TILE, LANE, TOK_TILE), lambda b, j: (b, 0, j, 0, 0)),
            pl.BlockSpec((1, SWA_HEADS, LANE, tile), lambda b, j: (b, 0, 0, j)),
        ]
        steps = bsz * nt
        for w in weights["to_cast"]:
            rows = w.shape[0] // steps
            assert rows * steps == w.shape[0] and rows % 16 == 0
            ins.append(w)
            in_specs.append(pl.BlockSpec((rows, w.shape[1]), lambda b, j: (b * nt + j, 0)))
            out_shape.append(jax.ShapeDtypeStruct(w.shape, BF16))
            out_specs.append(pl.BlockSpec((rows, w.shape[1]), lambda b, j: (b * nt + j, 0)))
    return pl.pallas_call(
        functools.partial(_proj_kernel, latent=latent),
        grid=(bsz, nt),
        in_specs=in_specs,
        out_specs=out_specs,
        out_shape=out_shape,
        compiler_params=pltpu.CompilerParams(vmem_limit_bytes=VMEM_LIMIT),
        name="project_latent" if latent else "project_context",
    )(*ins)


def _mla_kernel(qt_ref, kc_ref, kl_ref, vct_ref, vlt_ref, o_ref, ot_scr, s_scr, *, seq):
    n_ctx = kc_ref.shape[2]
    tq = TOK_TILE
    key_subs = ([(kc_ref, vct_ref, o) for o in range(0, n_ctx, MLA_SUB)]
                + [(kl_ref, vlt_ref, o) for o in range(0, seq, MLA_SUB)])
    n_key = len(key_subs)
    subs = [(h, j) for h in range(MLA_STAGE_HEADS) for j in range(n_key)]
    groups = MLA_HEADS // MLA_STAGE_HEADS

    def scores_into(slot, idx, stage, m8):
        t, grp = stage
        h, j = subs[idx]
        hd = grp * MLA_STAGE_HEADS + h
        k_ref, _, off = key_subs[j]
        sj = _dot(k_ref[0, hd, off:off + MLA_SUB, :], qt_ref[0, hd, t])
        s_scr[slot, idx * MLA_SUB:(idx + 1) * MLA_SUB, :] = sj
        mj = jnp.max(sj.reshape(MLA_SUB // SUBLANE, SUBLANE, tq), axis=0)
        m8 = list(m8)
        m8[h] = mj if j == 0 else jnp.maximum(m8[h], mj)
        return tuple(m8)

    def stage_body(stage, m8_cur, slot, nxt):
        t, grp = stage
        n_q = len(subs) if nxt is not None else 0
        m8_next = (None,) * MLA_STAGE_HEADS
        for idx in range(min(MLA_QK_LEAD, n_q)):
            m8_next = scores_into(1 - slot, idx, nxt, m8_next)
        for idx, (h, j) in enumerate(subs):
            hd = grp * MLA_STAGE_HEADS + h
            if j == 0:
                mb = jnp.broadcast_to(jnp.max(m8_cur[h], axis=0, keepdims=True), (MLA_CHUNK, tq))
                acc = None
            base = idx * MLA_SUB
            p = jnp.concatenate(
                [jnp.exp2(s_scr[slot, r:r + MLA_CHUNK, :] - mb).astype(BF16)
                 for r in range(base, base + MLA_SUB, MLA_CHUNK)], axis=0)
            _, v_ref, off = key_subs[j]
            d = _dot(v_ref[0, hd, :, off:off + MLA_SUB], p)
            acc = d if acc is None else acc + d
            if j == n_key - 1:
                row = pl.multiple_of(hd * MLA_V, MLA_V)
                ot_scr[t, pl.ds(row, MLA_V), :] = acc[0:MLA_V] / acc[MLA_V:MLA_V + 1]
            if idx + MLA_QK_LEAD < n_q:
                m8_next = scores_into(1 - slot, idx + MLA_QK_LEAD, nxt, m8_next)
        return m8_next

    def stage_of(n):
        return n // groups, lax.rem(n, groups)

    def stage_step(n, m8):
        return lax.cond(lax.rem(n, 2) == 0,
                        lambda m: stage_body(stage_of(n), m, 0, stage_of(n + 1)),
                        lambda m: stage_body(stage_of(n), m, 1, stage_of(n + 1)), m8)

    n_stage = groups * MLA_Q_TILES
    m8_first = (None,) * MLA_STAGE_HEADS
    for idx in range(len(subs)):
        m8_first = scores_into(0, idx, (0, 0), m8_first)
    m8_last = lax.fori_loop(0, n_stage - 1, stage_step, m8_first)
    stage_body((MLA_Q_TILES - 1, groups - 1), m8_last, (n_stage - 1) % 2, None)
    for t in range(MLA_Q_TILES):
        o_ref[0, t * tq:(t + 1) * tq, :] = ot_scr[t].T.astype(BF16)


def _mla_attention(qmt, km_c, km_l, vmt_c, vmt_l):
    bsz = qmt.shape[0]
    s = km_l.shape[2]
    n_ctx = km_c.shape[2]
    step_q = MLA_Q_TILES * TOK_TILE
    assert s % step_q == 0
    n_keys = n_ctx + s
    return pl.pallas_call(
        functools.partial(_mla_kernel, seq=s),
        grid=(bsz, s // step_q),
        in_specs=[
            pl.BlockSpec((1, MLA_HEADS, MLA_Q_TILES, LANE, TOK_TILE), lambda b, j: (b, 0, j, 0, 0)),
            pl.BlockSpec((1, MLA_HEADS, n_ctx, LANE), lambda b, j: (b, 0, 0, 0)),
            pl.BlockSpec((1, MLA_HEADS, s, LANE), lambda b, j: (b, 0, 0, 0)),
            pl.BlockSpec((1, MLA_HEADS, MLA_V_AUG, n_ctx), lambda b, j: (b, 0, 0, 0)),
            pl.BlockSpec((1, MLA_HEADS, MLA_V_AUG, s), lambda b, j: (b, 0, 0, 0)),
        ],
        out_specs=pl.BlockSpec((1, step_q, MLA_HEADS * MLA_V), lambda b, j: (b, j, 0)),
        out_shape=jax.ShapeDtypeStruct((bsz, s, MLA_HEADS * MLA_V), BF16),
        scratch_shapes=[
            pltpu.VMEM((MLA_Q_TILES, MLA_HEADS * MLA_V, TOK_TILE), F32),
            pltpu.VMEM((2, MLA_STAGE_HEADS * n_keys, TOK_TILE), F32),
        ],
        compiler_params=pltpu.CompilerParams(vmem_limit_bytes=VMEM_LIMIT,
                                             dimension_semantics=("arbitrary", "arbitrary")),
        name="mla_attention",
    )(qmt, km_c, km_l, vmt_c, vmt_l)


def _swa_kernel(sink_ref, bias_ref, qt_ref, qtn_ref, kc_ref, kl_ref, vct_ref, vlt_ref, o_ref,
                ot_scr, s_scr, m8_scr, *, seq):
    j = pl.program_id(1)
    nt = seq // TOK_TILE
    n_ctx = kc_ref.shape[1]
    tq = qt_ref.shape[3]
    n_band = SWA_WIN // SWA_SUB
    n_t = n_band + n_ctx // SWA_SUB
    lanes_per_sub = SWA_SUB // LANE
    subs = [(hd, t) for hd in range(SWA_HEADS) for t in range(n_t)]

    def window(jj):
        start = pl.multiple_of(jnp.clip(jj * TOK_TILE - WINDOW, 0, seq - SWA_WIN), LANE)
        variant = jnp.where(jj == 0, 0, jnp.where(jj == nt - 1, 2, 1))
        return start, variant

    def rows_of(idx):
        return slice(idx * SWA_SUB, (idx + 1) * SWA_SUB)

    def scores_into(slot, idx, q_ref, win):
        hd, t = subs[idx]
        start, variant = win
        if t < n_band:
            k = kl_ref[0, pl.ds(start + t * SWA_SUB, SWA_SUB), :]
        else:
            k = kc_ref[0, (t - n_band) * SWA_SUB:(t - n_band + 1) * SWA_SUB, :]
        sj = _dot(k, q_ref[0, hd])
        if t < n_band:
            sj = sj + bias_ref[variant, t * SWA_SUB:(t + 1) * SWA_SUB, :]
        s_scr[slot, rows_of(idx), :] = sj
        return jnp.max(sj.reshape(SWA_SUB // SUBLANE, SUBLANE, tq), axis=0)

    def values_t(g, t, t0):
        if t < n_band:
            tiles = [vlt_ref[0, g, t0 + t * lanes_per_sub + i] for i in range(lanes_per_sub)]
        else:
            tiles = [vct_ref[0, g, (t - n_band) * lanes_per_sub + i] for i in range(lanes_per_sub)]
        return jnp.concatenate(tiles, axis=1)

    def next_scores(slot, idx, nxt, m8n):
        q_ref, win_n = nxt
        mj = scores_into(slot, idx, q_ref, win_n)
        hd, t = subs[idx]
        m8n[hd] = mj if t == 0 else jnp.maximum(m8n[hd], mj)
        if t == n_t - 1:
            m8_scr[slot, hd] = m8n[hd]

    def stage_body(slot, win_cur, nxt):
        t0 = win_cur[0] // LANE
        n_q = len(subs) if nxt is not None else 0
        m8n = {}
        for idx in range(min(SWA_QK_LEAD, n_q)):
            next_scores(1 - slot, idx, nxt, m8n)
        for idx, (hd, t) in enumerate(subs):
            g = hd // SWA_GROUP
            if t == 0:
                sk = sink_ref[hd] * LOG2E
                m = jnp.maximum(jnp.max(m8_scr[slot, hd], axis=0, keepdims=True), sk)
                mb = jnp.broadcast_to(m, (SWA_CHUNK, tq))
                acc = None
            base = idx * SWA_SUB
            p = jnp.concatenate(
                [jnp.exp2(s_scr[slot, r:r + SWA_CHUNK, :] - mb).astype(BF16)
                 for r in range(base, base + SWA_SUB, SWA_CHUNK)], axis=0)
            d = _dot(values_t(g, t, t0), p)
            acc = d if acc is None else acc + d
            if t == n_t - 1:
                denom = acc[SWA_HD:SWA_HD + 1] + jnp.exp2(sk - m)
                ot_scr[hd * SWA_HD:(hd + 1) * SWA_HD, :] = acc[0:SWA_HD] / denom
            if idx + SWA_QK_LEAD < n_q:
                next_scores(1 - slot, idx + SWA_QK_LEAD, nxt, m8n)

    win = window(j)

    @pl.when(j == 0)
    def _():
        m8n = {}
        for idx in range(len(subs)):
            next_scores(0, idx, (qt_ref, win), m8n)

    for parity in range(2):
        @pl.when((lax.rem(j, 2) == parity) & (j < nt - 1))
        def _():
            stage_body(parity, win, (qtn_ref, window(j + 1)))

    @pl.when(j == nt - 1)
    def _():
        stage_body((nt - 1) % 2, win, None)

    o_ref[0] = ot_scr[...].T.astype(BF16)


def _swa_bias(seq):
    nt = seq // TOK_TILE
    out = []
    for j in (0, 1, nt - 1):
        start = min(max(j * TOK_TILE - WINDOW, 0), seq - SWA_WIN)
        qpos = j * TOK_TILE + np.arange(TOK_TILE)[None, :]
        kpos = start + np.arange(SWA_WIN)[:, None]
        out.append(np.where(np.abs(qpos - kpos) <= WINDOW, 0.0, NEG_INF))
    return jnp.asarray(np.stack(out), F32)


def _swa_attention(sink, qst, ks_c, ks_l, vst_c, vst_l):
    bsz, _, _, s = qst.shape
    n_ctx = ks_c.shape[1]
    nt = s // TOK_TILE
    assert nt >= 3 and s >= SWA_WIN and n_ctx % SWA_SUB == 0
    stage_rows = SWA_HEADS * (SWA_WIN + n_ctx)
    return pl.pallas_call(
        functools.partial(_swa_kernel, seq=s),
        grid=(bsz, nt),
        in_specs=[
            pl.BlockSpec(memory_space=pltpu.SMEM),
            pl.BlockSpec((3, SWA_WIN, TOK_TILE), lambda b, j: (0, 0, 0)),
            pl.BlockSpec((1, SWA_HEADS, LANE, TOK_TILE), lambda b, j: (b, 0, 0, j)),
            pl.BlockSpec((1, SWA_HEADS, LANE, TOK_TILE), lambda b, j: (b, 0, 0, jnp.minimum(j + 1, nt - 1))),
            pl.BlockSpec((1, n_ctx, LANE), lambda b, j: (b, 0, 0)),
            pl.BlockSpec((1, s, LANE), lambda b, j: (b, 0, 0)),
            pl.BlockSpec((1, SWA_KV_HEADS, n_ctx // LANE, SWA_V_AUG, LANE), lambda b, j: (b, 0, 0, 0, 0)),
            pl.BlockSpec((1, SWA_KV_HEADS, s // LANE, SWA_V_AUG, LANE), lambda b, j: (b, 0, 0, 0, 0)),
        ],
        out_specs=pl.BlockSpec((1, TOK_TILE, SWA_HEADS * SWA_HD), lambda b, j: (b, j, 0)),
        out_shape=jax.ShapeDtypeStruct((bsz, s, SWA_HEADS * SWA_HD), BF16),
        scratch_shapes=[
            pltpu.VMEM((SWA_HEADS * SWA_HD, TOK_TILE), F32),
            pltpu.VMEM((2, stage_rows, TOK_TILE), F32),
            pltpu.VMEM((2, SWA_HEADS, SUBLANE, TOK_TILE), F32),
        ],
        compiler_params=pltpu.CompilerParams(vmem_limit_bytes=VMEM_LIMIT,
                                             dimension_semantics=("arbitrary", "arbitrary")),
        name="swa_attention",
    )(sink, _swa_bias(s), qst, qst, ks_c, ks_l, vst_c, vst_l)


def _mlp_kernel(x_ref, mm_ref, ms_ref, mod_ref, wo_ref, gmlp_ref, w1_ref, w2_ref, o_ref, acc_ref):
    half = MLA_HEADS * MLA_V
    y1 = _dot(mm_ref[0], wo_ref[0:half, :]) + _dot(ms_ref[0], wo_ref[half:, :])
    x1 = x_ref[0] + mod_ref[0, 2:3, :] * y1
    h2 = (_rms_rows(x1, D_MODEL) * gmlp_ref[...] * (1.0 + mod_ref[0, 4:5, :])
          + mod_ref[0, 3:4, :]).astype(BF16)
    for c in range(D_FF // FF_CHUNK):
        a = jnp.maximum(_dot(h2, w1_ref[:, c * FF_CHUNK:(c + 1) * FF_CHUNK]), 0.0)
        part = _dot((a * a).astype(BF16), w2_ref[c * FF_CHUNK:(c + 1) * FF_CHUNK, :])
        if c == 0:
            acc_ref[...] = part
        else:
            acc_ref[...] += part
    o_ref[0] = x1 + mod_ref[0, 5:6, :] * acc_ref[...]


def _outproj_mlp(x, mix_m, mix_s, mod3, w_out, g_mlp, w1, w2):
    bsz, s, _ = x.shape
    half = MLA_HEADS * MLA_V
    const = lambda b, j: (0, 0)
    single = pl.Buffered(1)
    return pl.pallas_call(
        _mlp_kernel,
        grid=(bsz, s // MLP_TILE),
        in_specs=[
            pl.BlockSpec((1, MLP_TILE, D_MODEL), lambda b, j: (b, j, 0)),
            pl.BlockSpec((1, MLP_TILE, half), lambda b, j: (b, j, 0)),
            pl.BlockSpec((1, MLP_TILE, half), lambda b, j: (b, j, 0)),
            pl.BlockSpec((1, 6, D_MODEL), lambda b, j: (b, 0, 0)),
            pl.BlockSpec((D_MODEL, D_MODEL), const, pipeline_mode=single),
            pl.BlockSpec((1, D_MODEL), const),
            pl.BlockSpec((D_MODEL, D_FF), const, pipeline_mode=single),
            pl.BlockSpec((D_FF, D_MODEL), const, pipeline_mode=single),
        ],
        out_specs=pl.BlockSpec((1, MLP_TILE, D_MODEL), lambda b, j: (b, j, 0)),
        out_shape=jax.ShapeDtypeStruct((bsz, s, D_MODEL), F32),
        scratch_shapes=[pltpu.VMEM((MLP_TILE, D_MODEL), F32)],
        compiler_params=pltpu.CompilerParams(vmem_limit_bytes=VMEM_LIMIT),
        name="outproj_mlp",
    )(x, mix_m, mix_s, mod3, w_out, g_mlp, w1, w2)


def _rope_tables(seq):
    f32 = np.float32

    def cos_sin(rot_dim):
        n_freq = rot_dim // 4
        inv = f32(ROPE_THETA) ** (-np.arange(n_freq, dtype=f32) / f32(n_freq))
        rows = seq // GRID_W
        row = np.repeat(np.arange(rows, dtype=f32), GRID_W)
        col = np.tile(np.arange(GRID_W, dtype=f32), rows)
        ang = np.concatenate([row[:, None] * inv, col[:, None] * inv], axis=-1).astype(f32)
        return np.cos(ang).astype(f32), np.sin(ang).astype(f32)

    mcos, msin = cos_sin(MLA_ROPE)
    scos, ssin = cos_sin(SWA_HD)
    z16 = np.zeros((seq, 16), f32)
    z32 = np.zeros((seq, 32), f32)
    z64 = np.zeros((seq, 64), f32)
    kc = np.concatenate([np.ones((seq, 64), f32), mcos, mcos, np.ones((seq, 32), f32)], axis=1)
    ks1 = np.concatenate([z64, -msin, z16, z32], axis=1)
    ks2 = np.concatenate([z64, z16, msin, z32], axis=1)
    sc = np.concatenate([scos, scos, scos, scos], axis=1)
    ss1 = np.concatenate([-ssin, z32, -ssin, z32], axis=1)
    ss2 = np.concatenate([z32, ssin, z32, ssin], axis=1)
    tabs = dict(mcos=mcos.T, msin=msin.T, scos=scos.T, ssin=ssin.T,
                kc=kc, ks1=ks1, ks2=ks2, sc=sc, ss1=ss1, ss2=ss2)
    return {k: jnp.asarray(np.ascontiguousarray(v)) for k, v in tabs.items()}


def _prep_weights(g_attn, w_in, g_q_a, w_uq, g_kv_a, w_ukv, g_mla_q, g_mla_k, g_swa_q, g_swa_k):
    o_ckv = Q_LORA
    o_kr = o_ckv + KV_LORA
    o_qs = o_kr + MLA_ROPE
    o_ks = o_qs + SWA_HEADS * SWA_HD
    o_vs = o_ks + SWA_KV_HEADS * SWA_HD
    w_cq = w_in[:, :o_ckv]
    w_ckv = w_in[:, o_ckv:o_kr]
    w_kr = w_in[:, o_kr:o_qs]
    w_qs = w_in[:, o_qs:o_ks]
    w_ks = w_in[:, o_ks:o_vs]
    w_vs = w_in[:, o_vs:]
    w_kr_p = jnp.pad(w_kr, ((0, 0), (MLA_NOPE, LANE - MLA_QK)))
    wtok = jnp.concatenate([w_cq, w_ckv, w_kr_p, w_ks], axis=1).astype(BF16)
    w_ukv_h = w_ukv.reshape(KV_LORA, MLA_HEADS, MLA_NOPE + MLA_V)
    wuk = jnp.pad(w_ukv_h[:, :, :MLA_NOPE], ((0, 0), (0, 0), (0, LANE - MLA_NOPE)))
    wuk = wuk.reshape(KV_LORA, MLA_HEADS * LANE).astype(BF16)
    wuv = w_ukv_h[:, :, MLA_NOPE:].reshape(KV_LORA, MLA_HEADS * MLA_V).T.astype(BF16)
    w_uq_h = jnp.pad(w_uq.reshape(Q_LORA, MLA_HEADS, MLA_QK), ((0, 0), (0, 0), (0, LANE - MLA_QK)))
    wuq = w_uq_h.reshape(Q_LORA, MLA_HEADS * LANE).T.astype(BF16)
    return dict(
        gattn=g_attn[None, :], wtok=wtok, wvs=w_vs.T.astype(BF16), gkva=g_kv_a[None, :],
        wuk=wuk, wuv=wuv, gmk=jnp.pad(g_mla_k, (0, LANE - MLA_QK))[None, :],
        gsk=jnp.tile(g_swa_k, SWA_KV_HEADS)[None, :],
        wqs=w_qs.T.astype(BF16), gqa=g_q_a[None, :], wuq=wuq,
        gmq=jnp.pad(g_mla_q, (0, LANE - MLA_QK))[:, None], gsq=g_swa_q[:, None],
    )


def kernel(x, c, ctx, c_ctx, w_mod, b_mod, g_attn, w_in, g_q_a, w_uq, g_kv_a, w_ukv, g_mla_q, g_mla_k,
           g_swa_q, g_swa_k, swa_sink, w_out, g_mlp, w_mlp1, w_mlp2):
    bsz, seq, _ = x.shape
    assert w_mod.shape[0] == 1, "single-layer block"
    assert bsz < SUBLANE and seq % PROJ_TILE == 0 and seq % MLP_TILE == 0 and ctx.shape[1] % TOK_TILE == 0

    cond = jnp.zeros((SUBLANE, D_MODEL), F32).at[:bsz].set(c).at[bsz].set(c_ctx)
    mod = _modulation(cond, w_mod[0], b_mod[0][None, :])
    mod3 = mod.reshape(SUBLANE, 6, D_MODEL)

    weights = _prep_weights(g_attn[0], w_in[0], g_q_a[0], w_uq[0], g_kv_a[0], w_ukv[0],
                            g_mla_q[0], g_mla_k[0], g_swa_q[0], g_swa_k[0])
    tables = _rope_tables(seq)

    weights["to_cast"] = [w_out[0], w_mlp1[0], w_mlp2[0]]
    (km_l, vmt_l, ks_l, vst_l, qmt, qst,
     w_out_bf16, w_mlp1_bf16, w_mlp2_bf16) = _project(x, mod3, None, weights, tables, latent=True, tile=PROJ_TILE)
    km_c, vmt_c, ks_c, vst_c = _project(ctx, mod3, bsz, weights, None, latent=False, tile=TOK_TILE)

    mix_m = _mla_attention(qmt, km_c, km_l, vmt_c, vmt_l)
    mix_s = _swa_attention(swa_sink[0], qst, ks_c, ks_l, vst_c, vst_l)

    return _outproj_mlp(x, mix_m, mix_s, mod3, w_out_bf16, g_mlp[0][None, :], w_mlp1_bf16, w_mlp2_bf16)
```

```python
import functools
import math

import jax
import jax.numpy as jnp
import numpy as np
from jax import lax
from jax.experimental import pallas as pl
from jax.experimental.pallas import tpu as pltpu

D_MODEL = 1024
GRID_W = 64
MLA_HEADS = 8
MLA_NOPE = 64
MLA_ROPE = 32
MLA_QK = MLA_NOPE + MLA_ROPE
MLA_V = 64
MLA_V_AUG = MLA_V + 16
Q_LORA = 256
KV_LORA = 128
SWA_HEADS = 8
SWA_KV_HEADS = 2
SWA_GROUP = SWA_HEADS // SWA_KV_HEADS
SWA_HD = 64
SWA_V_AUG = SWA_HD + 16
WINDOW = 128
D_FF = 4 * D_MODEL
ROPE_THETA = 10000.0
EPS = 1e-6
NEG_INF = -1e30
LOG2E = 1.4426950408889634
MLA_QSCALE = LOG2E / math.sqrt(MLA_QK)
SWA_QSCALE = LOG2E / math.sqrt(SWA_HD)

LANE = 128
SUBLANE = 8
MXU_DIM = 256
TOK_TILE = MXU_DIM
PROJ_TILE = 1024
PROJ_PART = 256
MOD_TILE = 1024
MLP_TILE = 512
FF_CHUNK = 1024
SWA_WIN = 2 * TOK_TILE
MLA_SUB = MXU_DIM
MLA_Q_TILES = 4
MLA_STAGE_HEADS = 2
SWA_SUB = MXU_DIM
SWA_STAGE_TILES = 2
MLA_CHUNK = 32
SWA_CHUNK = 32
MLA_QK_LEAD = 3
SWA_QK_LEAD = 1
VMEM_LIMIT = 56 * 1024 * 1024

F32 = jnp.float32
BF16 = jnp.bfloat16
NT_DIMS = (((1,), (1,)), ((), ()))


def _dot(a, b):
    return jnp.dot(a, b, preferred_element_type=F32)


def _dot_nt(a, b):
    return lax.dot_general(a, b, NT_DIMS, preferred_element_type=F32)


def _mod_kernel(cond_ref, w_ref, b_ref, o_ref):
    cnd = cond_ref[...]
    act = cnd * jax.nn.sigmoid(cnd)
    o_ref[...] = _dot(act.astype(BF16), w_ref[...].astype(BF16)) + b_ref[...]


def _modulation(cond, w_mod, b_mod):
    n = w_mod.shape[1]
    tn = MOD_TILE
    return pl.pallas_call(
        _mod_kernel,
        grid=(n // tn,),
        in_specs=[
            pl.BlockSpec((SUBLANE, D_MODEL), lambda i: (0, 0)),
            pl.BlockSpec((D_MODEL, tn), lambda i: (0, i)),
            pl.BlockSpec((1, tn), lambda i: (0, i)),
        ],
        out_specs=pl.BlockSpec((SUBLANE, tn), lambda i: (0, i)),
        out_shape=jax.ShapeDtypeStruct((SUBLANE, n), F32),
        name="modulation",
    )(cond, w_mod, b_mod)


def _rms_rows(x, n):
    return x * lax.rsqrt(jnp.sum(x * x, axis=-1, keepdims=True) * (1.0 / n) + EPS)


def _rms_cols(x, n):
    return x * lax.rsqrt(jnp.sum(x * x, axis=0, keepdims=True) * (1.0 / n) + EPS)


def _proj_kernel(*refs, latent):
    if latent:
        (x_ref, mod_ref, gattn_ref, wtok_ref, wvs_ref, gkva_ref, wuk_ref, wuv_ref, gmk_ref, gsk_ref,
         wqs_ref, gqa_ref, wuq_ref, gmq_ref, gsq_ref,
         mcos_ref, msin_ref, scos_ref, ssin_ref, kc_ref, ks1_ref, ks2_ref, sc_ref, ss1_ref, ss2_ref,
         wo_f32_ref, w1_f32_ref, w2_f32_ref,
         km_ref, vmt_ref, ks_ref, vst_ref, qmt_ref, qst_ref,
         wo_bf16_ref, w1_bf16_ref, w2_bf16_ref) = refs
        for src, dst in ((wo_f32_ref, wo_bf16_ref), (w1_f32_ref, w1_bf16_ref), (w2_f32_ref, w2_bf16_ref)):
            dst[...] = src[...].astype(BF16)
    else:
        (x_ref, mod_ref, gattn_ref, wtok_ref, wvs_ref, gkva_ref, wuk_ref, wuv_ref, gmk_ref, gsk_ref,
         km_ref, vmt_ref, ks_ref, vst_ref) = refs

    tile = x_ref.shape[1]
    n_part = max(1, tile // PROJ_PART)
    part = tile // n_part

    gain = gattn_ref[...] * (1.0 + mod_ref[0, 1:2, :])
    shift = mod_ref[0, 0:1, :]
    gmk = gmk_ref[...]
    ones_row = (lax.broadcasted_iota(jnp.int32, (MLA_V_AUG - MLA_V, part), 0) == 0).astype(BF16)
    if latent:
        gmq = gmq_ref[...] * MLA_QSCALE
        gsq = gsq_ref[...] * SWA_QSCALE

    def part_phases(lo):
        rows = slice(lo, lo + part)

        h = (_rms_rows(x_ref[0, rows, :], D_MODEL) * gain + shift).astype(BF16)
        yield

        zt = _dot(h, wtok_ref[...])
        vst = _dot_nt(wvs_ref[...], h)
        if latent:
            qst = _dot_nt(wqs_ref[...], h)
        yield

        cq = zt[:, 0:256]
        ckv = zt[:, 256:384]
        krp = zt[:, 384:512]
        ksw = zt[:, 512:640]
        ckvn = (_rms_rows(ckv, KV_LORA) * gkva_ref[...]).astype(BF16)
        if latent:
            cqn = (_rms_rows(cq, Q_LORA) * gqa_ref[...]).astype(BF16)
        kr = krp * gmk
        if latent:
            kr = (kr * kc_ref[rows, :] + pltpu.roll(kr, 112, 1) * ks1_ref[rows, :]
                  + pltpu.roll(kr, 16, 1) * ks2_ref[rows, :])
        ss_kr = jnp.sum(krp * krp, axis=-1, keepdims=True)
        first = lax.broadcasted_iota(jnp.int32, ksw.shape, 1) < SWA_HD
        sq = ksw * ksw
        ss0 = jnp.sum(jnp.where(first, sq, 0.0), axis=-1, keepdims=True)
        ss1 = jnp.sum(jnp.where(first, 0.0, sq), axis=-1, keepdims=True)
        rr = jnp.where(first, lax.rsqrt(ss0 * (1.0 / SWA_HD) + EPS), lax.rsqrt(ss1 * (1.0 / SWA_HD) + EPS))
        ksn = ksw * rr * gsk_ref[...]
        if latent:
            ksn = (ksn * sc_ref[rows, :] + pltpu.roll(ksn, 96, 1) * ss1_ref[rows, :]
                   + pltpu.roll(ksn, 32, 1) * ss2_ref[rows, :])
        ks_ref[0, rows, :] = ksn.astype(BF16)
        for g in range(SWA_KV_HEADS):
            for t in range(part // LANE):
                dst = lo // LANE + t
                vst_ref[0, g, dst, 0:SWA_HD, :] = vst[g * SWA_HD:(g + 1) * SWA_HD, t * LANE:(t + 1) * LANE].astype(BF16)
                vst_ref[0, g, dst, SWA_HD:SWA_V_AUG, :] = ones_row[:, 0:LANE]
        yield

        kpre = _dot(ckvn, wuk_ref[...])
        vt = _dot_nt(wuv_ref[...], ckvn)
        if latent:
            qt = _dot_nt(wuq_ref[...], cqn)
        yield

        for hd in range(MLA_HEADS):
            kp = kpre[:, hd * LANE:(hd + 1) * LANE]
            ss = jnp.sum(kp * kp, axis=-1, keepdims=True) + ss_kr
            kn = (kp * gmk + kr) * lax.rsqrt(ss * (1.0 / MLA_QK) + EPS)
            km_ref[0, hd, rows, :] = kn.astype(BF16)
            vmt_ref[0, hd, 0:MLA_V, rows] = vt[hd * MLA_V:(hd + 1) * MLA_V].astype(BF16)
            vmt_ref[0, hd, MLA_V:MLA_V_AUG, rows] = ones_row
        if latent:
            mcos = mcos_ref[:, rows]
            msin = msin_ref[:, rows]
            for hd in range(MLA_HEADS):
                qn = _rms_cols(qt[hd * LANE:(hd + 1) * LANE], MLA_QK) * gmq
                x1 = qn[64:80]
                x2 = qn[80:96]
                qt_idx = lo // TOK_TILE
                qmt_ref[0, hd, qt_idx, 0:64, :] = qn[0:64].astype(BF16)
                qmt_ref[0, hd, qt_idx, 64:80, :] = (x1 * mcos - x2 * msin).astype(BF16)
                qmt_ref[0, hd, qt_idx, 80:96, :] = (x2 * mcos + x1 * msin).astype(BF16)
                qmt_ref[0, hd, qt_idx, 96:128, :] = jnp.zeros((32, part), BF16)
            scos = scos_ref[:, rows]
            ssin = ssin_ref[:, rows]
            for hd in range(SWA_HEADS):
                qn = _rms_cols(qst[hd * SWA_HD:(hd + 1) * SWA_HD], SWA_HD) * gsq
                x1 = qn[0:32]
                x2 = qn[32:64]
                g = hd // SWA_GROUP
                base = g * SWA_HD
                other = (1 - g) * SWA_HD
                qst_ref[0, hd, base:base + 32, rows] = (x1 * scos - x2 * ssin).astype(BF16)
                qst_ref[0, hd, base + 32:base + 64, rows] = (x2 * scos + x1 * ssin).astype(BF16)
                qst_ref[0, hd, other:other + 64, rows] = jnp.zeros((64, part), BF16)
        yield

    n_phase = 5
    parts = [part_phases(i * part) for i in range(n_part)]
    for step in range(n_phase + n_part - 1):
        for i, gen in enumerate(parts):
            if 0 <= step - i < n_phase:
                next(gen)


def _full(shape):
    nd = len(shape)
    return pl.BlockSpec(shape, lambda b, j: (0,) * nd)


def _project(x, mod3, mod_row, weights, tables, latent, tile):
    bsz, n, _ = x.shape
    nt = n // tile
    if mod_row is None:
        mod_map = lambda b, j: (b, 0, 0)
    else:
        mod_map = lambda b, j: (mod_row, 0, 0)
    common = [weights[k] for k in ("gattn", "wtok", "wvs", "gkva", "wuk", "wuv", "gmk", "gsk")]
    ins = [x, mod3] + common
    in_specs = [
        pl.BlockSpec((1, tile, D_MODEL), lambda b, j: (b, j, 0)),
        pl.BlockSpec((1, 6, D_MODEL), mod_map),
    ] + [_full(w.shape) for w in common]
    out_shape = [
        jax.ShapeDtypeStruct((bsz, MLA_HEADS, n, LANE), BF16),
        jax.ShapeDtypeStruct((bsz, MLA_HEADS, MLA_V_AUG, n), BF16),
        jax.ShapeDtypeStruct((bsz, n, LANE), BF16),
        jax.ShapeDtypeStruct((bsz, SWA_KV_HEADS, n // LANE, SWA_V_AUG, LANE), BF16),
    ]
    out_specs = [
        pl.BlockSpec((1, MLA_HEADS, tile, LANE), lambda b, j: (b, 0, j, 0)),
        pl.BlockSpec((1, MLA_HEADS, MLA_V_AUG, tile), lambda b, j: (b, 0, 0, j)),
        pl.BlockSpec((1, tile, LANE), lambda b, j: (b, j, 0)),
        pl.BlockSpec((1, SWA_KV_HEADS, tile // LANE, SWA_V_AUG, LANE), lambda b, j: (b, 0, j, 0, 0)),
    ]
    if latent:
        extra = [weights[k] for k in ("wqs", "gqa", "wuq", "gmq", "gsq")]
        ins += extra
        in_specs += [_full(w.shape) for w in extra]
        feat = [tables[k] for k in ("mcos", "msin", "scos", "ssin")]
        ins += feat
        in_specs += [pl.BlockSpec((t.shape[0], tile), lambda b, j: (0, j)) for t in feat]
        tok = [tables[k] for k in ("kc", "ks1", "ks2", "sc", "ss1", "ss2")]
        ins += tok
        in_specs += [pl.BlockSpec((tile, LANE), lambda b, j: (j, 0)) for _ in tok]
        assert PROJ_PART == TOK_TILE
        out_shape += [
            jax.ShapeDtypeStruct((bsz, MLA_HEADS, n // TOK_TILE, LANE, TOK_TILE), BF16),
            jax.ShapeDtypeStruct((bsz, SWA_HEADS, LANE, n), BF16),
        ]
        out_specs += [
            pl.BlockSpec((1, MLA_HEADS, tile // TOK_TILE, LANE, TOK_TILE), lambda b, j: (b, 0, j, 0, 0)),
            pl.BlockSpec((1, SWA_HEADS, LANE, tile), lambda b, j: (b, 0, 0, j)),
        ]
        steps = bsz * nt
        for w in weights["to_cast"]:
            rows = w.shape[0] // steps
            assert rows * steps == w.shape[0] and rows % 16 == 0
            ins.append(w)
            in_specs.append(pl.BlockSpec((rows, w.shape[1]), lambda b, j: (b * nt + j, 0)))
            out_shape.append(jax.ShapeDtypeStruct(w.shape, BF16))
            out_specs.append(pl.BlockSpec((rows, w.shape[1]), lambda b, j: (b * nt + j, 0)))
    return pl.pallas_call(
        functools.partial(_proj_kernel, latent=latent),
        grid=(bsz, nt),
        in_specs=in_specs,
        out_specs=out_specs,
        out_shape=out_shape,
        compiler_params=pltpu.CompilerParams(vmem_limit_bytes=VMEM_LIMIT),
        name="project_latent" if latent else "project_context",
    )(*ins)


def _mla_kernel(qt_ref, kc_ref, kl_ref, vct_ref, vlt_ref, o_ref, ot_scr, s_scr, *, seq):
    n_ctx = kc_ref.shape[2]
    tq = TOK_TILE
    key_subs = ([(kc_ref, vct_ref, o) for o in range(0, n_ctx, MLA_SUB)]
                + [(kl_ref, vlt_ref, o) for o in range(0, seq, MLA_SUB)])
    n_key = len(key_subs)
    subs = [(h, j) for h in range(MLA_STAGE_HEADS) for j in range(n_key)]
    groups = MLA_HEADS // MLA_STAGE_HEADS

    def scores_into(slot, idx, stage, m8):
        t, grp = stage
        h, j = subs[idx]
        hd = grp * MLA_STAGE_HEADS + h
        k_ref, _, off = key_subs[j]
        sj = _dot(k_ref[0, hd, off:off + MLA_SUB, :], qt_ref[0, hd, t])
        s_scr[slot, idx * MLA_SUB:(idx + 1) * MLA_SUB, :] = sj
        mj = jnp.max(sj.reshape(MLA_SUB // SUBLANE, SUBLANE, tq), axis=0)
        m8 = list(m8)
        m8[h] = mj if j == 0 else jnp.maximum(m8[h], mj)
        return tuple(m8)

    def stage_body(stage, m8_cur, slot, nxt):
        t, grp = stage
        n_q = len(subs) if nxt is not None else 0
        m8_next = (None,) * MLA_STAGE_HEADS
        for idx in range(min(MLA_QK_LEAD, n_q)):
            m8_next = scores_into(1 - slot, idx, nxt, m8_next)
        for idx, (h, j) in enumerate(subs):
            hd = grp * MLA_STAGE_HEADS + h
            if j == 0:
                mb = jnp.broadcast_to(jnp.max(m8_cur[h], axis=0, keepdims=True), (MLA_CHUNK, tq))
                acc = None
            base = idx * MLA_SUB
            p = jnp.concatenate(
                [jnp.exp2(s_scr[slot, r:r + MLA_CHUNK, :] - mb).astype(BF16)
                 for r in range(base, base + MLA_SUB, MLA_CHUNK)], axis=0)
            _, v_ref, off = key_subs[j]
            d = _dot(v_ref[0, hd, :, off:off + MLA_SUB], p)
            acc = d if acc is None else acc + d
            if j == n_key - 1:
                row = pl.multiple_of(hd * MLA_V, MLA_V)
                ot_scr[t, pl.ds(row, MLA_V), :] = acc[0:MLA_V] / acc[MLA_V:MLA_V + 1]
            if idx + MLA_QK_LEAD < n_q:
                m8_next = scores_into(1 - slot, idx + MLA_QK_LEAD, nxt, m8_next)
        return m8_next

    def stage_of(n):
        return n // groups, lax.rem(n, groups)

    def stage_step(n, m8):
        return lax.cond(lax.rem(n, 2) == 0,
                        lambda m: stage_body(stage_of(n), m, 0, stage_of(n + 1)),
                        lambda m: stage_body(stage_of(n), m, 1, stage_of(n + 1)), m8)

    n_stage = groups * MLA_Q_TILES
    m8_first = (None,) * MLA_STAGE_HEADS
    for idx in range(len(subs)):
        m8_first = scores_into(0, idx, (0, 0), m8_first)
    m8_last = lax.fori_loop(0, n_stage - 1, stage_step, m8_first)
    stage_body((MLA_Q_TILES - 1, groups - 1), m8_last, (n_stage - 1) % 2, None)
    for t in range(MLA_Q_TILES):
        o_ref[0, t * tq:(t + 1) * tq, :] = ot_scr[t].T.astype(BF16)


def _mla_attention(qmt, km_c, km_l, vmt_c, vmt_l):
    bsz = qmt.shape[0]
    s = km_l.shape[2]
    n_ctx = km_c.shape[2]
    step_q = MLA_Q_TILES * TOK_TILE
    assert s % step_q == 0
    n_keys = n_ctx + s
    return pl.pallas_call(
        functools.partial(_mla_kernel, seq=s),
        grid=(bsz, s // step_q),
        in_specs=[
            pl.BlockSpec((1, MLA_HEADS, MLA_Q_TILES, LANE, TOK_TILE), lambda b, j: (b, 0, j, 0, 0)),
            pl.BlockSpec((1, MLA_HEADS, n_ctx, LANE), lambda b, j: (b, 0, 0, 0)),
            pl.BlockSpec((1, MLA_HEADS, s, LANE), lambda b, j: (b, 0, 0, 0)),
            pl.BlockSpec((1, MLA_HEADS, MLA_V_AUG, n_ctx), lambda b, j: (b, 0, 0, 0)),
            pl.BlockSpec((1, MLA_HEADS, MLA_V_AUG, s), lambda b, j: (b, 0, 0, 0)),
        ],
        out_specs=pl.BlockSpec((1, step_q, MLA_HEADS * MLA_V), lambda b, j: (b, j, 0)),
        out_shape=jax.ShapeDtypeStruct((bsz, s, MLA_HEADS * MLA_V), BF16),
        scratch_shapes=[
            pltpu.VMEM((MLA_Q_TILES, MLA_HEADS * MLA_V, TOK_TILE), F32),
            pltpu.VMEM((2, MLA_STAGE_HEADS * n_keys, TOK_TILE), F32),
        ],
        compiler_params=pltpu.CompilerParams(vmem_limit_bytes=VMEM_LIMIT,
                                             dimension_semantics=("arbitrary", "arbitrary")),
        name="mla_attention",
    )(qmt, km_c, km_l, vmt_c, vmt_l)


def _swa_kernel(sink_ref, bias_ref, qt_ref, qtn_ref, kc_ref, kl_ref, vct_ref, vlt_ref, o_ref,
                ot_scr, s_scr, m8_scr, *, seq):
    j = pl.program_id(1)
    nt = seq // TOK_TILE
    n_stage = nt // SWA_STAGE_TILES
    n_ctx = kc_ref.shape[1]
    tq = TOK_TILE
    n_band = SWA_WIN // SWA_SUB
    n_t = n_band + n_ctx // SWA_SUB
    lanes_per_sub = SWA_SUB // LANE
    subs = [(ti, hd, t) for ti in range(SWA_STAGE_TILES) for hd in range(SWA_HEADS) for t in range(n_t)]

    def window(jj):
        start = pl.multiple_of(jnp.clip(jj * TOK_TILE - WINDOW, 0, seq - SWA_WIN), LANE)
        variant = jnp.where(jj == 0, 0, jnp.where(jj == nt - 1, 2, 1))
        return start, variant

    def rows_of(idx):
        return slice(idx * SWA_SUB, (idx + 1) * SWA_SUB)

    def windows(stage):
        return [window(stage * SWA_STAGE_TILES + ti) for ti in range(SWA_STAGE_TILES)]

    def scores_into(slot, idx, q_ref, wins):
        ti, hd, t = subs[idx]
        start, variant = wins[ti]
        if t < n_band:
            k = kl_ref[0, pl.ds(start + t * SWA_SUB, SWA_SUB), :]
        else:
            k = kc_ref[0, (t - n_band) * SWA_SUB:(t - n_band + 1) * SWA_SUB, :]
        sj = _dot(k, q_ref[0, hd, :, ti * tq:(ti + 1) * tq])
        if t < n_band:
            sj = sj + bias_ref[variant, t * SWA_SUB:(t + 1) * SWA_SUB, :]
        s_scr[slot, rows_of(idx), :] = sj
        return jnp.max(sj.reshape(SWA_SUB // SUBLANE, SUBLANE, tq), axis=0)

    def values_t(g, t, t0):
        if t < n_band:
            tiles = [vlt_ref[0, g, t0 + t * lanes_per_sub + i] for i in range(lanes_per_sub)]
        else:
            tiles = [vct_ref[0, g, (t - n_band) * lanes_per_sub + i] for i in range(lanes_per_sub)]
        return jnp.concatenate(tiles, axis=1)

    def next_scores(slot, idx, nxt, m8n):
        q_ref, wins_n = nxt
        mj = scores_into(slot, idx, q_ref, wins_n)
        ti, hd, t = subs[idx]
        key = ti * SWA_HEADS + hd
        m8n[key] = mj if t == 0 else jnp.maximum(m8n[key], mj)
        if t == n_t - 1:
            m8_scr[slot, key] = m8n[key]

    def stage_body(slot, wins_cur, nxt):
        n_q = len(subs) if nxt is not None else 0
        m8n = {}
        for idx in range(min(SWA_QK_LEAD, n_q)):
            next_scores(1 - slot, idx, nxt, m8n)
        for idx, (ti, hd, t) in enumerate(subs):
            g = hd // SWA_GROUP
            t0 = wins_cur[ti][0] // LANE
            if t == 0:
                sk = sink_ref[hd] * LOG2E
                m8 = m8_scr[slot, ti * SWA_HEADS + hd]
                m = jnp.maximum(jnp.max(m8, axis=0, keepdims=True), sk)
                mb = jnp.broadcast_to(m, (SWA_CHUNK, tq))
                acc = None
            base = idx * SWA_SUB
            p = jnp.concatenate(
                [jnp.exp2(s_scr[slot, r:r + SWA_CHUNK, :] - mb).astype(BF16)
                 for r in range(base, base + SWA_SUB, SWA_CHUNK)], axis=0)
            d = _dot(values_t(g, t, t0), p)
            acc = d if acc is None else acc + d
            if t == n_t - 1:
                denom = acc[SWA_HD:SWA_HD + 1] + jnp.exp2(sk - m)
                ot_scr[ti, hd * SWA_HD:(hd + 1) * SWA_HD, :] = acc[0:SWA_HD] / denom
            if idx + SWA_QK_LEAD < n_q:
                next_scores(1 - slot, idx + SWA_QK_LEAD, nxt, m8n)

    wins = windows(j)

    @pl.when(j == 0)
    def _():
        m8n = {}
        for idx in range(len(subs)):
            next_scores(0, idx, (qt_ref, wins), m8n)

    for parity in range(2):
        @pl.when((lax.rem(j, 2) == parity) & (j < n_stage - 1))
        def _():
            stage_body(parity, wins, (qtn_ref, windows(j + 1)))

    @pl.when(j == n_stage - 1)
    def _():
        stage_body((n_stage - 1) % 2, wins, None)

    for ti in range(SWA_STAGE_TILES):
        o_ref[0, ti * tq:(ti + 1) * tq, :] = ot_scr[ti].T.astype(BF16)


def _swa_bias(seq):
    nt = seq // TOK_TILE
    out = []
    for j in (0, 1, nt - 1):
        start = min(max(j * TOK_TILE - WINDOW, 0), seq - SWA_WIN)
        qpos = j * TOK_TILE + np.arange(TOK_TILE)[None, :]
        kpos = start + np.arange(SWA_WIN)[:, None]
        out.append(np.where(np.abs(qpos - kpos) <= WINDOW, 0.0, NEG_INF))
    return jnp.asarray(np.stack(out), F32)


def _swa_attention(sink, qst, ks_c, ks_l, vst_c, vst_l):
    bsz, _, _, s = qst.shape
    n_ctx = ks_c.shape[1]
    nt = s // TOK_TILE
    assert nt >= 3 and s >= SWA_WIN and n_ctx % SWA_SUB == 0
    assert nt % SWA_STAGE_TILES == 0
    n_stage = nt // SWA_STAGE_TILES
    step_q = SWA_STAGE_TILES * TOK_TILE
    stage_rows = SWA_STAGE_TILES * SWA_HEADS * (SWA_WIN + n_ctx)
    return pl.pallas_call(
        functools.partial(_swa_kernel, seq=s),
        grid=(bsz, n_stage),
        in_specs=[
            pl.BlockSpec(memory_space=pltpu.SMEM),
            pl.BlockSpec((3, SWA_WIN, TOK_TILE), lambda b, j: (0, 0, 0)),
            pl.BlockSpec((1, SWA_HEADS, LANE, step_q), lambda b, j: (b, 0, 0, j)),
            pl.BlockSpec((1, SWA_HEADS, LANE, step_q), lambda b, j: (b, 0, 0, jnp.minimum(j + 1, n_stage - 1))),
            pl.BlockSpec((1, n_ctx, LANE), lambda b, j: (b, 0, 0)),
            pl.BlockSpec((1, s, LANE), lambda b, j: (b, 0, 0)),
            pl.BlockSpec((1, SWA_KV_HEADS, n_ctx // LANE, SWA_V_AUG, LANE), lambda b, j: (b, 0, 0, 0, 0)),
            pl.BlockSpec((1, SWA_KV_HEADS, s // LANE, SWA_V_AUG, LANE), lambda b, j: (b, 0, 0, 0, 0)),
        ],
        out_specs=pl.BlockSpec((1, step_q, SWA_HEADS * SWA_HD), lambda b, j: (b, j, 0)),
        out_shape=jax.ShapeDtypeStruct((bsz, s, SWA_HEADS * SWA_HD), BF16),
        scratch_shapes=[
            pltpu.VMEM((SWA_STAGE_TILES, SWA_HEADS * SWA_HD, TOK_TILE), F32),
            pltpu.VMEM((2, stage_rows, TOK_TILE), F32),
            pltpu.VMEM((2, SWA_STAGE_TILES * SWA_HEADS, SUBLANE, TOK_TILE), F32),
        ],
        compiler_params=pltpu.CompilerParams(vmem_limit_bytes=VMEM_LIMIT,
                                             dimension_semantics=("arbitrary", "arbitrary")),
        name="swa_attention",
    )(sink, _swa_bias(s), qst, qst, ks_c, ks_l, vst_c, vst_l)


def _mlp_kernel(x_ref, mm_ref, ms_ref, mod_ref, wo_ref, gmlp_ref, w1_ref, w2_ref, o_ref, acc_ref):
    half = MLA_HEADS * MLA_V
    y1 = _dot(mm_ref[0], wo_ref[0:half, :]) + _dot(ms_ref[0], wo_ref[half:, :])
    x1 = x_ref[0] + mod_ref[0, 2:3, :] * y1
    h2 = (_rms_rows(x1, D_MODEL) * gmlp_ref[...] * (1.0 + mod_ref[0, 4:5, :])
          + mod_ref[0, 3:4, :]).astype(BF16)
    for c in range(D_FF // FF_CHUNK):
        a = jnp.maximum(_dot(h2, w1_ref[:, c * FF_CHUNK:(c + 1) * FF_CHUNK]), 0.0)
        part = _dot((a * a).astype(BF16), w2_ref[c * FF_CHUNK:(c + 1) * FF_CHUNK, :])
        if c == 0:
            acc_ref[...] = part
        else:
            acc_ref[...] += part
    o_ref[0] = x1 + mod_ref[0, 5:6, :] * acc_ref[...]


def _outproj_mlp(x, mix_m, mix_s, mod3, w_out, g_mlp, w1, w2):
    bsz, s, _ = x.shape
    half = MLA_HEADS * MLA_V
    const = lambda b, j: (0, 0)
    single = pl.Buffered(1)
    return pl.pallas_call(
        _mlp_kernel,
        grid=(bsz, s // MLP_TILE),
        in_specs=[
            pl.BlockSpec((1, MLP_TILE, D_MODEL), lambda b, j: (b, j, 0)),
            pl.BlockSpec((1, MLP_TILE, half), lambda b, j: (b, j, 0)),
            pl.BlockSpec((1, MLP_TILE, half), lambda b, j: (b, j, 0)),
            pl.BlockSpec((1, 6, D_MODEL), lambda b, j: (b, 0, 0)),
            pl.BlockSpec((D_MODEL, D_MODEL), const, pipeline_mode=single),
            pl.BlockSpec((1, D_MODEL), const),
            pl.BlockSpec((D_MODEL, D_FF), const, pipeline_mode=single),
            pl.BlockSpec((D_FF, D_MODEL), const, pipeline_mode=single),
        ],
        out_specs=pl.BlockSpec((1, MLP_TILE, D_MODEL), lambda b, j: (b, j, 0)),
        out_shape=jax.ShapeDtypeStruct((bsz, s, D_MODEL), F32),
        scratch_shapes=[pltpu.VMEM((MLP_TILE, D_MODEL), F32)],
        compiler_params=pltpu.CompilerParams(vmem_limit_bytes=VMEM_LIMIT),
        name="outproj_mlp",
    )(x, mix_m, mix_s, mod3, w_out, g_mlp, w1, w2)


def _rope_tables(seq):
    f32 = np.float32

    def cos_sin(rot_dim):
        n_freq = rot_dim // 4
        inv = f32(ROPE_THETA) ** (-np.arange(n_freq, dtype=f32) / f32(n_freq))
        rows = seq // GRID_W
        row = np.repeat(np.arange(rows, dtype=f32), GRID_W)
        col = np.tile(np.arange(GRID_W, dtype=f32), rows)
        ang = np.concatenate([row[:, None] * inv, col[:, None] * inv], axis=-1).astype(f32)
        return np.cos(ang).astype(f32), np.sin(ang).astype(f32)

    mcos, msin = cos_sin(MLA_ROPE)
    scos, ssin = cos_sin(SWA_HD)
    z16 = np.zeros((seq, 16), f32)
    z32 = np.zeros((seq, 32), f32)
    z64 = np.zeros((seq, 64), f32)
    kc = np.concatenate([np.ones((seq, 64), f32), mcos, mcos, np.ones((seq, 32), f32)], axis=1)
    ks1 = np.concatenate([z64, -msin, z16, z32], axis=1)
    ks2 = np.concatenate([z64, z16, msin, z32], axis=1)
    sc = np.concatenate([scos, scos, scos, scos], axis=1)
    ss1 = np.concatenate([-ssin, z32, -ssin, z32], axis=1)
    ss2 = np.concatenate([z32, ssin, z32, ssin], axis=1)
    tabs = dict(mcos=mcos.T, msin=msin.T, scos=scos.T, ssin=ssin.T,
                kc=kc, ks1=ks1, ks2=ks2, sc=sc, ss1=ss1, ss2=ss2)
    return {k: jnp.asarray(np.ascontiguousarray(v)) for k, v in tabs.items()}


def _prep_weights(g_attn, w_in, g_q_a, w_uq, g_kv_a, w_ukv, g_mla_q, g_mla_k, g_swa_q, g_swa_k):
    o_ckv = Q_LORA
    o_kr = o_ckv + KV_LORA
    o_qs = o_kr + MLA_ROPE
    o_ks = o_qs + SWA_HEADS * SWA_HD
    o_vs = o_ks + SWA_KV_HEADS * SWA_HD
    w_cq = w_in[:, :o_ckv]
    w_ckv = w_in[:, o_ckv:o_kr]
    w_kr = w_in[:, o_kr:o_qs]
    w_qs = w_in[:, o_qs:o_ks]
    w_ks = w_in[:, o_ks:o_vs]
    w_vs = w_in[:, o_vs:]
    w_kr_p = jnp.pad(w_kr, ((0, 0), (MLA_NOPE, LANE - MLA_QK)))
    wtok = jnp.concatenate([w_cq, w_ckv, w_kr_p, w_ks], axis=1).astype(BF16)
    w_ukv_h = w_ukv.reshape(KV_LORA, MLA_HEADS, MLA_NOPE + MLA_V)
    wuk = jnp.pad(w_ukv_h[:, :, :MLA_NOPE], ((0, 0), (0, 0), (0, LANE - MLA_NOPE)))
    wuk = wuk.reshape(KV_LORA, MLA_HEADS * LANE).astype(BF16)
    wuv = w_ukv_h[:, :, MLA_NOPE:].reshape(KV_LORA, MLA_HEADS * MLA_V).T.astype(BF16)
    w_uq_h = jnp.pad(w_uq.reshape(Q_LORA, MLA_HEADS, MLA_QK), ((0, 0), (0, 0), (0, LANE - MLA_QK)))
    wuq = w_uq_h.reshape(Q_LORA, MLA_HEADS * LANE).T.astype(BF16)
    return dict(
        gattn=g_attn[None, :], wtok=wtok, wvs=w_vs.T.astype(BF16), gkva=g_kv_a[None, :],
        wuk=wuk, wuv=wuv, gmk=jnp.pad(g_mla_k, (0, LANE - MLA_QK))[None, :],
        gsk=jnp.tile(g_swa_k, SWA_KV_HEADS)[None, :],
        wqs=w_qs.T.astype(BF16), gqa=g_q_a[None, :], wuq=wuq,
        gmq=jnp.pad(g_mla_q, (0, LANE - MLA_QK))[:, None], gsq=g_swa_q[:, None],
    )


def kernel(x, c, ctx, c_ctx, w_mod, b_mod, g_attn, w_in, g_q_a, w_uq, g_kv_a, w_ukv, g_mla_q, g_mla_k,
           g_swa_q, g_swa_k, swa_sink, w_out, g_mlp, w_mlp1, w_mlp2):
    bsz, seq, _ = x.shape
    assert w_mod.shape[0] == 1, "single-layer block"
    assert bsz < SUBLANE and seq % PROJ_TILE == 0 and seq % MLP_TILE == 0 and ctx.shape[1] % TOK_TILE == 0

    cond = jnp.zeros((SUBLANE, D_MODEL), F32).at[:bsz].set(c).at[bsz].set(c_ctx)
    mod = _modulation(cond, w_mod[0], b_mod[0][None, :])
    mod3 = mod.reshape(SUBLANE, 6, D_MODEL)

    weights = _prep_weights(g_attn[0], w_in[0], g_q_a[0], w_uq[0], g_kv_a[0], w_ukv[0],
                            g_mla_q[0], g_mla_k[0], g_swa_q[0], g_swa_k[0])
    tables = _rope_tables(seq)

    weights["to_cast"] = [w_out[0], w_mlp1[0], w_mlp2[0]]
    (km_l, vmt_l, ks_l, vst_l, qmt, qst,
     w_out_bf16, w_mlp1_bf16, w_mlp2_bf16) = _project(x, mod3, None, weights, tables, latent=True, tile=PROJ_TILE)
    km_c, vmt_c, ks_c, vst_c = _project(ctx, mod3, bsz, weights, None, latent=False, tile=TOK_TILE)

    mix_m = _mla_attention(qmt, km_c, km_l, vmt_c, vmt_l)
    mix_s = _swa_attention(swa_sink[0], qst, ks_c, ks_l, vst_c, vst_l)

    return _outproj_mlp(x, mix_m, mix_s, mod3, w_out_bf16, g_mlp[0][None, :], w_mlp1_bf16, w_mlp2_bf16)
```

```python
import functools
import math

import jax
import jax.numpy as jnp
import numpy as np
from jax import lax
from jax.experimental import pallas as pl
from jax.experimental.pallas import tpu as pltpu

D_MODEL = 1024
GRID_W = 64
MLA_HEADS = 8
MLA_NOPE = 64
MLA_ROPE = 32
MLA_QK = MLA_NOPE + MLA_ROPE
MLA_V = 64
MLA_V_AUG = MLA_V + 16
Q_LORA = 256
KV_LORA = 128
SWA_HEADS = 8
SWA_KV_HEADS = 2
SWA_GROUP = SWA_HEADS // SWA_KV_HEADS
SWA_HD = 64
SWA_V_AUG = SWA_HD + 16
WINDOW = 128
D_FF = 4 * D_MODEL
ROPE_THETA = 10000.0
EPS = 1e-6
NEG_INF = -1e30
LOG2E = 1.4426950408889634
MLA_QSCALE = LOG2E / math.sqrt(MLA_QK)
SWA_QSCALE = LOG2E / math.sqrt(SWA_HD)

LANE = 128
SUBLANE = 8
MXU_DIM = 256
TOK_TILE = MXU_DIM
PROJ_TILE = 1024
PROJ_PART = 256
MOD_TILE = 1024
MLP_TILE = 512
FF_CHUNK = 1024
SWA_WIN = 2 * TOK_TILE
MLA_SUB = MXU_DIM
MLA_Q_TILES = 4
MLA_STAGE_HEADS = 2
SWA_SUB = MXU_DIM
SWA_STAGE_TILES = 2
MLA_CHUNK = 32
SWA_CHUNK = 32
MLA_QK_LEAD = 3
SWA_QK_LEAD = 1
VMEM_LIMIT = 56 * 1024 * 1024

F32 = jnp.float32
BF16 = jnp.bfloat16
NT_DIMS = (((1,), (1,)), ((), ()))


def _dot(a, b):
    return jnp.dot(a, b, preferred_element_type=F32)


def _dot_nt(a, b):
    return lax.dot_general(a, b, NT_DIMS, preferred_element_type=F32)


def _mod_kernel(cond_ref, w_ref, b_ref, o_ref):
    cnd = cond_ref[...]
    act = cnd * jax.nn.sigmoid(cnd)
    o_ref[...] = _dot(act.astype(BF16), w_ref[...].astype(BF16)) + b_ref[...]


def _modulation(cond, w_mod, b_mod):
    n = w_mod.shape[1]
    tn = MOD_TILE
    return pl.pallas_call(
        _mod_kernel,
        grid=(n // tn,),
        in_specs=[
            pl.BlockSpec((SUBLANE, D_MODEL), lambda i: (0, 0)),
            pl.BlockSpec((D_MODEL, tn), lambda i: (0, i)),
            pl.BlockSpec((1, tn), lambda i: (0, i)),
        ],
        out_specs=pl.BlockSpec((SUBLANE, tn), lambda i: (0, i)),
        out_shape=jax.ShapeDtypeStruct((SUBLANE, n), F32),
        name="modulation",
    )(cond, w_mod, b_mod)


def _rms_rows(x, n):
    return x * lax.rsqrt(jnp.sum(x * x, axis=-1, keepdims=True) * (1.0 / n) + EPS)


def _rms_cols(x, n):
    return x * lax.rsqrt(jnp.sum(x * x, axis=0, keepdims=True) * (1.0 / n) + EPS)


def _proj_kernel(*refs, latent):
    if latent:
        (x_ref, mod_ref, gattn_ref, wtok_ref, wvs_ref, gkva_ref, wuk_ref, wuv_ref, gmk_ref, gsk_ref,
         wqs_ref, gqa_ref, wuq_ref, gmq_ref, gsq_ref,
         mcos_ref, msin_ref, scos_ref, ssin_ref, kc_ref, ks1_ref, ks2_ref, sc_ref, ss1_ref, ss2_ref,
         wo_f32_ref, w1_f32_ref, w2_f32_ref,
         km_ref, vmt_ref, ks_ref, vst_ref, qmt_ref, qst_ref,
         wo_bf16_ref, w1_bf16_ref, w2_bf16_ref) = refs
        for src, dst in ((wo_f32_ref, wo_bf16_ref), (w1_f32_ref, w1_bf16_ref), (w2_f32_ref, w2_bf16_ref)):
            dst[...] = src[...].astype(BF16)
    else:
        (x_ref, mod_ref, gattn_ref, wtok_ref, wvs_ref, gkva_ref, wuk_ref, wuv_ref, gmk_ref, gsk_ref,
         km_ref, vmt_ref, ks_ref, vst_ref) = refs

    tile = x_ref.shape[1]
    n_part = max(1, tile // PROJ_PART)
    part = tile // n_part

    gain = gattn_ref[...] * (1.0 + mod_ref[0, 1:2, :])
    shift = mod_ref[0, 0:1, :]
    gmk = gmk_ref[...]
    ones_row = (lax.broadcasted_iota(jnp.int32, (MLA_V_AUG - MLA_V, part), 0) == 0).astype(BF16)
    if latent:
        gmq = gmq_ref[...] * MLA_QSCALE
        gsq = gsq_ref[...] * SWA_QSCALE

    def part_phases(lo):
        rows = slice(lo, lo + part)

        h = (_rms_rows(x_ref[0, rows, :], D_MODEL) * gain + shift).astype(BF16)
        yield

        zt = _dot(h, wtok_ref[...])
        vst = _dot_nt(wvs_ref[...], h)
        if latent:
            qst = _dot_nt(wqs_ref[...], h)
        yield

        cq = zt[:, 0:256]
        ckv = zt[:, 256:384]
        krp = zt[:, 384:512]
        ksw = zt[:, 512:640]
        ckvn = (_rms_rows(ckv, KV_LORA) * gkva_ref[...]).astype(BF16)
        if latent:
            cqn = (_rms_rows(cq, Q_LORA) * gqa_ref[...]).astype(BF16)
        kr = krp * gmk
        if latent:
            kr = (kr * kc_ref[rows, :] + pltpu.roll(kr, 112, 1) * ks1_ref[rows, :]
                  + pltpu.roll(kr, 16, 1) * ks2_ref[rows, :])
        ss_kr = jnp.sum(krp * krp, axis=-1, keepdims=True)
        first = lax.broadcasted_iota(jnp.int32, ksw.shape, 1) < SWA_HD
        sq = ksw * ksw
        ss0 = jnp.sum(jnp.where(first, sq, 0.0), axis=-1, keepdims=True)
        ss1 = jnp.sum(jnp.where(first, 0.0, sq), axis=-1, keepdims=True)
        rr = jnp.where(first, lax.rsqrt(ss0 * (1.0 / SWA_HD) + EPS), lax.rsqrt(ss1 * (1.0 / SWA_HD) + EPS))
        ksn = ksw * rr * gsk_ref[...]
        if latent:
            ksn = (ksn * sc_ref[rows, :] + pltpu.roll(ksn, 96, 1) * ss1_ref[rows, :]
                   + pltpu.roll(ksn, 32, 1) * ss2_ref[rows, :])
        ks_ref[0, rows, :] = ksn.astype(BF16)
        for g in range(SWA_KV_HEADS):
            for t in range(part // LANE):
                dst = lo // LANE + t
                vst_ref[0, g, dst, 0:SWA_HD, :] = vst[g * SWA_HD:(g + 1) * SWA_HD, t * LANE:(t + 1) * LANE].astype(BF16)
                vst_ref[0, g, dst, SWA_HD:SWA_V_AUG, :] = ones_row[:, 0:LANE]
        yield

        kpre = _dot(ckvn, wuk_ref[...])
        vt = _dot_nt(wuv_ref[...], ckvn)
        if latent:
            qt = _dot_nt(wuq_ref[...], cqn)
        yield

        for hd in range(MLA_HEADS):
            kp = kpre[:, hd * LANE:(hd + 1) * LANE]
            ss = jnp.sum(kp * kp, axis=-1, keepdims=True) + ss_kr
            kn = (kp * gmk + kr) * lax.rsqrt(ss * (1.0 / MLA_QK) + EPS)
            km_ref[0, hd, rows, :] = kn.astype(BF16)
            vmt_ref[0, hd, 0:MLA_V, rows] = vt[hd * MLA_V:(hd + 1) * MLA_V].astype(BF16)
            vmt_ref[0, hd, MLA_V:MLA_V_AUG, rows] = ones_row
        if latent:
            mcos = mcos_ref[:, rows]
            msin = msin_ref[:, rows]
            for hd in range(MLA_HEADS):
                qn = _rms_cols(qt[hd * MLA_QK:(hd + 1) * MLA_QK], MLA_QK) * gmq
                x1 = qn[64:80]
                x2 = qn[80:96]
                qt_idx = lo // TOK_TILE
                qmt_ref[0, hd, qt_idx, 0:64, :] = qn[0:64].astype(BF16)
                qmt_ref[0, hd, qt_idx, 64:80, :] = (x1 * mcos - x2 * msin).astype(BF16)
                qmt_ref[0, hd, qt_idx, 80:96, :] = (x2 * mcos + x1 * msin).astype(BF16)
                qmt_ref[0, hd, qt_idx, 96:128, :] = jnp.zeros((32, part), BF16)
            scos = scos_ref[:, rows]
            ssin = ssin_ref[:, rows]
            for hd in range(SWA_HEADS):
                qn = _rms_cols(qst[hd * SWA_HD:(hd + 1) * SWA_HD], SWA_HD) * gsq
                x1 = qn[0:32]
                x2 = qn[32:64]
                g = hd // SWA_GROUP
                base = g * SWA_HD
                other = (1 - g) * SWA_HD
                qst_ref[0, hd, base:base + 32, rows] = (x1 * scos - x2 * ssin).astype(BF16)
                qst_ref[0, hd, base + 32:base + 64, rows] = (x2 * scos + x1 * ssin).astype(BF16)
                qst_ref[0, hd, other:other + 64, rows] = jnp.zeros((64, part), BF16)
        yield

    n_phase = 5
    parts = [part_phases(i * part) for i in range(n_part)]
    for step in range(n_phase + n_part - 1):
        for i, gen in enumerate(parts):
            if 0 <= step - i < n_phase:
                next(gen)


def _full(shape):
    nd = len(shape)
    return pl.BlockSpec(shape, lambda b, j: (0,) * nd)


def _project(x, mod3, mod_row, weights, tables, latent, tile):
    bsz, n, _ = x.shape
    nt = n // tile
    if mod_row is None:
        mod_map = lambda b, j: (b, 0, 0)
    else:
        mod_map = lambda b, j: (mod_row, 0, 0)
    common = [weights[k] for k in ("gattn", "wtok", "wvs", "gkva", "wuk", "wuv", "gmk", "gsk")]
    ins = [x, mod3] + common
    in_specs = [
        pl.BlockSpec((1, tile, D_MODEL), lambda b, j: (b, j, 0)),
        pl.BlockSpec((1, 6, D_MODEL), mod_map),
    ] + [_full(w.shape) for w in common]
    out_shape = [
        jax.ShapeDtypeStruct((bsz, MLA_HEADS, n, LANE), BF16),
        jax.ShapeDtypeStruct((bsz, MLA_HEADS, MLA_V_AUG, n), BF16),
        jax.ShapeDtypeStruct((bsz, n, LANE), BF16),
        jax.ShapeDtypeStruct((bsz, SWA_KV_HEADS, n // LANE, SWA_V_AUG, LANE), BF16),
    ]
    out_specs = [
        pl.BlockSpec((1, MLA_HEADS, tile, LANE), lambda b, j: (b, 0, j, 0)),
        pl.BlockSpec((1, MLA_HEADS, MLA_V_AUG, tile), lambda b, j: (b, 0, 0, j)),
        pl.BlockSpec((1, tile, LANE), lambda b, j: (b, j, 0)),
        pl.BlockSpec((1, SWA_KV_HEADS, tile // LANE, SWA_V_AUG, LANE), lambda b, j: (b, 0, j, 0, 0)),
    ]
    if latent:
        extra = [weights[k] for k in ("wqs", "gqa", "wuq", "gmq", "gsq")]
        ins += extra
        in_specs += [_full(w.shape) for w in extra]
        feat = [tables[k] for k in ("mcos", "msin", "scos", "ssin")]
        ins += feat
        in_specs += [pl.BlockSpec((t.shape[0], tile), lambda b, j: (0, j)) for t in feat]
        tok = [tables[k] for k in ("kc", "ks1", "ks2", "sc", "ss1", "ss2")]
        ins += tok
        in_specs += [pl.BlockSpec((tile, LANE), lambda b, j: (j, 0)) for _ in tok]
        assert PROJ_PART == TOK_TILE
        out_shape += [
            jax.ShapeDtypeStruct((bsz, MLA_HEADS, n // TOK_TILE, LANE, TOK_TILE), BF16),
            jax.ShapeDtypeStruct((bsz, SWA_HEADS, LANE, n), BF16),
        ]
        out_specs += [
            pl.BlockSpec((1, MLA_HEADS, tile // TOK_TILE, LANE, TOK_TILE), lambda b, j: (b, 0, j, 0, 0)),
            pl.BlockSpec((1, SWA_HEADS, LANE, tile), lambda b, j: (b, 0, 0, j)),
        ]
        steps = bsz * nt
        for w in weights["to_cast"]:
            rows = w.shape[0] // steps
            assert rows * steps == w.shape[0] and rows % 16 == 0
            ins.append(w)
            in_specs.append(pl.BlockSpec((rows, w.shape[1]), lambda b, j: (b * nt + j, 0)))
            out_shape.append(jax.ShapeDtypeStruct(w.shape, BF16))
            out_specs.append(pl.BlockSpec((rows, w.shape[1]), lambda b, j: (b * nt + j, 0)))
    return pl.pallas_call(
        functools.partial(_proj_kernel, latent=latent),
        grid=(bsz, nt),
        in_specs=in_specs,
        out_specs=out_specs,
        out_shape=out_shape,
        compiler_params=pltpu.CompilerParams(vmem_limit_bytes=VMEM_LIMIT),
        name="project_latent" if latent else "project_context",
    )(*ins)


def _mla_kernel(qt_ref, kc_ref, kl_ref, vct_ref, vlt_ref, o_ref, ot_scr, s_scr, *, seq):
    n_ctx = kc_ref.shape[2]
    tq = TOK_TILE
    key_subs = ([(kc_ref, vct_ref, o) for o in range(0, n_ctx, MLA_SUB)]
                + [(kl_ref, vlt_ref, o) for o in range(0, seq, MLA_SUB)])
    n_key = len(key_subs)
    subs = [(h, j) for h in range(MLA_STAGE_HEADS) for j in range(n_key)]
    groups = MLA_HEADS // MLA_STAGE_HEADS

    def scores_into(slot, idx, stage, m8):
        t, grp = stage
        h, j = subs[idx]
        hd = grp * MLA_STAGE_HEADS + h
        k_ref, _, off = key_subs[j]
        sj = _dot(k_ref[0, hd, off:off + MLA_SUB, :], qt_ref[0, hd, t])
        s_scr[slot, idx * MLA_SUB:(idx + 1) * MLA_SUB, :] = sj
        mj = jnp.max(sj.reshape(MLA_SUB // SUBLANE, SUBLANE, tq), axis=0)
        m8 = list(m8)
        m8[h] = mj if j == 0 else jnp.maximum(m8[h], mj)
        return tuple(m8)

    def stage_body(stage, m8_cur, slot, nxt):
        t, grp = stage
        n_q = len(subs) if nxt is not None else 0
        m8_next = (None,) * MLA_STAGE_HEADS
        for idx in range(min(MLA_QK_LEAD, n_q)):
            m8_next = scores_into(1 - slot, idx, nxt, m8_next)
        for idx, (h, j) in enumerate(subs):
            hd = grp * MLA_STAGE_HEADS + h
            if j == 0:
                mb = jnp.broadcast_to(jnp.max(m8_cur[h], axis=0, keepdims=True), (MLA_CHUNK, tq))
                acc = None
            base = idx * MLA_SUB
            p = jnp.concatenate(
                [jnp.exp2(s_scr[slot, r:r + MLA_CHUNK, :] - mb).astype(BF16)
                 for r in range(base, base + MLA_SUB, MLA_CHUNK)], axis=0)
            _, v_ref, off = key_subs[j]
            d = _dot(v_ref[0, hd, :, off:off + MLA_SUB], p)
            acc = d if acc is None else acc + d
            if j == n_key - 1:
                row = pl.multiple_of(hd * MLA_V, MLA_V)
                ot_scr[t, pl.ds(row, MLA_V), :] = acc[0:MLA_V] / acc[MLA_V:MLA_V + 1]
            if idx + MLA_QK_LEAD < n_q:
                m8_next = scores_into(1 - slot, idx + MLA_QK_LEAD, nxt, m8_next)
        return m8_next

    def stage_of(n):
        return n // groups, lax.rem(n, groups)

    def stage_step(n, m8):
        return lax.cond(lax.rem(n, 2) == 0,
                        lambda m: stage_body(stage_of(n), m, 0, stage_of(n + 1)),
                        lambda m: stage_body(stage_of(n), m, 1, stage_of(n + 1)), m8)

    n_stage = groups * MLA_Q_TILES
    m8_first = (None,) * MLA_STAGE_HEADS
    for idx in range(len(subs)):
        m8_first = scores_into(0, idx, (0, 0), m8_first)
    m8_last = lax.fori_loop(0, n_stage - 1, stage_step, m8_first)
    stage_body((MLA_Q_TILES - 1, groups - 1), m8_last, (n_stage - 1) % 2, None)
    for t in range(MLA_Q_TILES):
        o_ref[0, t * tq:(t + 1) * tq, :] = ot_scr[t].T.astype(BF16)


def _mla_attention(qmt, km_c, km_l, vmt_c, vmt_l):
    bsz = qmt.shape[0]
    s = km_l.shape[2]
    n_ctx = km_c.shape[2]
    step_q = MLA_Q_TILES * TOK_TILE
    assert s % step_q == 0
    n_keys = n_ctx + s
    return pl.pallas_call(
        functools.partial(_mla_kernel, seq=s),
        grid=(bsz, s // step_q),
        in_specs=[
            pl.BlockSpec((1, MLA_HEADS, MLA_Q_TILES, LANE, TOK_TILE), lambda b, j: (b, 0, j, 0, 0)),
            pl.BlockSpec((1, MLA_HEADS, n_ctx, LANE), lambda b, j: (b, 0, 0, 0)),
            pl.BlockSpec((1, MLA_HEADS, s, LANE), lambda b, j: (b, 0, 0, 0)),
            pl.BlockSpec((1, MLA_HEADS, MLA_V_AUG, n_ctx), lambda b, j: (b, 0, 0, 0)),
            pl.BlockSpec((1, MLA_HEADS, MLA_V_AUG, s), lambda b, j: (b, 0, 0, 0)),
        ],
        out_specs=pl.BlockSpec((1, step_q, MLA_HEADS * MLA_V), lambda b, j: (b, j, 0)),
        out_shape=jax.ShapeDtypeStruct((bsz, s, MLA_HEADS * MLA_V), BF16),
        scratch_shapes=[
            pltpu.VMEM((MLA_Q_TILES, MLA_HEADS * MLA_V, TOK_TILE), F32),
            pltpu.VMEM((2, MLA_STAGE_HEADS * n_keys, TOK_TILE), F32),
        ],
        compiler_params=pltpu.CompilerParams(vmem_limit_bytes=VMEM_LIMIT,
                                             dimension_semantics=("arbitrary", "arbitrary")),
        name="mla_attention",
    )(qmt, km_c, km_l, vmt_c, vmt_l)


def _swa_kernel(sink_ref, bias_ref, qt_ref, qtn_ref, kc_ref, kl_ref, vct_ref, vlt_ref, o_ref,
                ot_scr, s_scr, m8_scr, *, seq):
    j = pl.program_id(1)
    nt = seq // TOK_TILE
    n_stage = nt // SWA_STAGE_TILES
    n_ctx = kc_ref.shape[1]
    tq = TOK_TILE
    n_band = SWA_WIN // SWA_SUB
    n_t = n_band + n_ctx // SWA_SUB
    lanes_per_sub = SWA_SUB // LANE
    subs = [(ti, hd, t) for ti in range(SWA_STAGE_TILES) for hd in range(SWA_HEADS) for t in range(n_t)]

    def window(jj):
        start = pl.multiple_of(jnp.clip(jj * TOK_TILE - WINDOW, 0, seq - SWA_WIN), LANE)
        variant = jnp.where(jj == 0, 0, jnp.where(jj == nt - 1, 2, 1))
        return start, variant

    def rows_of(idx):
        return slice(idx * SWA_SUB, (idx + 1) * SWA_SUB)

    def windows(stage):
        return [window(stage * SWA_STAGE_TILES + ti) for ti in range(SWA_STAGE_TILES)]

    def scores_into(slot, idx, q_ref, wins):
        ti, hd, t = subs[idx]
        start, variant = wins[ti]
        if t < n_band:
            k = kl_ref[0, pl.ds(start + t * SWA_SUB, SWA_SUB), :]
        else:
            k = kc_ref[0, (t - n_band) * SWA_SUB:(t - n_band + 1) * SWA_SUB, :]
        sj = _dot(k, q_ref[0, hd, :, ti * tq:(ti + 1) * tq])
        if t < n_band:
            sj = sj + bias_ref[variant, t * SWA_SUB:(t + 1) * SWA_SUB, :]
        s_scr[slot, rows_of(idx), :] = sj
        return jnp.max(sj.reshape(SWA_SUB // SUBLANE, SUBLANE, tq), axis=0)

    def values_t(g, t, t0):
        if t < n_band:
            tiles = [vlt_ref[0, g, t0 + t * lanes_per_sub + i] for i in range(lanes_per_sub)]
        else:
            tiles = [vct_ref[0, g, (t - n_band) * lanes_per_sub + i] for i in range(lanes_per_sub)]
        return jnp.concatenate(tiles, axis=1)

    def next_scores(slot, idx, nxt, m8n):
        q_ref, wins_n = nxt
        mj = scores_into(slot, idx, q_ref, wins_n)
        ti, hd, t = subs[idx]
        key = ti * SWA_HEADS + hd
        m8n[key] = mj if t == 0 else jnp.maximum(m8n[key], mj)
        if t == n_t - 1:
            m8_scr[slot, key] = m8n[key]

    def stage_body(slot, wins_cur, nxt):
        n_q = len(subs) if nxt is not None else 0
        m8n = {}
        for idx in range(min(SWA_QK_LEAD, n_q)):
            next_scores(1 - slot, idx, nxt, m8n)
        for idx, (ti, hd, t) in enumerate(subs):
            g = hd // SWA_GROUP
            t0 = wins_cur[ti][0] // LANE
            if t == 0:
                sk = sink_ref[hd] * LOG2E
                m8 = m8_scr[slot, ti * SWA_HEADS + hd]
                m = jnp.maximum(jnp.max(m8, axis=0, keepdims=True), sk)
                mb = jnp.broadcast_to(m, (SWA_CHUNK, tq))
                acc = None
            base = idx * SWA_SUB
            p = jnp.concatenate(
                [jnp.exp2(s_scr[slot, r:r + SWA_CHUNK, :] - mb).astype(BF16)
                 for r in range(base, base + SWA_SUB, SWA_CHUNK)], axis=0)
            d = _dot(values_t(g, t, t0), p)
            acc = d if acc is None else acc + d
            if t == n_t - 1:
                denom = acc[SWA_HD:SWA_HD + 1] + jnp.exp2(sk - m)
                ot_scr[ti, hd * SWA_HD:(hd + 1) * SWA_HD, :] = acc[0:SWA_HD] / denom
            if idx + SWA_QK_LEAD < n_q:
                next_scores(1 - slot, idx + SWA_QK_LEAD, nxt, m8n)

    wins = windows(j)

    @pl.when(j == 0)
    def _():
        m8n = {}
        for idx in range(len(subs)):
            next_scores(0, idx, (qt_ref, wins), m8n)

    for parity in range(2):
        @pl.when((lax.rem(j, 2) == parity) & (j < n_stage - 1))
        def _():
            stage_body(parity, wins, (qtn_ref, windows(j + 1)))

    @pl.when(j == n_stage - 1)
    def _():
        stage_body((n_stage - 1) % 2, wins, None)

    for ti in range(SWA_STAGE_TILES):
        o_ref[0, ti * tq:(ti + 1) * tq, :] = ot_scr[ti].T.astype(BF16)


def _swa_bias(seq):
    nt = seq // TOK_TILE
    out = []
    for j in (0, 1, nt - 1):
        start = min(max(j * TOK_TILE - WINDOW, 0), seq - SWA_WIN)
        qpos = j * TOK_TILE + np.arange(TOK_TILE)[None, :]
        kpos = start + np.arange(SWA_WIN)[:, None]
        out.append(np.where(np.abs(qpos - kpos) <= WINDOW, 0.0, NEG_INF))
    return jnp.asarray(np.stack(out), F32)


def _swa_attention(sink, qst, ks_c, ks_l, vst_c, vst_l):
    bsz, _, _, s = qst.shape
    n_ctx = ks_c.shape[1]
    nt = s // TOK_TILE
    assert nt >= 3 and s >= SWA_WIN and n_ctx % SWA_SUB == 0
    assert nt % SWA_STAGE_TILES == 0
    n_stage = nt // SWA_STAGE_TILES
    step_q = SWA_STAGE_TILES * TOK_TILE
    stage_rows = SWA_STAGE_TILES * SWA_HEADS * (SWA_WIN + n_ctx)
    return pl.pallas_call(
        functools.partial(_swa_kernel, seq=s),
        grid=(bsz, n_stage),
        in_specs=[
            pl.BlockSpec(memory_space=pltpu.SMEM),
            pl.BlockSpec((3, SWA_WIN, TOK_TILE), lambda b, j: (0, 0, 0)),
            pl.BlockSpec((1, SWA_HEADS, LANE, step_q), lambda b, j: (b, 0, 0, j)),
            pl.BlockSpec((1, SWA_HEADS, LANE, step_q), lambda b, j: (b, 0, 0, jnp.minimum(j + 1, n_stage - 1))),
            pl.BlockSpec((1, n_ctx, LANE), lambda b, j: (b, 0, 0)),
            pl.BlockSpec((1, s, LANE), lambda b, j: (b, 0, 0)),
            pl.BlockSpec((1, SWA_KV_HEADS, n_ctx // LANE, SWA_V_AUG, LANE), lambda b, j: (b, 0, 0, 0, 0)),
            pl.BlockSpec((1, SWA_KV_HEADS, s // LANE, SWA_V_AUG, LANE), lambda b, j: (b, 0, 0, 0, 0)),
        ],
        out_specs=pl.BlockSpec((1, step_q, SWA_HEADS * SWA_HD), lambda b, j: (b, j, 0)),
        out_shape=jax.ShapeDtypeStruct((bsz, s, SWA_HEADS * SWA_HD), BF16),
        scratch_shapes=[
            pltpu.VMEM((SWA_STAGE_TILES, SWA_HEADS * SWA_HD, TOK_TILE), F32),
            pltpu.VMEM((2, stage_rows, TOK_TILE), F32),
            pltpu.VMEM((2, SWA_STAGE_TILES * SWA_HEADS, SUBLANE, TOK_TILE), F32),
        ],
        compiler_params=pltpu.CompilerParams(vmem_limit_bytes=VMEM_LIMIT,
                                             dimension_semantics=("arbitrary", "arbitrary")),
        name="swa_attention",
    )(sink, _swa_bias(s), qst, qst, ks_c, ks_l, vst_c, vst_l)


def _mlp_kernel(x_ref, mm_ref, ms_ref, mod_ref, wo_ref, gmlp_ref, w1_ref, w2_ref, o_ref, acc_ref):
    half = MLA_HEADS * MLA_V
    y1 = _dot(mm_ref[0], wo_ref[0:half, :]) + _dot(ms_ref[0], wo_ref[half:, :])
    x1 = x_ref[0] + mod_ref[0, 2:3, :] * y1
    h2 = (_rms_rows(x1, D_MODEL) * gmlp_ref[...] * (1.0 + mod_ref[0, 4:5, :])
          + mod_ref[0, 3:4, :]).astype(BF16)
    for c in range(D_FF // FF_CHUNK):
        a = jnp.maximum(_dot(h2, w1_ref[:, c * FF_CHUNK:(c + 1) * FF_CHUNK]), 0.0)
        part = _dot((a * a).astype(BF16), w2_ref[c * FF_CHUNK:(c + 1) * FF_CHUNK, :])
        if c == 0:
            acc_ref[...] = part
        else:
            acc_ref[...] += part
    o_ref[0] = x1 + mod_ref[0, 5:6, :] * acc_ref[...]


def _outproj_mlp(x, mix_m, mix_s, mod3, w_out, g_mlp, w1, w2):
    bsz, s, _ = x.shape
    half = MLA_HEADS * MLA_V
    const = lambda b, j: (0, 0)
    single = pl.Buffered(1)
    return pl.pallas_call(
        _mlp_kernel,
        grid=(bsz, s // MLP_TILE),
        in_specs=[
            pl.BlockSpec((1, MLP_TILE, D_MODEL), lambda b, j: (b, j, 0)),
            pl.BlockSpec((1, MLP_TILE, half), lambda b, j: (b, j, 0)),
            pl.BlockSpec((1, MLP_TILE, half), lambda b, j: (b, j, 0)),
            pl.BlockSpec((1, 6, D_MODEL), lambda b, j: (b, 0, 0)),
            pl.BlockSpec((D_MODEL, D_MODEL), const, pipeline_mode=single),
            pl.BlockSpec((1, D_MODEL), const),
            pl.BlockSpec((D_MODEL, D_FF), const, pipeline_mode=single),
            pl.BlockSpec((D_FF, D_MODEL), const, pipeline_mode=single),
        ],
        out_specs=pl.BlockSpec((1, MLP_TILE, D_MODEL), lambda b, j: (b, j, 0)),
        out_shape=jax.ShapeDtypeStruct((bsz, s, D_MODEL), F32),
        scratch_shapes=[pltpu.VMEM((MLP_TILE, D_MODEL), F32)],
        compiler_params=pltpu.CompilerParams(vmem_limit_bytes=VMEM_LIMIT),
        name="outproj_mlp",
    )(x, mix_m, mix_s, mod3, w_out, g_mlp, w1, w2)


def _rope_tables(seq):
    f32 = np.float32

    def cos_sin(rot_dim):
        n_freq = rot_dim // 4
        inv = f32(ROPE_THETA) ** (-np.arange(n_freq, dtype=f32) / f32(n_freq))
        rows = seq // GRID_W
        row = np.repeat(np.arange(rows, dtype=f32), GRID_W)
        col = np.tile(np.arange(GRID_W, dtype=f32), rows)
        ang = np.concatenate([row[:, None] * inv, col[:, None] * inv], axis=-1).astype(f32)
        return np.cos(ang).astype(f32), np.sin(ang).astype(f32)

    mcos, msin = cos_sin(MLA_ROPE)
    scos, ssin = cos_sin(SWA_HD)
    z16 = np.zeros((seq, 16), f32)
    z32 = np.zeros((seq, 32), f32)
    z64 = np.zeros((seq, 64), f32)
    kc = np.concatenate([np.ones((seq, 64), f32), mcos, mcos, np.ones((seq, 32), f32)], axis=1)
    ks1 = np.concatenate([z64, -msin, z16, z32], axis=1)
    ks2 = np.concatenate([z64, z16, msin, z32], axis=1)
    sc = np.concatenate([scos, scos, scos, scos], axis=1)
    ss1 = np.concatenate([-ssin, z32, -ssin, z32], axis=1)
    ss2 = np.concatenate([z32, ssin, z32, ssin], axis=1)
    tabs = dict(mcos=mcos.T, msin=msin.T, scos=scos.T, ssin=ssin.T,
                kc=kc, ks1=ks1, ks2=ks2, sc=sc, ss1=ss1, ss2=ss2)
    return {k: jnp.asarray(np.ascontiguousarray(v)) for k, v in tabs.items()}


def _prep_weights(g_attn, w_in, g_q_a, w_uq, g_kv_a, w_ukv, g_mla_q, g_mla_k, g_swa_q, g_swa_k):
    o_ckv = Q_LORA
    o_kr = o_ckv + KV_LORA
    o_qs = o_kr + MLA_ROPE
    o_ks = o_qs + SWA_HEADS * SWA_HD
    o_vs = o_ks + SWA_KV_HEADS * SWA_HD
    w_cq = w_in[:, :o_ckv]
    w_ckv = w_in[:, o_ckv:o_kr]
    w_kr = w_in[:, o_kr:o_qs]
    w_qs = w_in[:, o_qs:o_ks]
    w_ks = w_in[:, o_ks:o_vs]
    w_vs = w_in[:, o_vs:]
    w_kr_p = jnp.pad(w_kr, ((0, 0), (MLA_NOPE, LANE - MLA_QK)))
    wtok = jnp.concatenate([w_cq, w_ckv, w_kr_p, w_ks], axis=1).astype(BF16)
    w_ukv_h = w_ukv.reshape(KV_LORA, MLA_HEADS, MLA_NOPE + MLA_V)
    wuk = jnp.pad(w_ukv_h[:, :, :MLA_NOPE], ((0, 0), (0, 0), (0, LANE - MLA_NOPE)))
    wuk = wuk.reshape(KV_LORA, MLA_HEADS * LANE).astype(BF16)
    wuv = w_ukv_h[:, :, MLA_NOPE:].reshape(KV_LORA, MLA_HEADS * MLA_V).T.astype(BF16)
    wuq = w_uq.T.astype(BF16)
    return dict(
        gattn=g_attn[None, :], wtok=wtok, wvs=w_vs.T.astype(BF16), gkva=g_kv_a[None, :],
        wuk=wuk, wuv=wuv, gmk=jnp.pad(g_mla_k, (0, LANE - MLA_QK))[None, :],
        gsk=jnp.tile(g_swa_k, SWA_KV_HEADS)[None, :],
        wqs=w_qs.T.astype(BF16), gqa=g_q_a[None, :], wuq=wuq,
        gmq=g_mla_q[:, None], gsq=g_swa_q[:, None],
    )


def kernel(x, c, ctx, c_ctx, w_mod, b_mod, g_attn, w_in, g_q_a, w_uq, g_kv_a, w_ukv, g_mla_q, g_mla_k,
           g_swa_q, g_swa_k, swa_sink, w_out, g_mlp, w_mlp1, w_mlp2):
    bsz, seq, _ = x.shape
    assert w_mod.shape[0] == 1, "single-layer block"
    assert bsz < SUBLANE and seq % PROJ_TILE == 0 and seq % MLP_TILE == 0 and ctx.shape[1] % TOK_TILE == 0

    cond = jnp.zeros((SUBLANE, D_MODEL), F32).at[:bsz].set(c).at[bsz].set(c_ctx)
    mod = _modulation(cond, w_mod[0], b_mod[0][None, :])
    mod3 = mod.reshape(SUBLANE, 6, D_MODEL)

    weights = _prep_weights(g_attn[0], w_in[0], g_q_a[0], w_uq[0], g_kv_a[0], w_ukv[0],
                            g_mla_q[0], g_mla_k[0], g_swa_q[0], g_swa_k[0])
    tables = _rope_tables(seq)

    weights["to_cast"] = [w_out[0], w_mlp1[0], w_mlp2[0]]
    (km_l, vmt_l, ks_l, vst_l, qmt, qst,
     w_out_bf16, w_mlp1_bf16, w_mlp2_bf16) = _project(x, mod3, None, weights, tables, latent=True, tile=PROJ_TILE)
    km_c, vmt_c, ks_c, vst_c = _project(ctx, mod3, bsz, weights, None, latent=False, tile=TOK_TILE)

    mix_m = _mla_attention(qmt, km_c, km_l, vmt_c, vmt_l)
    mix_s = _swa_attention(swa_sink[0], qst, ks_c, ks_l, vst_c, vst_l)

    return _outproj_mlp(x, mix_m, mix_s, mod3, w_out_bf16, g_mlp[0][None, :], w_mlp1_bf16, w_mlp2_bf16)
```

```python
import functools
import math

import jax
import jax.numpy as jnp
import numpy as np
from jax import lax
from jax.experimental import pallas as pl
from jax.experimental.pallas import tpu as pltpu

D_MODEL = 1024
GRID_W = 64
MLA_HEADS = 8
MLA_NOPE = 64
MLA_ROPE = 32
MLA_QK = MLA_NOPE + MLA_ROPE
MLA_V = 64
MLA_V_AUG = MLA_V + 16
Q_LORA = 256
KV_LORA = 128
SWA_HEADS = 8
SWA_KV_HEADS = 2
SWA_GROUP = SWA_HEADS // SWA_KV_HEADS
SWA_HD = 64
SWA_V_AUG = SWA_HD + 16
WINDOW = 128
D_FF = 4 * D_MODEL
ROPE_THETA = 10000.0
EPS = 1e-6
NEG_INF = -1e30
LOG2E = 1.4426950408889634
MLA_QSCALE = LOG2E / math.sqrt(MLA_QK)
SWA_QSCALE = LOG2E / math.sqrt(SWA_HD)

LANE = 128
SUBLANE = 8
MXU_DIM = 256
TOK_TILE = MXU_DIM
PROJ_TILE = 1024
PROJ_PART = 256
MOD_TILE = 1024
MLP_TILE = 1024
FF_CHUNK = 1024
SWA_WIN = 2 * TOK_TILE
MLA_SUB = MXU_DIM
MLA_Q_TILES = 4
MLA_STAGE_HEADS = 2
SWA_SUB = MXU_DIM
SWA_STAGE_TILES = 2
MLA_CHUNK = 32
SWA_CHUNK = 32
MLA_QK_LEAD = 3
SWA_QK_LEAD = 1
VMEM_LIMIT = 56 * 1024 * 1024

F32 = jnp.float32
BF16 = jnp.bfloat16
NT_DIMS = (((1,), (1,)), ((), ()))
NEXT_STEP = "next-step"


def _dot(a, b):
    return jnp.dot(a, b, preferred_element_type=F32)


def _dot_nt(a, b):
    return lax.dot_general(a, b, NT_DIMS, preferred_element_type=F32)


def _mod_kernel(cond_ref, w_ref, b_ref, o_ref):
    cnd = cond_ref[...]
    act = cnd * jax.nn.sigmoid(cnd)
    o_ref[...] = _dot(act.astype(BF16), w_ref[...].astype(BF16)) + b_ref[...]


def _modulation(cond, w_mod, b_mod):
    n = w_mod.shape[1]
    tn = MOD_TILE
    return pl.pallas_call(
        _mod_kernel,
        grid=(n // tn,),
        in_specs=[
            pl.BlockSpec((SUBLANE, D_MODEL), lambda i: (0, 0)),
            pl.BlockSpec((D_MODEL, tn), lambda i: (0, i)),
            pl.BlockSpec((1, tn), lambda i: (0, i)),
        ],
        out_specs=pl.BlockSpec((SUBLANE, tn), lambda i: (0, i)),
        out_shape=jax.ShapeDtypeStruct((SUBLANE, n), F32),
        name="modulation",
    )(cond, w_mod, b_mod)


def _rms_rows(x, n):
    return x * lax.rsqrt(jnp.sum(x * x, axis=-1, keepdims=True) * (1.0 / n) + EPS)


def _rms_cols(x, n):
    return x * lax.rsqrt(jnp.sum(x * x, axis=0, keepdims=True) * (1.0 / n) + EPS)


def _proj_kernel(*refs, latent):
    if latent:
        (x_ref, mod_ref, gattn_ref, wtok_ref, wvs_ref, gkva_ref, wuk_ref, wuv_ref, gmk_ref, gsk_ref,
         wqs_ref, gqa_ref, wuq_ref, gmq_ref, gsq_ref,
         mcos_ref, msin_ref, scos_ref, ssin_ref, kc_ref, ks1_ref, ks2_ref, sc_ref, ss1_ref, ss2_ref,
         wo_f32_ref, w1_f32_ref, w2_f32_ref,
         km_ref, vmt_ref, ks_ref, vst_ref, qmt_ref, qst_ref,
         wo_bf16_ref, w1_bf16_ref, w2_bf16_ref) = refs
        for src, dst in ((wo_f32_ref, wo_bf16_ref), (w1_f32_ref, w1_bf16_ref), (w2_f32_ref, w2_bf16_ref)):
            dst[...] = src[...].astype(BF16)
    else:
        (x_ref, mod_ref, gattn_ref, wtok_ref, wvs_ref, gkva_ref, wuk_ref, wuv_ref, gmk_ref, gsk_ref,
         km_ref, vmt_ref, ks_ref, vst_ref) = refs

    tile = x_ref.shape[1]
    n_part = max(1, tile // PROJ_PART)
    part = tile // n_part

    gain = gattn_ref[...] * (1.0 + mod_ref[0, 1:2, :])
    shift = mod_ref[0, 0:1, :]
    gmk = gmk_ref[...]
    ones_row = (lax.broadcasted_iota(jnp.int32, (MLA_V_AUG - MLA_V, part), 0) == 0).astype(BF16)
    if latent:
        gmq = gmq_ref[...] * MLA_QSCALE
        gsq = gsq_ref[...] * SWA_QSCALE

    def part_phases(lo):
        rows = slice(lo, lo + part)

        h = (_rms_rows(x_ref[0, rows, :], D_MODEL) * gain + shift).astype(BF16)
        yield

        zt = _dot(h, wtok_ref[...])
        vst = _dot_nt(wvs_ref[...], h)
        if latent:
            qst = _dot_nt(wqs_ref[...], h)
        yield

        cq = zt[:, 0:256]
        ckv = zt[:, 256:384]
        krp = zt[:, 384:512]
        ksw = zt[:, 512:640]
        ckvn = (_rms_rows(ckv, KV_LORA) * gkva_ref[...]).astype(BF16)
        if latent:
            cqn = (_rms_rows(cq, Q_LORA) * gqa_ref[...]).astype(BF16)
        kr = krp * gmk
        if latent:
            kr = (kr * kc_ref[rows, :] + pltpu.roll(kr, 112, 1) * ks1_ref[rows, :]
                  + pltpu.roll(kr, 16, 1) * ks2_ref[rows, :])
        ss_kr = jnp.sum(krp * krp, axis=-1, keepdims=True)
        first = lax.broadcasted_iota(jnp.int32, ksw.shape, 1) < SWA_HD
        sq = ksw * ksw
        ss0 = jnp.sum(jnp.where(first, sq, 0.0), axis=-1, keepdims=True)
        ss1 = jnp.sum(jnp.where(first, 0.0, sq), axis=-1, keepdims=True)
        rr = jnp.where(first, lax.rsqrt(ss0 * (1.0 / SWA_HD) + EPS), lax.rsqrt(ss1 * (1.0 / SWA_HD) + EPS))
        ksn = ksw * rr * gsk_ref[...]
        if latent:
            ksn = (ksn * sc_ref[rows, :] + pltpu.roll(ksn, 96, 1) * ss1_ref[rows, :]
                   + pltpu.roll(ksn, 32, 1) * ss2_ref[rows, :])
        ks_ref[0, rows, :] = ksn.astype(BF16)
        for g in range(SWA_KV_HEADS):
            for t in range(part // LANE):
                dst = lo // LANE + t
                vst_ref[0, g, dst, 0:SWA_HD, :] = vst[g * SWA_HD:(g + 1) * SWA_HD, t * LANE:(t + 1) * LANE].astype(BF16)
                vst_ref[0, g, dst, SWA_HD:SWA_V_AUG, :] = ones_row[:, 0:LANE]
        yield

        kpre = _dot(ckvn, wuk_ref[...])
        vt = _dot_nt(wuv_ref[...], ckvn)
        if latent:
            qt = _dot_nt(wuq_ref[...], cqn)
        yield

        for hd in range(MLA_HEADS):
            kp = kpre[:, hd * LANE:(hd + 1) * LANE]
            ss = jnp.sum(kp * kp, axis=-1, keepdims=True) + ss_kr
            kn = (kp * gmk + kr) * lax.rsqrt(ss * (1.0 / MLA_QK) + EPS)
            km_ref[0, hd, rows, :] = kn.astype(BF16)
            vmt_ref[0, hd, 0:MLA_V, rows] = vt[hd * MLA_V:(hd + 1) * MLA_V].astype(BF16)
            vmt_ref[0, hd, MLA_V:MLA_V_AUG, rows] = ones_row
        if latent:
            mcos = mcos_ref[:, rows]
            msin = msin_ref[:, rows]
            for hd in range(MLA_HEADS):
                qn = _rms_cols(qt[hd * MLA_QK:(hd + 1) * MLA_QK], MLA_QK) * gmq
                x1 = qn[64:80]
                x2 = qn[80:96]
                qt_idx = lo // TOK_TILE
                qmt_ref[0, hd, qt_idx, 0:64, :] = qn[0:64].astype(BF16)
                qmt_ref[0, hd, qt_idx, 64:80, :] = (x1 * mcos - x2 * msin).astype(BF16)
                qmt_ref[0, hd, qt_idx, 80:96, :] = (x2 * mcos + x1 * msin).astype(BF16)
                qmt_ref[0, hd, qt_idx, 96:128, :] = jnp.zeros((32, part), BF16)
            scos = scos_ref[:, rows]
            ssin = ssin_ref[:, rows]
            for hd in range(SWA_HEADS):
                qn = _rms_cols(qst[hd * SWA_HD:(hd + 1) * SWA_HD], SWA_HD) * gsq
                x1 = qn[0:32]
                x2 = qn[32:64]
                g = hd // SWA_GROUP
                base = g * SWA_HD
                other = (1 - g) * SWA_HD
                qst_ref[0, hd, base:base + 32, rows] = (x1 * scos - x2 * ssin).astype(BF16)
                qst_ref[0, hd, base + 32:base + 64, rows] = (x2 * scos + x1 * ssin).astype(BF16)
                qst_ref[0, hd, other:other + 64, rows] = jnp.zeros((64, part), BF16)
        yield

    n_phase = 5
    parts = [part_phases(i * part) for i in range(n_part)]
    for step in range(n_phase + n_part - 1):
        for i, gen in enumerate(parts):
            if 0 <= step - i < n_phase:
                next(gen)


def _full(shape):
    nd = len(shape)
    return pl.BlockSpec(shape, lambda b, j: (0,) * nd)


def _project(x, mod3, mod_row, weights, tables, latent, tile):
    bsz, n, _ = x.shape
    nt = n // tile
    if mod_row is None:
        mod_map = lambda b, j: (b, 0, 0)
    else:
        mod_map = lambda b, j: (mod_row, 0, 0)
    common = [weights[k] for k in ("gattn", "wtok", "wvs", "gkva", "wuk", "wuv", "gmk", "gsk")]
    ins = [x, mod3] + common
    in_specs = [
        pl.BlockSpec((1, tile, D_MODEL), lambda b, j: (b, j, 0)),
        pl.BlockSpec((1, 6, D_MODEL), mod_map),
    ] + [_full(w.shape) for w in common]
    out_shape = [
        jax.ShapeDtypeStruct((bsz, MLA_HEADS, n, LANE), BF16),
        jax.ShapeDtypeStruct((bsz, MLA_HEADS, MLA_V_AUG, n), BF16),
        jax.ShapeDtypeStruct((bsz, n, LANE), BF16),
        jax.ShapeDtypeStruct((bsz, SWA_KV_HEADS, n // LANE, SWA_V_AUG, LANE), BF16),
    ]
    out_specs = [
        pl.BlockSpec((1, MLA_HEADS, tile, LANE), lambda b, j: (b, 0, j, 0)),
        pl.BlockSpec((1, MLA_HEADS, MLA_V_AUG, tile), lambda b, j: (b, 0, 0, j)),
        pl.BlockSpec((1, tile, LANE), lambda b, j: (b, j, 0)),
        pl.BlockSpec((1, SWA_KV_HEADS, tile // LANE, SWA_V_AUG, LANE), lambda b, j: (b, 0, j, 0, 0)),
    ]
    if latent:
        extra = [weights[k] for k in ("wqs", "gqa", "wuq", "gmq", "gsq")]
        ins += extra
        in_specs += [_full(w.shape) for w in extra]
        feat = [tables[k] for k in ("mcos", "msin", "scos", "ssin")]
        ins += feat
        in_specs += [pl.BlockSpec((t.shape[0], tile), lambda b, j: (0, j)) for t in feat]
        tok = [tables[k] for k in ("kc", "ks1", "ks2", "sc", "ss1", "ss2")]
        ins += tok
        in_specs += [pl.BlockSpec((tile, LANE), lambda b, j: (j, 0)) for _ in tok]
        assert PROJ_PART == TOK_TILE
        out_shape += [
            jax.ShapeDtypeStruct((bsz, MLA_HEADS, n // TOK_TILE, LANE, TOK_TILE), BF16),
            jax.ShapeDtypeStruct((bsz, SWA_HEADS, LANE, n), BF16),
        ]
        out_specs += [
            pl.BlockSpec((1, MLA_HEADS, tile // TOK_TILE, LANE, TOK_TILE), lambda b, j: (b, 0, j, 0, 0)),
            pl.BlockSpec((1, SWA_HEADS, LANE, tile), lambda b, j: (b, 0, 0, j)),
        ]
        steps = bsz * nt
        for w in weights["to_cast"]:
            rows = w.shape[0] // steps
            assert rows * steps == w.shape[0] and rows % 16 == 0
            ins.append(w)
            in_specs.append(pl.BlockSpec((rows, w.shape[1]), lambda b, j: (b * nt + j, 0)))
            out_shape.append(jax.ShapeDtypeStruct(w.shape, BF16))
            out_specs.append(pl.BlockSpec((rows, w.shape[1]), lambda b, j: (b * nt + j, 0)))
    return pl.pallas_call(
        functools.partial(_proj_kernel, latent=latent),
        grid=(bsz, nt),
        in_specs=in_specs,
        out_specs=out_specs,
        out_shape=out_shape,
        compiler_params=pltpu.CompilerParams(vmem_limit_bytes=VMEM_LIMIT),
        name="project_latent" if latent else "project_context",
    )(*ins)


def _mla_kernel(qt_ref, qtn_ref, kc_ref, kl_ref, vct_ref, vlt_ref, o_ref, ot_scr, s_scr, m8_scr, *, seq):
    step = pl.program_id(1)
    n_step = pl.num_programs(1)
    n_ctx = kc_ref.shape[2]
    tq = TOK_TILE
    key_subs = ([(kc_ref, vct_ref, o) for o in range(0, n_ctx, MLA_SUB)]
                + [(kl_ref, vlt_ref, o) for o in range(0, seq, MLA_SUB)])
    n_key = len(key_subs)
    subs = [(h, j) for h in range(MLA_STAGE_HEADS) for j in range(n_key)]
    groups = MLA_HEADS // MLA_STAGE_HEADS

    def scores_into(slot, idx, stage, m8):
        h, j = subs[idx]
        if stage is NEXT_STEP:
            hd, q = h, qtn_ref[0, h, 0]
        else:
            t, grp = stage
            hd = grp * MLA_STAGE_HEADS + h
            q = qt_ref[0, hd, t]
        k_ref, _, off = key_subs[j]
        sj = _dot(k_ref[0, hd, off:off + MLA_SUB, :], q)
        s_scr[slot, idx * MLA_SUB:(idx + 1) * MLA_SUB, :] = sj
        mj = jnp.max(sj.reshape(MLA_SUB // SUBLANE, SUBLANE, tq), axis=0)
        m8 = list(m8)
        m8[h] = mj if j == 0 else jnp.maximum(m8[h], mj)
        return tuple(m8)

    def stage_body(stage, m8_cur, slot, nxt):
        t, grp = stage
        n_q = len(subs) if nxt is not None else 0
        m8_next = (None,) * MLA_STAGE_HEADS
        for idx in range(min(MLA_QK_LEAD, n_q)):
            m8_next = scores_into(1 - slot, idx, nxt, m8_next)
        for idx, (h, j) in enumerate(subs):
            hd = grp * MLA_STAGE_HEADS + h
            if j == 0:
                mb = jnp.broadcast_to(jnp.max(m8_cur[h], axis=0, keepdims=True), (MLA_CHUNK, tq))
                acc = None
            base = idx * MLA_SUB
            p = jnp.concatenate(
                [jnp.exp2(s_scr[slot, r:r + MLA_CHUNK, :] - mb).astype(BF16)
                 for r in range(base, base + MLA_SUB, MLA_CHUNK)], axis=0)
            _, v_ref, off = key_subs[j]
            d = _dot(v_ref[0, hd, :, off:off + MLA_SUB], p)
            acc = d if acc is None else acc + d
            if j == n_key - 1:
                row = pl.multiple_of(hd * MLA_V, MLA_V)
                ot_scr[t, pl.ds(row, MLA_V), :] = acc[0:MLA_V] / acc[MLA_V:MLA_V + 1]
            if idx + MLA_QK_LEAD < n_q:
                m8_next = scores_into(1 - slot, idx + MLA_QK_LEAD, nxt, m8_next)
        return m8_next

    def stage_of(n):
        return n // groups, lax.rem(n, groups)

    def stage_step(n, m8):
        return lax.cond(lax.rem(n, 2) == 0,
                        lambda m: stage_body(stage_of(n), m, 0, stage_of(n + 1)),
                        lambda m: stage_body(stage_of(n), m, 1, stage_of(n + 1)), m8)

    n_stage = groups * MLA_Q_TILES
    assert n_stage % 2 == 0
    last_stage = (MLA_Q_TILES - 1, groups - 1)

    def save_m8(m8):
        for h in range(MLA_STAGE_HEADS):
            m8_scr[h] = m8[h]

    @pl.when(step == 0)
    def _():
        m8 = (None,) * MLA_STAGE_HEADS
        for idx in range(len(subs)):
            m8 = scores_into(0, idx, (0, 0), m8)
        save_m8(m8)

    m8_first = tuple(m8_scr[h] for h in range(MLA_STAGE_HEADS))
    m8_last = lax.fori_loop(0, n_stage - 1, stage_step, m8_first)

    @pl.when(step < n_step - 1)
    def _():
        save_m8(stage_body(last_stage, m8_last, 1, NEXT_STEP))

    @pl.when(step == n_step - 1)
    def _():
        stage_body(last_stage, m8_last, 1, None)
    for t in range(MLA_Q_TILES):
        o_ref[0, t * tq:(t + 1) * tq, :] = ot_scr[t].T.astype(BF16)


def _mla_attention(qmt, km_c, km_l, vmt_c, vmt_l):
    bsz = qmt.shape[0]
    s = km_l.shape[2]
    n_ctx = km_c.shape[2]
    step_q = MLA_Q_TILES * TOK_TILE
    assert s % step_q == 0
    n_keys = n_ctx + s
    return pl.pallas_call(
        functools.partial(_mla_kernel, seq=s),
        grid=(bsz, s // step_q),
        in_specs=[
            pl.BlockSpec((1, MLA_HEADS, MLA_Q_TILES, LANE, TOK_TILE), lambda b, j: (b, 0, j, 0, 0)),
            pl.BlockSpec((1, MLA_STAGE_HEADS, 1, LANE, TOK_TILE),
                         lambda b, j: (b, 0, jnp.minimum((j + 1) * MLA_Q_TILES, s // TOK_TILE - 1), 0, 0)),
            pl.BlockSpec((1, MLA_HEADS, n_ctx, LANE), lambda b, j: (b, 0, 0, 0)),
            pl.BlockSpec((1, MLA_HEADS, s, LANE), lambda b, j: (b, 0, 0, 0)),
            pl.BlockSpec((1, MLA_HEADS, MLA_V_AUG, n_ctx), lambda b, j: (b, 0, 0, 0)),
            pl.BlockSpec((1, MLA_HEADS, MLA_V_AUG, s), lambda b, j: (b, 0, 0, 0)),
        ],
        out_specs=pl.BlockSpec((1, step_q, MLA_HEADS * MLA_V), lambda b, j: (b, j, 0)),
        out_shape=jax.ShapeDtypeStruct((bsz, s, MLA_HEADS * MLA_V), BF16),
        scratch_shapes=[
            pltpu.VMEM((MLA_Q_TILES, MLA_HEADS * MLA_V, TOK_TILE), F32),
            pltpu.VMEM((2, MLA_STAGE_HEADS * n_keys, TOK_TILE), F32),
            pltpu.VMEM((MLA_STAGE_HEADS, SUBLANE, TOK_TILE), F32),
        ],
        compiler_params=pltpu.CompilerParams(vmem_limit_bytes=VMEM_LIMIT,
                                             dimension_semantics=("arbitrary", "arbitrary")),
        name="mla_attention",
    )(qmt, qmt, km_c, km_l, vmt_c, vmt_l)


def _swa_kernel(sink_ref, bias_ref, qt_ref, qtn_ref, kc_ref, kl_ref, vct_ref, vlt_ref, o_ref,
                ot_scr, s_scr, m8_scr, *, seq):
    j = pl.program_id(1)
    nt = seq // TOK_TILE
    n_stage = nt // SWA_STAGE_TILES
    n_ctx = kc_ref.shape[1]
    tq = TOK_TILE
    n_band = SWA_WIN // SWA_SUB
    n_t = n_band + n_ctx // SWA_SUB
    lanes_per_sub = SWA_SUB // LANE
    subs = [(ti, hd, t) for ti in range(SWA_STAGE_TILES) for hd in range(SWA_HEADS) for t in range(n_t)]

    def window(jj):
        start = pl.multiple_of(jnp.clip(jj * TOK_TILE - WINDOW, 0, seq - SWA_WIN), LANE)
        variant = jnp.where(jj == 0, 0, jnp.where(jj == nt - 1, 2, 1))
        return start, variant

    def rows_of(idx):
        return slice(idx * SWA_SUB, (idx + 1) * SWA_SUB)

    def windows(stage):
        return [window(stage * SWA_STAGE_TILES + ti) for ti in range(SWA_STAGE_TILES)]

    def scores_into(slot, idx, q_ref, wins):
        ti, hd, t = subs[idx]
        start, variant = wins[ti]
        if t < n_band:
            k = kl_ref[0, pl.ds(start + t * SWA_SUB, SWA_SUB), :]
        else:
            k = kc_ref[0, (t - n_band) * SWA_SUB:(t - n_band + 1) * SWA_SUB, :]
        sj = _dot(k, q_ref[0, hd, :, ti * tq:(ti + 1) * tq])
        if t < n_band:
            sj = sj + bias_ref[variant, t * SWA_SUB:(t + 1) * SWA_SUB, :]
        s_scr[slot, rows_of(idx), :] = sj
        return jnp.max(sj.reshape(SWA_SUB // SUBLANE, SUBLANE, tq), axis=0)

    def values_t(g, t, t0):
        if t < n_band:
            tiles = [vlt_ref[0, g, t0 + t * lanes_per_sub + i] for i in range(lanes_per_sub)]
        else:
            tiles = [vct_ref[0, g, (t - n_band) * lanes_per_sub + i] for i in range(lanes_per_sub)]
        return jnp.concatenate(tiles, axis=1)

    def next_scores(slot, idx, nxt, m8n):
        q_ref, wins_n = nxt
        mj = scores_into(slot, idx, q_ref, wins_n)
        ti, hd, t = subs[idx]
        key = ti * SWA_HEADS + hd
        m8n[key] = mj if t == 0 else jnp.maximum(m8n[key], mj)
        if t == n_t - 1:
            m8_scr[slot, key] = m8n[key]

    def stage_body(slot, wins_cur, nxt):
        n_q = len(subs) if nxt is not None else 0
        m8n = {}
        for idx in range(min(SWA_QK_LEAD, n_q)):
            next_scores(1 - slot, idx, nxt, m8n)
        for idx, (ti, hd, t) in enumerate(subs):
            g = hd // SWA_GROUP
            t0 = wins_cur[ti][0] // LANE
            if t == 0:
                sk = sink_ref[hd] * LOG2E
                m8 = m8_scr[slot, ti * SWA_HEADS + hd]
                m = jnp.maximum(jnp.max(m8, axis=0, keepdims=True), sk)
                mb = jnp.broadcast_to(m, (SWA_CHUNK, tq))
                acc = None
            base = idx * SWA_SUB
            p = jnp.concatenate(
                [jnp.exp2(s_scr[slot, r:r + SWA_CHUNK, :] - mb).astype(BF16)
                 for r in range(base, base + SWA_SUB, SWA_CHUNK)], axis=0)
            d = _dot(values_t(g, t, t0), p)
            acc = d if acc is None else acc + d
            if t == n_t - 1:
                denom = acc[SWA_HD:SWA_HD + 1] + jnp.exp2(sk - m)
                ot_scr[ti, hd * SWA_HD:(hd + 1) * SWA_HD, :] = acc[0:SWA_HD] / denom
            if idx + SWA_QK_LEAD < n_q:
                next_scores(1 - slot, idx + SWA_QK_LEAD, nxt, m8n)

    wins = windows(j)

    @pl.when(j == 0)
    def _():
        m8n = {}
        for idx in range(len(subs)):
            next_scores(0, idx, (qt_ref, wins), m8n)

    for parity in range(2):
        @pl.when((lax.rem(j, 2) == parity) & (j < n_stage - 1))
        def _():
            stage_body(parity, wins, (qtn_ref, windows(j + 1)))

    @pl.when(j == n_stage - 1)
    def _():
        stage_body((n_stage - 1) % 2, wins, None)

    for ti in range(SWA_STAGE_TILES):
        o_ref[0, ti * tq:(ti + 1) * tq, :] = ot_scr[ti].T.astype(BF16)


def _swa_bias(seq):
    nt = seq // TOK_TILE
    out = []
    for j in (0, 1, nt - 1):
        start = min(max(j * TOK_TILE - WINDOW, 0), seq - SWA_WIN)
        qpos = j * TOK_TILE + np.arange(TOK_TILE)[None, :]
        kpos = start + np.arange(SWA_WIN)[:, None]
        out.append(np.where(np.abs(qpos - kpos) <= WINDOW, 0.0, NEG_INF))
    return jnp.asarray(np.stack(out), F32)


def _swa_attention(sink, qst, ks_c, ks_l, vst_c, vst_l):
    bsz, _, _, s = qst.shape
    n_ctx = ks_c.shape[1]
    nt = s // TOK_TILE
    assert nt >= 3 and s >= SWA_WIN and n_ctx % SWA_SUB == 0
    assert nt % SWA_STAGE_TILES == 0
    n_stage = nt // SWA_STAGE_TILES
    step_q = SWA_STAGE_TILES * TOK_TILE
    stage_rows = SWA_STAGE_TILES * SWA_HEADS * (SWA_WIN + n_ctx)
    return pl.pallas_call(
        functools.partial(_swa_kernel, seq=s),
        grid=(bsz, n_stage),
        in_specs=[
            pl.BlockSpec(memory_space=pltpu.SMEM),
            pl.BlockSpec((3, SWA_WIN, TOK_TILE), lambda b, j: (0, 0, 0)),
            pl.BlockSpec((1, SWA_HEADS, LANE, step_q), lambda b, j: (b, 0, 0, j)),
            pl.BlockSpec((1, SWA_HEADS, LANE, step_q), lambda b, j: (b, 0, 0, jnp.minimum(j + 1, n_stage - 1))),
            pl.BlockSpec((1, n_ctx, LANE), lambda b, j: (b, 0, 0)),
            pl.BlockSpec((1, s, LANE), lambda b, j: (b, 0, 0)),
            pl.BlockSpec((1, SWA_KV_HEADS, n_ctx // LANE, SWA_V_AUG, LANE), lambda b, j: (b, 0, 0, 0, 0)),
            pl.BlockSpec((1, SWA_KV_HEADS, s // LANE, SWA_V_AUG, LANE), lambda b, j: (b, 0, 0, 0, 0)),
        ],
        out_specs=pl.BlockSpec((1, step_q, SWA_HEADS * SWA_HD), lambda b, j: (b, j, 0)),
        out_shape=jax.ShapeDtypeStruct((bsz, s, SWA_HEADS * SWA_HD), BF16),
        scratch_shapes=[
            pltpu.VMEM((SWA_STAGE_TILES, SWA_HEADS * SWA_HD, TOK_TILE), F32),
            pltpu.VMEM((2, stage_rows, TOK_TILE), F32),
            pltpu.VMEM((2, SWA_STAGE_TILES * SWA_HEADS, SUBLANE, TOK_TILE), F32),
        ],
        compiler_params=pltpu.CompilerParams(vmem_limit_bytes=VMEM_LIMIT,
                                             dimension_semantics=("arbitrary", "arbitrary")),
        name="swa_attention",
    )(sink, _swa_bias(s), qst, qst, ks_c, ks_l, vst_c, vst_l)


def _mlp_kernel(x_ref, mm_ref, ms_ref, mod_ref, wo_ref, gmlp_ref, w1_ref, w2_ref, o_ref, acc_ref):
    half = MLA_HEADS * MLA_V
    y1 = _dot(mm_ref[0], wo_ref[0:half, :]) + _dot(ms_ref[0], wo_ref[half:, :])
    x1 = x_ref[0] + mod_ref[0, 2:3, :] * y1
    h2 = (_rms_rows(x1, D_MODEL) * gmlp_ref[...] * (1.0 + mod_ref[0, 4:5, :])
          + mod_ref[0, 3:4, :]).astype(BF16)
    for c in range(D_FF // FF_CHUNK):
        a = jnp.maximum(_dot(h2, w1_ref[:, c * FF_CHUNK:(c + 1) * FF_CHUNK]), 0.0)
        part = _dot((a * a).astype(BF16), w2_ref[c * FF_CHUNK:(c + 1) * FF_CHUNK, :])
        if c == 0:
            acc_ref[...] = part
        else:
            acc_ref[...] += part
    o_ref[0] = x1 + mod_ref[0, 5:6, :] * acc_ref[...]


def _outproj_mlp(x, mix_m, mix_s, mod3, w_out, g_mlp, w1, w2):
    bsz, s, _ = x.shape
    half = MLA_HEADS * MLA_V
    const = lambda b, j: (0, 0)
    single = pl.Buffered(1)
    return pl.pallas_call(
        _mlp_kernel,
        grid=(bsz, s // MLP_TILE),
        in_specs=[
            pl.BlockSpec((1, MLP_TILE, D_MODEL), lambda b, j: (b, j, 0)),
            pl.BlockSpec((1, MLP_TILE, half), lambda b, j: (b, j, 0)),
            pl.BlockSpec((1, MLP_TILE, half), lambda b, j: (b, j, 0)),
            pl.BlockSpec((1, 6, D_MODEL), lambda b, j: (b, 0, 0)),
            pl.BlockSpec((D_MODEL, D_MODEL), const, pipeline_mode=single),
            pl.BlockSpec((1, D_MODEL), const),
            pl.BlockSpec((D_MODEL, D_FF), const, pipeline_mode=single),
            pl.BlockSpec((D_FF, D_MODEL), const, pipeline_mode=single),
        ],
        out_specs=pl.BlockSpec((1, MLP_TILE, D_MODEL), lambda b, j: (b, j, 0)),
        out_shape=jax.ShapeDtypeStruct((bsz, s, D_MODEL), F32),
        scratch_shapes=[pltpu.VMEM((MLP_TILE, D_MODEL), F32)],
        compiler_params=pltpu.CompilerParams(vmem_limit_bytes=VMEM_LIMIT),
        name="outproj_mlp",
    )(x, mix_m, mix_s, mod3, w_out, g_mlp, w1, w2)


def _rope_tables(seq):
    f32 = np.float32

    def cos_sin(rot_dim):
        n_freq = rot_dim // 4
        inv = f32(ROPE_THETA) ** (-np.arange(n_freq, dtype=f32) / f32(n_freq))
        rows = seq // GRID_W
        row = np.repeat(np.arange(rows, dtype=f32), GRID_W)
        col = np.tile(np.arange(GRID_W, dtype=f32), rows)
        ang = np.concatenate([row[:, None] * inv, col[:, None] * inv], axis=-1).astype(f32)
        return np.cos(ang).astype(f32), np.sin(ang).astype(f32)

    mcos, msin = cos_sin(MLA_ROPE)
    scos, ssin = cos_sin(SWA_HD)
    z16 = np.zeros((seq, 16), f32)
    z32 = np.zeros((seq, 32), f32)
    z64 = np.zeros((seq, 64), f32)
    kc = np.concatenate([np.ones((seq, 64), f32), mcos, mcos, np.ones((seq, 32), f32)], axis=1)
    ks1 = np.concatenate([z64, -msin, z16, z32], axis=1)
    ks2 = np.concatenate([z64, z16, msin, z32], axis=1)
    sc = np.concatenate([scos, scos, scos, scos], axis=1)
    ss1 = np.concatenate([-ssin, z32, -ssin, z32], axis=1)
    ss2 = np.concatenate([z32, ssin, z32, ssin], axis=1)
    tabs = dict(mcos=mcos.T, msin=msin.T, scos=scos.T, ssin=ssin.T,
                kc=kc, ks1=ks1, ks2=ks2, sc=sc, ss1=ss1, ss2=ss2)
    return {k: jnp.asarray(np.ascontiguousarray(v)) for k, v in tabs.items()}


def _prep_weights(g_attn, w_in, g_q_a, w_uq, g_kv_a, w_ukv, g_mla_q, g_mla_k, g_swa_q, g_swa_k):
    o_ckv = Q_LORA
    o_kr = o_ckv + KV_LORA
    o_qs = o_kr + MLA_ROPE
    o_ks = o_qs + SWA_HEADS * SWA_HD
    o_vs = o_ks + SWA_KV_HEADS * SWA_HD
    w_cq = w_in[:, :o_ckv]
    w_ckv = w_in[:, o_ckv:o_kr]
    w_kr = w_in[:, o_kr:o_qs]
    w_qs = w_in[:, o_qs:o_ks]
    w_ks = w_in[:, o_ks:o_vs]
    w_vs = w_in[:, o_vs:]
    w_kr_p = jnp.pad(w_kr, ((0, 0), (MLA_NOPE, LANE - MLA_QK)))
    wtok = jnp.concatenate([w_cq, w_ckv, w_kr_p, w_ks], axis=1).astype(BF16)
    w_ukv_h = w_ukv.reshape(KV_LORA, MLA_HEADS, MLA_NOPE + MLA_V)
    wuk = jnp.pad(w_ukv_h[:, :, :MLA_NOPE], ((0, 0), (0, 0), (0, LANE - MLA_NOPE)))
    wuk = wuk.reshape(KV_LORA, MLA_HEADS * LANE).astype(BF16)
    wuv = w_ukv_h[:, :, MLA_NOPE:].reshape(KV_LORA, MLA_HEADS * MLA_V).T.astype(BF16)
    wuq = w_uq.T.astype(BF16)
    return dict(
        gattn=g_attn[None, :], wtok=wtok, wvs=w_vs.T.astype(BF16), gkva=g_kv_a[None, :],
        wuk=wuk, wuv=wuv, gmk=jnp.pad(g_mla_k, (0, LANE - MLA_QK))[None, :],
        gsk=jnp.tile(g_swa_k, SWA_KV_HEADS)[None, :],
        wqs=w_qs.T.astype(BF16), gqa=g_q_a[None, :], wuq=wuq,
        gmq=g_mla_q[:, None], gsq=g_swa_q[:, None],
    )


def kernel(x, c, ctx, c_ctx, w_mod, b_mod, g_attn, w_in, g_q_a, w_uq, g_kv_a, w_ukv, g_mla_q, g_mla_k,
           g_swa_q, g_swa_k, swa_sink, w_out, g_mlp, w_mlp1, w_mlp2):
    bsz, seq, _ = x.shape
    assert w_mod.shape[0] == 1, "single-layer block"
    assert bsz < SUBLANE and seq % PROJ_TILE == 0 and seq % MLP_TILE == 0 and ctx.shape[1] % TOK_TILE == 0

    cond = jnp.zeros((SUBLANE, D_MODEL), F32).at[:bsz].set(c).at[bsz].set(c_ctx)
    mod = _modulation(cond, w_mod[0], b_mod[0][None, :])
    mod3 = mod.reshape(SUBLANE, 6, D_MODEL)

    weights = _prep_weights(g_attn[0], w_in[0], g_q_a[0], w_uq[0], g_kv_a[0], w_ukv[0],
                            g_mla_q[0], g_mla_k[0], g_swa_q[0], g_swa_k[0])
    tables = _rope_tables(seq)

    weights["to_cast"] = [w_out[0], w_mlp1[0], w_mlp2[0]]
    (km_l, vmt_l, ks_l, vst_l, qmt, qst,
     w_out_bf16, w_mlp1_bf16, w_mlp2_bf16) = _project(x, mod3, None, weights, tables, latent=True, tile=PROJ_TILE)
    km_c, vmt_c, ks_c, vst_c = _project(ctx, mod3, bsz, weights, None, latent=False, tile=TOK_TILE)

    mix_m = _mla_attention(qmt, km_c, km_l, vmt_c, vmt_l)
    mix_s = _swa_attention(swa_sink[0], qst, ks_c, ks_l, vst_c, vst_l)

    return _outproj_mlp(x, mix_m, mix_s, mod3, w_out_bf16, g_mlp[0][None, :], w_mlp1_bf16, w_mlp2_bf16)
```
